```python
import math
import jax
import jax.numpy as jnp
from jax import lax
import numpy as np


D_MODEL = 1024
BATCH = 8
SEQ = 8192
DEPTH = 2
DEC_BATCH = 2
DEC_SEQ = 16384
PAST_LEN = 128

ROPE_THETA = 500000.0
NORM_EPS = 1e-6
NEG_INF = -1e30

A_HEADS = D_MODEL // 128
A_HEAD_DIM = 64
A_ROPE_DIM = A_HEAD_DIM // 4
A_PATTERNS = ((128, 1), (512, 4), (2048, 16))
A_BLOCK = 64
A_WIDTH = A_HEADS * A_HEAD_DIM

B_HEADS = D_MODEL // 128
B_NOPE_DIM = 64
B_ROPE_DIM = 32
B_QK_DIM = B_NOPE_DIM + B_ROPE_DIM
B_V_DIM = 64
B_Q_RANK = D_MODEL // 4
B_KV_RANK = D_MODEL // 8
B_QBLOCK = 128
B_WIDTH = B_HEADS * B_V_DIM

C_WIDTH = D_MODEL // 2
C_BLOCKS = 8
C_BLOCK_DIM = C_WIDTH // C_BLOCKS
C_CONV = 4
C_GATE_C = 8.0

D_HEADS = 4
D_HEAD_DIM = 128
D_WIDTH = D_HEADS * D_HEAD_DIM
D_CONV = 4
D_CHUNK = 64

N_BRANCH = 4
MIX_WIDTH = A_WIDTH + B_WIDTH + C_WIDTH + D_WIDTH
FF_DIM = ((8 * D_MODEL + 2) // 3 + 255) // 256 * 256

IN_SPLITS = (
    A_WIDTH, A_WIDTH, A_WIDTH,
    B_Q_RANK, B_KV_RANK, B_ROPE_DIM,
    C_WIDTH, C_WIDTH,
    D_WIDTH, D_WIDTH, D_WIDTH, D_WIDTH,
    2 * D_HEADS, 2 * D_HEADS,
    N_BRANCH * D_MODEL,
)
IN_DIM = sum(IN_SPLITS)

kernel_name = 'hybrid_parallel_bidir_encoder'


def rms_norm(x, g):
    xf = x.astype(jnp.float32)
    y = xf * lax.rsqrt(jnp.mean(jnp.square(xf), axis=-1, keepdims=True) + NORM_EPS)
    return (y * g.astype(jnp.float32)).astype(x.dtype)


def l2_norm(x):
    xf = x.astype(jnp.float32)
    return (xf * lax.rsqrt(jnp.sum(jnp.square(xf), axis=-1, keepdims=True) + NORM_EPS)).astype(x.dtype)


def rope_cos_sin(seq, dim):
    inv_freq = 1.0 / (ROPE_THETA ** (jnp.arange(0, dim, 2, dtype=jnp.float32) / dim))
    ang = jnp.arange(seq, dtype=jnp.float32)[:, None] * inv_freq[None, :]
    return jnp.cos(ang), jnp.sin(ang)


def apply_rope(x, cos, sin):
    xf = x.astype(jnp.float32)
    x1, x2 = jnp.split(xf, 2, axis=-1)
    c = cos[:, None, :]
    s = sin[:, None, :]
    return jnp.concatenate([x1 * c - x2 * s, x2 * c + x1 * s], axis=-1).astype(x.dtype)


def partial_rope(x, cos, sin, rope_dim):
    return jnp.concatenate([apply_rope(x[..., :rope_dim], cos, sin), x[..., rope_dim:]], axis=-1)


def centred_depthwise_conv(x, w):
    k_size = w.shape[0]
    left = k_size // 2
    right = k_size - 1 - left
    seq = x.shape[1]
    xp = jnp.pad(x, ((0, 0), (left, right), (0, 0)))
    out = xp[:, 0:seq] * w[0]
    for j in range(1, k_size):
        out = out + xp[:, j:j + seq] * w[j]
    return out


def dilated_attention(q, k, v):
    bsz, seq, heads, dh = q.shape
    scale = dh ** -0.5
    outs, lses = [], []
    for window, dil in A_PATTERNS:
        half = window // (2 * dil)
        n_side = -(-half // A_BLOCK)
        sub_len = seq // dil
        nb = -(-sub_len // A_BLOCK)
        lp = nb * A_BLOCK
        kw_len = (2 * n_side + 1) * A_BLOCK

        def to_sub(t):
            return t.reshape(bsz, sub_len, dil, heads, dh).transpose(0, 2, 1, 3, 4)

        qs = jnp.pad(to_sub(q), ((0, 0), (0, 0), (0, lp - sub_len), (0, 0), (0, 0)))
        qs = qs.reshape(bsz, dil, nb, A_BLOCK, heads, dh)
        kpad = ((0, 0), (0, 0), (n_side * A_BLOCK, lp - sub_len + n_side * A_BLOCK), (0, 0), (0, 0))
        kb = jnp.pad(to_sub(k), kpad).reshape(bsz, dil, nb + 2 * n_side, A_BLOCK, heads, dh)
        vb = jnp.pad(to_sub(v), kpad).reshape(bsz, dil, nb + 2 * n_side, A_BLOCK, heads, dh)
        kw = jnp.concatenate([kb[:, :, j:j + nb] for j in range(2 * n_side + 1)], axis=3)
        vw = jnp.concatenate([vb[:, :, j:j + nb] for j in range(2 * n_side + 1)], axis=3)
        qpos = jnp.arange(nb)[:, None] * A_BLOCK + jnp.arange(A_BLOCK)[None, :]
        kpos = jnp.arange(nb)[:, None] * A_BLOCK - n_side * A_BLOCK + jnp.arange(kw_len)[None, :]
        rel = kpos[:, None, :] - qpos[:, :, None]
        valid = (jnp.abs(rel) <= half) & (kpos[:, None, :] >= 0) & (kpos[:, None, :] < sub_len)
        s = jnp.einsum('brnqhd,brnkhd->brnhqk', qs, kw, preferred_element_type=jnp.float32) * scale
        s = jnp.where(valid[:, None], s, NEG_INF)
        m = jnp.max(s, axis=-1, keepdims=True)
        p = jnp.exp(s - m)
        z = jnp.sum(p, axis=-1, keepdims=True)
        o = jnp.einsum('brnhqk,brnkhd->brnqhd', (p / z).astype(v.dtype), vw)
        lse = (m + jnp.log(z))[..., 0]
        o = o.reshape(bsz, dil, lp, heads, dh)[:, :, :sub_len]
        o = o.transpose(0, 2, 1, 3, 4).reshape(bsz, seq, heads, dh)
        lse = lse.transpose(0, 1, 2, 4, 3).reshape(bsz, dil, lp, heads)[:, :, :sub_len]
        lse = lse.transpose(0, 2, 1, 3).reshape(bsz, seq, heads)
        outs.append(o)
        lses.append(lse)
    wts = jax.nn.softmax(jnp.stack(lses, axis=0), axis=0)
    out = jnp.sum(wts[..., None] * jnp.stack(outs, axis=0).astype(jnp.float32), axis=0)
    return out.astype(q.dtype)


def mla_attention(cq, ckv, kr, qa_g, wuq, kva_g, wukv, qn_g, kn_g):
    bsz, seq, _ = cq.shape
    q = (rms_norm(cq, qa_g) @ wuq).reshape(bsz, seq, B_HEADS, B_QK_DIM)
    kv = (rms_norm(ckv, kva_g) @ wukv).reshape(bsz, seq, B_HEADS, B_NOPE_DIM + B_V_DIM)
    k_nope, v = jnp.split(kv, [B_NOPE_DIM], axis=-1)
    k = jnp.concatenate([k_nope, jnp.broadcast_to(kr[:, :, None, :], (bsz, seq, B_HEADS, B_ROPE_DIM))], axis=-1)
    q = rms_norm(q, qn_g)
    k = rms_norm(k, kn_g)
    cos, sin = rope_cos_sin(seq, B_ROPE_DIM)
    q = jnp.concatenate([q[..., :B_NOPE_DIM], apply_rope(q[..., B_NOPE_DIM:], cos, sin)], axis=-1)
    k = jnp.concatenate([k[..., :B_NOPE_DIM], apply_rope(k[..., B_NOPE_DIM:], cos, sin)], axis=-1)
    scale = B_QK_DIM ** -0.5
    qb = q.reshape(bsz, seq // B_QBLOCK, B_QBLOCK, B_HEADS, B_QK_DIM).transpose(1, 0, 2, 3, 4)

    def attend(qi):
        s = jnp.einsum('bqhd,bkhd->bhqk', qi, k, preferred_element_type=jnp.float32) * scale
        p = jax.nn.softmax(s, axis=-1)
        return jnp.einsum('bhqk,bkhd->bqhd', p.astype(v.dtype), v)

    o = lax.map(attend, qb)
    return o.transpose(1, 0, 2, 3, 4).reshape(bsz, seq, B_WIDTH)


def linear_scan(a, b):
    def combine(e1, e2):
        a1, b1 = e1
        a2, b2 = e2
        return a1 * a2, a2 * b1 + b2
    _, h = lax.associative_scan(combine, (a, b), axis=1)
    return h


def rglru_branch(xb, gb, conv_w, conv_b, wr, br, wi, bi, lam):
    bsz, seq, width = xb.shape
    xc = centred_depthwise_conv(xb, conv_w) + conv_b
    xg = xc.reshape(bsz, seq, C_BLOCKS, C_BLOCK_DIM)
    xf = xc.astype(jnp.float32)

    def direction(d):
        r = jax.nn.sigmoid((jnp.einsum('bsgi,gio->bsgo', xg, wr[d]).reshape(bsz, seq, width) + br[d]).astype(jnp.float32))
        i = jax.nn.sigmoid((jnp.einsum('bsgi,gio->bsgo', xg, wi[d]).reshape(bsz, seq, width) + bi[d]).astype(jnp.float32))
        log_a = -C_GATE_C * jax.nn.softplus(-lam[d].astype(jnp.float32)) * r
        a = jnp.exp(log_a)
        u = jnp.sqrt(-jnp.expm1(2.0 * log_a)) * (i * xf)
        return a, u

    a_f, u_f = direction(0)
    h_fwd = linear_scan(a_f, u_f)
    a_b, u_b = direction(1)
    h_bwd = jnp.flip(linear_scan(jnp.flip(a_b, axis=1), jnp.flip(u_b, axis=1)), axis=1)
    return ((h_fwd + h_bwd) * jax.nn.gelu(gb.astype(jnp.float32))).astype(xb.dtype)


def chunk_gated_delta(q, k, v, g, beta):
    bsz, seq, heads, dk = q.shape
    dv = v.shape[-1]
    c = D_CHUNK
    nc = seq // c

    def chunks(t):
        return t.astype(jnp.float32).reshape(bsz, nc, c, heads, -1).transpose(1, 0, 3, 2, 4)

    qc = chunks(q) * (dk ** -0.5)
    kc = chunks(k)
    vc = chunks(v)
    bc = chunks(beta[..., None])
    gc = jnp.cumsum(chunks(g[..., None])[..., 0], axis=-1)
    idx = jnp.arange(c)
    incl = idx[:, None] >= idx[None, :]
    strict = idx[:, None] > idx[None, :]
    decay = jnp.exp(jnp.where(incl, gc[..., :, None] - gc[..., None, :], -jnp.inf))
    kk = jnp.einsum('nbhid,nbhjd->nbhij', kc * bc, kc)
    lmat = jnp.where(strict, kk * decay, 0.0)
    eye = jnp.eye(c, dtype=jnp.float32)
    t_inv = lax.linalg.triangular_solve(eye + lmat, jnp.broadcast_to(eye, lmat.shape),
                                        left_side=True, lower=True, unit_diagonal=True)
    u = t_inv @ (vc * bc)
    w = t_inv @ (kc * bc * jnp.exp(gc)[..., None])
    attn = jnp.where(incl, jnp.einsum('nbhid,nbhjd->nbhij', qc, kc) * decay, 0.0)

    def step(state, xs):
        q_i, k_i, u_i, w_i, g_i, a_i = xs
        v_new = u_i - w_i @ state
        o_i = (q_i * jnp.exp(g_i)[..., None]) @ state + a_i @ v_new
        g_last = g_i[..., -1:]
        state = state * jnp.exp(g_last)[..., None] + jnp.einsum(
            'bhck,bhcv->bhkv', k_i * jnp.exp(g_last - g_i)[..., None], v_new)
        return state, o_i

    state0 = jnp.zeros((bsz, heads, dk, dv), jnp.float32)
    _, o = lax.scan(step, state0, (qc, kc, u, w, gc, attn))
    return o.transpose(1, 0, 3, 2, 4).reshape(bsz, seq, heads, dv)


def gdn_branch(q, k, v, z, a_raw, b_raw, conv_w, a_log, dt_bias, on_g):
    bsz, seq, _ = q.shape
    out_dtype = z.dtype
    qkv = jax.nn.silu(centred_depthwise_conv(jnp.concatenate([q, k, v], axis=-1), conv_w))
    qh, kh, vh = jnp.split(qkv, 3, axis=-1)
    qh = l2_norm(qh.reshape(bsz, seq, D_HEADS, D_HEAD_DIM))
    kh = l2_norm(kh.reshape(bsz, seq, D_HEADS, D_HEAD_DIM))
    vh = vh.reshape(bsz, seq, D_HEADS, D_HEAD_DIM)
    a_raw = a_raw.reshape(bsz, seq, 2, D_HEADS).astype(jnp.float32)
    b_raw = b_raw.reshape(bsz, seq, 2, D_HEADS).astype(jnp.float32)
    g_f = -jnp.exp(a_log[0].astype(jnp.float32)) * jax.nn.softplus(a_raw[:, :, 0] + dt_bias[0].astype(jnp.float32))
    beta_f = jax.nn.sigmoid(b_raw[:, :, 0])
    g_b = -jnp.exp(a_log[1].astype(jnp.float32)) * jax.nn.softplus(a_raw[:, :, 1] + dt_bias[1].astype(jnp.float32))
    beta_b = jax.nn.sigmoid(b_raw[:, :, 1])
    o_fwd = chunk_gated_delta(qh, kh, vh, g_f, beta_f)
    o_bwd = jnp.flip(chunk_gated_delta(jnp.flip(qh, axis=1), jnp.flip(kh, axis=1), jnp.flip(vh, axis=1),
                                       jnp.flip(g_b, axis=1), jnp.flip(beta_b, axis=1)), axis=1)
    zh = z.reshape(bsz, seq, D_HEADS, D_HEAD_DIM).astype(jnp.float32)
    y = rms_norm(o_fwd + o_bwd, on_g) * jax.nn.silu(zh)
    return y.reshape(bsz, seq, D_WIDTH).astype(out_dtype)


def encoder_layer(x, ln1_g, w_in, a_qn_g, a_kn_g, b_qa_g, b_wuq, b_kva_g, b_wukv, b_qn_g, b_kn_g,
                  c_conv_w, c_conv_b, c_wr, c_br, c_wi, c_bi, c_lam, d_conv_w, d_a_log, d_dt_bias, d_on_g,
                  w_branch, w_out, ln2_g, w_up, w_down):
    bsz, seq, _ = x.shape
    xn = rms_norm(x, ln1_g)
    h = xn @ w_in
    split_at = [int(i) for i in np.cumsum(IN_SPLITS)[:-1]]
    (a_q, a_k, a_v, b_cq, b_ckv, b_kr, c_x, c_g,
     d_q, d_k, d_v, d_z, d_a, d_b, gates) = jnp.split(h, split_at, axis=-1)

    cos, sin = rope_cos_sin(seq, A_ROPE_DIM)
    aq = partial_rope(rms_norm(a_q.reshape(bsz, seq, A_HEADS, A_HEAD_DIM), a_qn_g), cos, sin, A_ROPE_DIM)
    ak = partial_rope(rms_norm(a_k.reshape(bsz, seq, A_HEADS, A_HEAD_DIM), a_kn_g), cos, sin, A_ROPE_DIM)
    o_a = dilated_attention(aq, ak, a_v.reshape(bsz, seq, A_HEADS, A_HEAD_DIM)).reshape(bsz, seq, A_WIDTH)
    o_b = mla_attention(b_cq, b_ckv, b_kr, b_qa_g, b_wuq, b_kva_g, b_wukv, b_qn_g, b_kn_g)
    o_c = rglru_branch(c_x, c_g, c_conv_w, c_conv_b, c_wr, c_br, c_wi, c_bi, c_lam)
    o_d = gdn_branch(d_q, d_k, d_v, d_z, d_a, d_b, d_conv_w, d_a_log, d_dt_bias, d_on_g)

    gate = jax.nn.sigmoid(gates.astype(jnp.float32)).reshape(bsz, seq, N_BRANCH, D_MODEL)
    merged = None
    start = 0
    for i, o_i in enumerate((o_a, o_b, o_c, o_d)):
        width = o_i.shape[-1]
        term = gate[:, :, i] * (o_i @ w_branch[start:start + width]).astype(jnp.float32)
        merged = term if merged is None else merged + term
        start += width
    x = x + merged.astype(x.dtype) @ w_out

    xn2 = rms_norm(x, ln2_g)
    g_ff, u_ff = jnp.split(xn2 @ w_up, 2, axis=-1)
    x = x + (jax.nn.silu(g_ff) * u_ff) @ w_down
    return x


def setup_inputs(seed: int = 0) -> dict:
    key = jax.random.key(seed)
    ks = jax.random.split(key, 32)
    f32 = jnp.float32
    nl = DEPTH

    def dense(k, shape, fan_in):
        return jax.random.normal(k, shape, f32) * (fan_in ** -0.5)

    def gain(k, shape):
        return 1.0 + 0.02 * jax.random.normal(k, shape, f32)

    def small(k, shape):
        return 0.02 * jax.random.normal(k, shape, f32)

    lam_u = jax.random.uniform(ks[20], (nl, 2, C_WIDTH), dtype=f32, minval=0.9, maxval=0.999)
    lam_p = lam_u ** (1.0 / C_GATE_C)
    c_lam = jnp.log(lam_p) - jnp.log1p(-lam_p)
    d_a_log = jnp.log(jax.random.uniform(ks[21], (nl, 2, D_HEADS), dtype=f32, minval=1.0, maxval=16.0))
    dt = jnp.exp(jax.random.uniform(ks[22], (nl, 2, D_HEADS), dtype=f32,
                                    minval=math.log(1e-3), maxval=math.log(1e-1)))
    d_dt_bias = dt + jnp.log(-jnp.expm1(-dt))

    return {
        'x_prompt': jax.random.normal(ks[0], (BATCH, SEQ, D_MODEL), f32),
        'x_sample': jax.random.normal(ks[1], (DEC_BATCH, DEC_SEQ, D_MODEL), f32),
        'ln1_g': gain(ks[2], (nl, D_MODEL)),
        'w_in': dense(ks[3], (nl, D_MODEL, IN_DIM), D_MODEL),
        'a_qn_g': gain(ks[4], (nl, A_HEAD_DIM)),
        'a_kn_g': gain(ks[5], (nl, A_HEAD_DIM)),
        'b_qa_g': gain(ks[6], (nl, B_Q_RANK)),
        'b_wuq': dense(ks[7], (nl, B_Q_RANK, B_HEADS * B_QK_DIM), B_Q_RANK),
        'b_kva_g': gain(ks[8], (nl, B_KV_RANK)),
        'b_wukv': dense(ks[9], (nl, B_KV_RANK, B_HEADS * (B_NOPE_DIM + B_V_DIM)), B_KV_RANK),
        'b_qn_g': gain(ks[10], (nl, B_QK_DIM)),
        'b_kn_g': gain(ks[11], (nl, B_QK_DIM)),
        'c_conv_w': dense(ks[12], (nl, C_CONV, C_WIDTH), C_CONV),
        'c_conv_b': small(ks[13], (nl, C_WIDTH)),
        'c_wr': dense(ks[14], (nl, 2, C_BLOCKS, C_BLOCK_DIM, C_BLOCK_DIM), C_BLOCK_DIM),
        'c_br': small(ks[15], (nl, 2, C_WIDTH)),
        'c_wi': dense(ks[16], (nl, 2, C_BLOCKS, C_BLOCK_DIM, C_BLOCK_DIM), C_BLOCK_DIM),
        'c_bi': small(ks[17], (nl, 2, C_WIDTH)),
        'c_lam': c_lam,
        'd_conv_w': dense(ks[18], (nl, D_CONV, 3 * D_WIDTH), D_CONV),
        'd_a_log': d_a_log,
        'd_dt_bias': d_dt_bias,
        'd_on_g': gain(ks[19], (nl, D_HEAD_DIM)),
        'w_branch': dense(ks[23], (nl, MIX_WIDTH, D_MODEL), MIX_WIDTH),
        'w_out': dense(ks[24], (nl, D_MODEL, D_MODEL), D_MODEL),
        'ln2_g': gain(ks[25], (nl, D_MODEL)),
        'w_up': dense(ks[26], (nl, D_MODEL, 2 * FF_DIM), D_MODEL),
        'w_down': dense(ks[27], (nl, FF_DIM, D_MODEL), FF_DIM),
    }


def reference(x_prompt, x_sample, ln1_g, w_in, a_qn_g, a_kn_g, b_qa_g, b_wuq, b_kva_g, b_wukv, b_qn_g, b_kn_g,
              c_conv_w, c_conv_b, c_wr, c_br, c_wi, c_bi, c_lam, d_conv_w, d_a_log, d_dt_bias, d_on_g,
              w_branch, w_out, ln2_g, w_up, w_down):
    def trunk(x):
        for l in range(DEPTH):
            x = encoder_layer(x, ln1_g[l], w_in[l], a_qn_g[l], a_kn_g[l], b_qa_g[l], b_wuq[l], b_kva_g[l],
                              b_wukv[l], b_qn_g[l], b_kn_g[l], c_conv_w[l], c_conv_b[l], c_wr[l], c_br[l],
                              c_wi[l], c_bi[l], c_lam[l], d_conv_w[l], d_a_log[l], d_dt_bias[l], d_on_g[l],
                              w_branch[l], w_out[l], ln2_g[l], w_up[l], w_down[l])
        return x

    y_prompt = trunk(x_prompt)
    y_sample = trunk(x_sample)
    return (y_prompt, y_sample)
```

```python
import functools

import jax
import jax.numpy as jnp
import numpy as np
from jax import lax
from jax.experimental import pallas as pl
from jax.experimental.pallas import tpu as pltpu

F32 = jnp.float32
BF16 = jnp.bfloat16

D_MODEL = 1024
ROPE_THETA = 500000.0
NORM_EPS = 1e-6
NEG_INF = -1e30

A_HEADS = 8
A_HEAD_DIM = 64
A_ROPE_DIM = 16
A_DILATIONS = (1, 4, 16)
A_HALF = 64
A_WIDTH = A_HEADS * A_HEAD_DIM

B_HEADS = 8
B_NOPE_DIM = 64
B_ROPE_DIM = 32
B_QK_DIM = B_NOPE_DIM + B_ROPE_DIM
B_V_DIM = 64
B_Q_RANK = 256
B_KV_RANK = 128

C_WIDTH = 512
C_BLOCKS = 8
C_BLOCK_DIM = 64
C_GATE_C = 8.0

D_HEADS = 4
D_HEAD_DIM = 128
D_WIDTH = 512
D_CHUNK = 64

FF_DIM = 2816
LANES = 128
SUBLANES = 8
VMEM_LIMIT = 56 * 1024 * 1024

_OFF = np.cumsum([0, 512, 512, 512, 256, 128, 32, 512, 512, 512, 512, 512, 512, 8, 8, 4096])
(O_AQ, O_AK, O_AV, O_BCQ, O_BCKV, O_BKR, O_CX, O_CG, O_DQ, O_DK, O_DV, O_DZ, O_DA, O_DB, O_GATE, O_END) = [int(v) for v in _OFF]


def _const_spec(shape):
    nd = len(shape)
    return pl.BlockSpec(shape, lambda *_: (0,) * nd, pipeline_mode=pl.Buffered(1))


def _params(sem):
    return pltpu.CompilerParams(dimension_semantics=sem, vmem_limit_bytes=VMEM_LIMIT)


def _dot(a, b):
    return jnp.dot(a, b, preferred_element_type=F32)


def _dot_nt(a, b):
    return lax.dot_general(a, b, (((1,), (1,)), ((), ())), preferred_element_type=F32)


def _dot_tn(a, b):
    return lax.dot_general(a, b, (((0,), (0,)), ((), ())), preferred_element_type=F32)


def _rms_rows(x, g):
    return x * lax.rsqrt(jnp.mean(x * x, axis=-1, keepdims=True) + NORM_EPS) * g


def _split2(x):
    hi = x.astype(BF16)
    lo = (x - hi.astype(F32)).astype(BF16)
    return hi, lo


def _split3(x):
    p1 = x.astype(BF16)
    r1 = x - p1.astype(F32)
    p2 = r1.astype(BF16)
    p3 = (r1 - p2.astype(F32)).astype(BF16)
    return p1, p2, p3


def _sigmoid(x):
    return 1.0 / (1.0 + jnp.exp(-x))


def _silu(x):
    return x * _sigmoid(x)


def _softplus(x):
    return jnp.maximum(x, 0.0) + jnp.log(1.0 + jnp.exp(-jnp.abs(x)))


def _rope_tables(pos0, rows, consts):
    pos = (pos0 + lax.broadcasted_iota(jnp.int32, (rows, LANES), 0)).astype(F32)
    ang = pos * consts[0:1, :]
    c = jnp.cos(ang)
    s = jnp.sin(ang)
    return c, s * consts[1:2, :], s * consts[2:3, :]


def _in_proj_body(x_ref, ln_ref, wa_ref, wb_ref, wc_ref, wd_ref, wab_ref, wg_ref,
                  gmat_ref, aqg_ref, akg_ref, arope_ref,
                  qag_ref, wuq_ref, kvag_ref, wukv_ref, bqg_ref, bkg_ref, brope_ref,
                  aq_ref, ak_ref, av_ref, bq_ref, bk_ref, bv_ref,
                  cx_ref, cg_ref, dqkv_ref, dz_ref, dab_ref, gate_ref, *, tm):
    pos0 = pl.program_id(1) * tm
    xn = _rms_rows(x_ref[0], ln_ref[...]).astype(BF16)

    ca, sa1, sa2 = _rope_tables(pos0, tm, arope_ref[...])
    ca = jnp.concatenate([ca] * 4, axis=1)
    sa1 = jnp.concatenate([sa1] * 4, axis=1)
    sa2 = jnp.concatenate([sa2] * 4, axis=1)
    gmat = gmat_ref[...]

    def a_norm_rope(t, g):
        hi, lo = _split2(t * t)
        ms = (_dot(hi, gmat) + _dot(lo, gmat)) * (1.0 / A_HEAD_DIM)
        y = t * lax.rsqrt(ms + NORM_EPS) * g
        return y * ca + pltpu.roll(y, A_WIDTH - A_ROPE_DIM // 2, 1) * sa1 + pltpu.roll(y, A_ROPE_DIM // 2, 1) * sa2

    q = _dot(xn, wa_ref[:, 0:A_WIDTH])
    aq_ref[0] = (a_norm_rope(q, aqg_ref[...]) * (A_HEAD_DIM ** -0.5)).astype(BF16)
    k = _dot(xn, wa_ref[:, A_WIDTH:2 * A_WIDTH])
    ak_ref[0] = a_norm_rope(k, akg_ref[...]).astype(BF16)
    av_ref[0] = _dot(xn, wa_ref[:, 2 * A_WIDTH:3 * A_WIDTH]).astype(BF16)

    hb = _dot(xn, wb_ref[...])
    cq = _rms_rows(hb[:, 0:B_Q_RANK], qag_ref[...]).astype(BF16)
    ckv = _rms_rows(hb[:, B_Q_RANK:B_Q_RANK + B_KV_RANK], kvag_ref[...]).astype(BF16)
    kr = hb[:, B_Q_RANK + B_KV_RANK:]
    cb, sb1, sb2 = _rope_tables(pos0, tm, brope_ref[...])
    lane = lax.broadcasted_iota(jnp.int32, (tm, LANES), 1)
    ones_col = jnp.where(lane == B_V_DIM, 1.0, 0.0)

    def b_norm_rope(t, g):
        ms = jnp.sum(t * t, axis=-1, keepdims=True) * (1.0 / B_QK_DIM)
        y = t * lax.rsqrt(ms + NORM_EPS) * g
        return y * cb + pltpu.roll(y, LANES - B_ROPE_DIM // 2, 1) * sb1 + pltpu.roll(y, B_ROPE_DIM // 2, 1) * sb2

    for h in range(B_HEADS):
        sl = slice(h * LANES, (h + 1) * LANES)
        qh = _dot(cq, wuq_ref[:, sl])
        bq_ref[0, h] = (b_norm_rope(qh, bqg_ref[...]) * (B_QK_DIM ** -0.5)).astype(BF16)
        kh = _dot(ckv, wukv_ref[:, sl]) + kr
        bk_ref[0, h] = b_norm_rope(kh, bkg_ref[...]).astype(BF16)
        vh = _dot(ckv, wukv_ref[:, B_HEADS * LANES + h * LANES:B_HEADS * LANES + (h + 1) * LANES])
        bv_ref[0, h] = (vh + ones_col).astype(BF16)

    hc = _dot(xn, wc_ref[...])
    cx_ref[0] = hc[:, 0:C_WIDTH]
    cg_ref[0] = hc[:, C_WIDTH:].astype(BF16)
    for j in range(3):
        dqkv_ref[0, :, j * D_WIDTH:(j + 1) * D_WIDTH] = _dot(xn, wd_ref[:, j * D_WIDTH:(j + 1) * D_WIDTH])
    dz_ref[0] = _dot(xn, wd_ref[:, 3 * D_WIDTH:]).astype(BF16)
    dab_ref[0] = _dot(xn, wab_ref[...])
    for j in range(4 * D_MODEL // 512):
        gate_ref[0, :, j * 512:(j + 1) * 512] = _dot(xn, wg_ref[:, j * 512:(j + 1) * 512]).astype(BF16)


def _in_proj(x, w, *, tm=256):
    bsz, seq, _ = x.shape
    grid = (bsz, seq // tm)
    tok = lambda width: pl.BlockSpec((1, tm, width), lambda b, i: (b, i, 0))
    head = pl.BlockSpec((1, B_HEADS, tm, LANES), lambda b, i: (b, 0, i, 0))
    consts = [w["ln1"], w["wa"], w["wb"], w["wc"], w["wd"], w["wab"], w["wg"],
              w["gmat"], w["aqg"], w["akg"], w["arope"],
              w["qag"], w["wuq"], w["kvag"], w["wukv"], w["bqg"], w["bkg"], w["brope"]]
    tokf = lambda width, dt: jax.ShapeDtypeStruct((bsz, seq, width), dt)
    headf = jax.ShapeDtypeStruct((bsz, B_HEADS, seq, LANES), BF16)
    return pl.pallas_call(
        functools.partial(_in_proj_body, tm=tm),
        grid=grid,
        in_specs=[tok(D_MODEL)] + [_const_spec(c.shape) for c in consts],
        out_specs=[tok(A_WIDTH), tok(A_WIDTH), tok(A_WIDTH), head, head, head,
                   tok(C_WIDTH), tok(C_WIDTH), tok(3 * D_WIDTH), tok(D_WIDTH), tok(LANES), tok(4 * D_MODEL)],
        out_shape=[tokf(A_WIDTH, BF16), tokf(A_WIDTH, BF16), tokf(A_WIDTH, BF16), headf, headf, headf,
                   tokf(C_WIDTH, F32), tokf(C_WIDTH, BF16), tokf(3 * D_WIDTH, F32), tokf(D_WIDTH, BF16),
                   tokf(LANES, F32), tokf(4 * D_MODEL, BF16)],
        compiler_params=_params(("parallel", "parallel")),
        name="in_proj",
    )(x, *consts)


def _a_attn_body(q_ref, k_ref, kp_ref, kn_ref, v_ref, vp_ref, vn_ref, o_ref, lse_ref, kbuf, vbuf, *, lq, sub_len):
    blk = A_HALF
    i = pl.program_id(2)
    kbuf[0:blk] = kp_ref[0]
    kbuf[blk:blk + lq] = k_ref[0]
    kbuf[blk + lq:] = kn_ref[0]
    vbuf[0:blk] = vp_ref[0]
    vbuf[blk:blk + lq] = v_ref[0]
    vbuf[blk + lq:] = vn_ref[0]
    kw = 3 * blk
    rel = lax.broadcasted_iota(jnp.int32, (blk, kw), 1) - blk - lax.broadcasted_iota(jnp.int32, (blk, kw), 0)
    band = jnp.abs(rel) <= A_HALF
    kcol = lax.broadcasted_iota(jnp.int32, (blk, kw), 1)
    lane = lax.broadcasted_iota(jnp.int32, (blk, LANES), 1)
    first = lane < A_HEAD_DIM

    def sub_block(j, carry):
        r0 = pl.multiple_of(j * blk, blk)
        kpos = i * lq + r0 - blk + kcol
        valid = band & (kpos >= 0) & (kpos < sub_len)
        for hp in range(A_HEADS // 2):
            sl = slice(hp * LANES, (hp + 1) * LANES)
            q2 = q_ref[0, pl.ds(r0, blk), sl]
            k2 = kbuf[pl.ds(r0, kw), sl]
            v2 = vbuf[pl.ds(r0, kw), sl]
            zero = jnp.zeros_like(q2)
            outs, lses = [], []
            for qh in (jnp.where(first, q2, zero), jnp.where(first, zero, q2)):
                s = jnp.where(valid, _dot_nt(qh, k2), NEG_INF)
                m = jnp.max(s, axis=-1, keepdims=True)
                p = jnp.exp(s - m)
                z = jnp.sum(p, axis=-1, keepdims=True)
                outs.append(_dot(p.astype(BF16), v2) / z)
                lses.append(m + jnp.log(z))
            o_ref[0, pl.ds(r0, blk), sl] = jnp.where(first, outs[0], outs[1])
            lse_ref[0, pl.ds(r0, blk), sl] = jnp.where(first, lses[0], lses[1])
        return carry

    lax.fori_loop(0, lq // blk, sub_block, 0)


def _a_attn(q, k, v, dil):
    bsz, seq, width = q.shape
    sub_len = seq // dil
    lq = min(512, sub_len)
    nblk = sub_len // A_HALF
    per = lq // A_HALF
    view = lambda t: t.reshape(bsz, sub_len, dil * width)
    main = pl.BlockSpec((1, lq, width), lambda b, r, i: (b, i, r))
    prev = pl.BlockSpec((1, A_HALF, width), lambda b, r, i: (b, jnp.maximum(i * per - 1, 0), r))
    nxt = pl.BlockSpec((1, A_HALF, width), lambda b, r, i: (b, jnp.minimum((i + 1) * per, nblk - 1), r))
    qv, kv, vv = view(q), view(k), view(v)
    o, lse = pl.pallas_call(
        functools.partial(_a_attn_body, lq=lq, sub_len=sub_len),
        grid=(bsz, dil, sub_len // lq),
        in_specs=[main, main, prev, nxt, main, prev, nxt],
        out_specs=[main, main],
        out_shape=[jax.ShapeDtypeStruct(qv.shape, F32), jax.ShapeDtypeStruct(qv.shape, F32)],
        scratch_shapes=[pltpu.VMEM((lq + 2 * A_HALF, width), BF16), pltpu.VMEM((lq + 2 * A_HALF, width), BF16)],
        compiler_params=_params(("parallel", "parallel", "parallel")),
        name=f"a_attn_d{dil}",
    )(qv, kv, kv, kv, vv, vv, vv)
    return o.reshape(bsz, seq, width), lse.reshape(bsz, seq, width)


def _b_flash_body(q_ref, k_ref, v_ref, o_ref, *, tq, tk, seq):
    q = q_ref[0, 0]

    def chunk(c, carry):
        m, acc = carry
        r0 = pl.multiple_of(c * tk, tk)
        s = _dot_nt(q, k_ref[0, 0, pl.ds(r0, tk), :])
        m_new = jnp.maximum(m, jnp.max(s, axis=-1, keepdims=True))
        alpha = jnp.exp(m - m_new)
        p = jnp.exp(s - m_new).astype(BF16)
        acc = alpha * acc + _dot(p, v_ref[0, 0, pl.ds(r0, tk), :])
        return m_new, acc

    m0 = jnp.full((tq, 1), NEG_INF, F32)
    acc0 = jnp.zeros((tq, LANES), F32)
    _, acc = lax.fori_loop(0, seq // tk, chunk, (m0, acc0))
    lane = lax.broadcasted_iota(jnp.int32, (tq, LANES), 1)
    denom = jnp.sum(jnp.where(lane == B_V_DIM, acc, 0.0), axis=-1, keepdims=True)
    o_ref[0] = jnp.where(lane < B_V_DIM, acc / denom, 0.0).astype(BF16)


def _b_flash(q, k, v, *, tq=256, tk=512):
    bsz, heads, seq, _ = q.shape
    return pl.pallas_call(
        functools.partial(_b_flash_body, tq=tq, tk=tk, seq=seq),
        grid=(bsz, heads, seq // tq),
        in_specs=[pl.BlockSpec((1, 1, tq, LANES), lambda b, h, i: (b, h, i, 0)),
                  pl.BlockSpec((1, 1, seq, LANES), lambda b, h, i: (b, h, 0, 0)),
                  pl.BlockSpec((1, 1, seq, LANES), lambda b, h, i: (b, h, 0, 0))],
        out_specs=pl.BlockSpec((1, tq, LANES), lambda b, h, i: (b, i, h)),
        out_shape=jax.ShapeDtypeStruct((bsz, seq, heads * LANES), BF16),
        compiler_params=_params(("parallel", "parallel", "parallel")),
        name="b_flash",
    )(q, k, v)


def _fill_ext(ext, prev_ref, main_ref, next_ref, is_first, is_last, ts):
    h = SUBLANES
    ext[0:h] = jnp.where(is_first, 0.0, prev_ref[0])
    ext[h:h + ts] = main_ref[0]
    ext[h + ts:] = jnp.where(is_last, 0.0, next_ref[0])


def _conv4(ext, w_ref, ts):
    h = SUBLANES
    out = ext[h - 2:h - 2 + ts] * w_ref[0:1, :]
    for j in range(1, 4):
        out = out + ext[h - 2 + j:h - 2 + j + ts] * w_ref[j:j + 1, :]
    return out


def _halo_specs(ts, width, nt, rev):
    per = ts // SUBLANES
    nblk = nt * per
    t = (lambda i: nt - 1 - i) if rev else (lambda i: i)
    main = pl.BlockSpec((1, ts, width), lambda b, i: (b, t(i), 0))
    prev = pl.BlockSpec((1, SUBLANES, width), lambda b, i: (b, jnp.maximum(t(i) * per - 1, 0), 0))
    nxt = pl.BlockSpec((1, SUBLANES, width), lambda b, i: (b, jnp.minimum((t(i) + 1) * per, nblk - 1), 0))
    return main, prev, nxt


def _scan_tile(a_s, u_s, h_ref, carry_ref, ts, reverse):
    n = ts // SUBLANES
    row = lax.broadcasted_iota(jnp.int32, (SUBLANES, C_WIDTH), 0)

    def group(g, carry):
        gi = (n - 1 - g) if reverse else g
        r0 = pl.multiple_of(gi * SUBLANES, SUBLANES)
        a = a_s[pl.ds(r0, SUBLANES), :]
        b = u_s[pl.ds(r0, SUBLANES), :]
        for sh in (1, 2, 4):
            if reverse:
                keep = row < SUBLANES - sh
                shift = SUBLANES - sh
            else:
                keep = row >= sh
                shift = sh
            a_sh = pltpu.roll(a, shift, 0)
            b_sh = pltpu.roll(b, shift, 0)
            b = jnp.where(keep, a * b_sh + b, b)
            a = jnp.where(keep, a * a_sh, a)
        h = b + a * carry
        h_ref[0, pl.ds(r0, SUBLANES), :] = h
        last = h[0:1, :] if reverse else h[SUBLANES - 1:SUBLANES, :]
        return jnp.broadcast_to(last, (SUBLANES, C_WIDTH))

    carry_ref[...] = lax.fori_loop(0, n, group, carry_ref[...])


def _rglru_body(xf_ref, xfp_ref, xfn_ref, xb_ref, xbp_ref, xbn_ref,
                cw_ref, cb_ref, wf_ref, wb_ref, bf_ref, bb_ref, lam_ref,
                hf_ref, hb_ref, ext, a_s, u_s, carry_f, carry_b, *, ts, nt):
    i = pl.program_id(1)

    @pl.when(i == 0)
    def _():
        carry_f[...] = jnp.zeros_like(carry_f)
        carry_b[...] = jnp.zeros_like(carry_b)

    def direction(d, main, prev, nxt, tile, w_ref, b_ref, h_ref, carry_ref):
        _fill_ext(ext, prev, main, nxt, tile == 0, tile == nt - 1, ts)
        xc = _conv4(ext, cw_ref, ts) + cb_ref[...]
        g = _dot(xc.astype(BF16), w_ref[...]) + b_ref[...]
        r = _sigmoid(g[:, 0:C_WIDTH])
        gi = _sigmoid(g[:, C_WIDTH:])
        log_a = (-C_GATE_C * _softplus(-lam_ref[d:d + 1, :])) * r
        a_s[...] = jnp.exp(log_a)
        u_s[...] = jnp.sqrt(1.0 - jnp.exp(2.0 * log_a)) * (gi * xc)
        _scan_tile(a_s, u_s, h_ref, carry_ref, ts, reverse=(d == 1))

    direction(0, xf_ref, xfp_ref, xfn_ref, i, wf_ref, bf_ref, hf_ref, carry_f)
    direction(1, xb_ref, xbp_ref, xbn_ref, nt - 1 - i, wb_ref, bb_ref, hb_ref, carry_b)


def _rglru(cx, w, *, ts=512):
    bsz, seq, _ = cx.shape
    ts = min(ts, seq)
    nt = seq // ts
    fm, fp, fn = _halo_specs(ts, C_WIDTH, nt, False)
    bm, bp, bn = _halo_specs(ts, C_WIDTH, nt, True)
    consts = [w["c_conv_w"], w["c_conv_b"], w["c_wf"], w["c_wb"], w["c_bf"], w["c_bb"], w["c_lam"]]
    out = jax.ShapeDtypeStruct((bsz, seq, C_WIDTH), F32)
    return pl.pallas_call(
        functools.partial(_rglru_body, ts=ts, nt=nt),
        grid=(bsz, nt),
        in_specs=[fm, fp, fn, bm, bp, bn] + [_const_spec(c.shape) for c in consts],
        out_specs=[fm, bm],
        out_shape=[out, out],
        scratch_shapes=[pltpu.VMEM((ts + 2 * SUBLANES, C_WIDTH), F32), pltpu.VMEM((ts, C_WIDTH), F32),
                        pltpu.VMEM((ts, C_WIDTH), F32), pltpu.VMEM((SUBLANES, C_WIDTH), F32),
                        pltpu.VMEM((SUBLANES, C_WIDTH), F32)],
        compiler_params=_params(("parallel", "arbitrary")),
        name="rglru",
    )(cx, cx, cx, cx, cx, cx, *consts)


def _d_prep_body(x_ref, xp_ref, xn_ref, ab_ref, cw_ref, dconst_ref, q_ref, k_ref, v_ref, gb_ref, ext, *, ts, nt):
    i = pl.program_id(1)
    _fill_ext(ext, xp_ref, x_ref, xn_ref, i == 0, i == nt - 1, ts)
    y = _silu(_conv4(ext, cw_ref, ts))
    for h in range(D_HEADS):
        for j, ref in ((0, q_ref), (1, k_ref)):
            t = y[:, j * D_WIDTH + h * LANES:j * D_WIDTH + (h + 1) * LANES]
            ref[0, :, h * LANES:(h + 1) * LANES] = t * lax.rsqrt(jnp.sum(t * t, axis=-1, keepdims=True) + NORM_EPS)
    v_ref[0] = y[:, 2 * D_WIDTH:]
    ab = ab_ref[0]
    lane = lax.broadcasted_iota(jnp.int32, ab.shape, 1)
    g = -jnp.exp(dconst_ref[0:1, :]) * _softplus(ab + dconst_ref[1:2, :])
    gb_ref[0] = jnp.where(lane < 2 * D_HEADS, g, _sigmoid(ab))


def _d_prep(dqkv, dab, w, *, ts=256):
    bsz, seq, width = dqkv.shape
    ts = min(ts, seq)
    nt = seq // ts
    main, prev, nxt = _halo_specs(ts, width, nt, False)
    tok = lambda wd: pl.BlockSpec((1, ts, wd), lambda b, i: (b, i, 0))
    o512 = jax.ShapeDtypeStruct((bsz, seq, D_WIDTH), F32)
    return pl.pallas_call(
        functools.partial(_d_prep_body, ts=ts, nt=nt),
        grid=(bsz, nt),
        in_specs=[main, prev, nxt, tok(LANES), _const_spec(w["d_conv_w"].shape), _const_spec(w["d_const"].shape)],
        out_specs=[tok(D_WIDTH), tok(D_WIDTH), tok(D_WIDTH), tok(LANES)],
        out_shape=[o512, o512, o512, jax.ShapeDtypeStruct((bsz, seq, LANES), F32)],
        scratch_shapes=[pltpu.VMEM((ts + 2 * SUBLANES, width), F32)],
        compiler_params=_params(("parallel", "parallel")),
        name="d_prep",
    )(dqkv, dqkv, dqkv, dab, w["d_conv_w"], w["d_const"])


def _d_chunk_body(qf_ref, kf_ref, vf_ref, gf_ref, qb_ref, kb_ref, vb_ref, gbk_ref,
                  of_ref, ob_ref, s_ref, gc_s, *, ts):
    i = pl.program_id(1)

    @pl.when(i == 0)
    def _():
        s_ref[...] = jnp.zeros_like(s_ref)

    c = D_CHUNK
    nc = ts // c
    scale = D_HEAD_DIM ** -0.5
    row_t = lax.broadcasted_iota(jnp.int32, (ts, ts), 0)
    col_t = lax.broadcasted_iota(jnp.int32, (ts, ts), 1)
    same = (row_t // c) == (col_t // c)
    ri = lax.broadcasted_iota(jnp.int32, (c, c), 0)
    ci = lax.broadcasted_iota(jnp.int32, (c, c), 1)
    lane = lax.broadcasted_iota(jnp.int32, (c, LANES), 1)
    ones3 = jnp.where(lane < 3, 1.0, 0.0).astype(BF16)

    dirs = ((0, qf_ref, kf_ref, vf_ref, gf_ref, of_ref), (1, qb_ref, kb_ref, vb_ref, gbk_ref, ob_ref))
    for d, q_ref, k_ref, v_ref, g_ref, o_ref in dirs:
        rev = d == 1
        before = (col_t >= row_t) if rev else (col_t <= row_t)
        tri = jnp.where(same & before, 1.0, 0.0).astype(BF16)
        p1, p2, p3 = _split3(g_ref[0])
        gc_s[d] = _dot(tri, p1) + _dot(tri, p2) + _dot(tri, p3)

    def chunk(step, carry):
        for d, q_ref, k_ref, v_ref, g_ref, o_ref in dirs:
            rev = d == 1
            incl = (ci >= ri) if rev else (ci <= ri)
            strict = (ci > ri) if rev else (ci < ri)
            cc = (nc - 1 - step) if rev else step
            r0 = pl.multiple_of(cc * c, c)
            gcb = gc_s[d, pl.ds(r0, c), :]
            gbb = g_ref[0, pl.ds(r0, c), :]
            for h in range(D_HEADS):
                sl = slice(h * LANES, (h + 1) * LANES)
                col = d * D_HEADS + h
                gcol = jnp.sum(jnp.where(lane == col, gcb, 0.0), axis=-1, keepdims=True)
                beta = jnp.sum(jnp.where(lane == 2 * D_HEADS + col, gbb, 0.0), axis=-1, keepdims=True)
                gtot = gcol[0:1, :] if rev else gcol[c - 1:c, :]
                g1, g2, g3 = (p.astype(F32) for p in _split3(gcol))
                pieces = jnp.where(lane == 0, g1, jnp.where(lane == 1, g2, jnp.where(lane == 2, g3, 0.0)))
                grow = _dot_nt(ones3, pieces.astype(BF16))
                decay = jnp.exp(jnp.where(incl, gcol - grow, NEG_INF))
                q = q_ref[0, pl.ds(r0, c), sl]
                k = k_ref[0, pl.ds(r0, c), sl]
                v = v_ref[0, pl.ds(r0, c), sl]
                kbeta = k * beta
                kb16 = k.astype(BF16)
                lmat = jnp.where(strict, _dot_nt(kbeta.astype(BF16), kb16) * decay, 0.0)
                attn = jnp.where(incl, _dot_nt((q * scale).astype(BF16), kb16) * decay, 0.0)
                nmat = -lmat
                x16 = lmat.astype(BF16)
                xpow = _dot(x16, x16)
                for it in range(5):
                    x16 = xpow.astype(BF16)
                    nmat = nmat + xpow + _dot(nmat.astype(BF16), x16)
                    if it < 4:
                        xpow = _dot(x16, x16)
                n16 = nmat.astype(BF16)
                vb = v * beta
                wk = kbeta * jnp.exp(gcol)
                u = vb + _dot(n16, vb.astype(BF16))
                w = wk + _dot(n16, wk.astype(BF16))
                state = s_ref[col]
                s16 = state.astype(BF16)
                v_new = u - _dot(w.astype(BF16), s16)
                vn16 = v_new.astype(BF16)
                o = _dot((q * (scale * jnp.exp(gcol))).astype(BF16), s16) + _dot(attn.astype(BF16), vn16)
                o_ref[0, pl.ds(r0, c), sl] = o
                kdec = (k * jnp.exp(gtot - gcol)).astype(BF16)
                s_ref[col] = state * jnp.exp(gtot) + _dot_tn(kdec, vn16)
        return carry

    lax.fori_loop(0, nc, chunk, 0)


def _d_chunk(q, k, v, gb, *, ts=256):
    bsz, seq, _ = q.shape
    ts = min(ts, seq)
    nt = seq // ts
    fwd = lambda wd: pl.BlockSpec((1, ts, wd), lambda b, i: (b, i, 0))
    bwd = lambda wd: pl.BlockSpec((1, ts, wd), lambda b, i: (b, nt - 1 - i, 0))
    out = jax.ShapeDtypeStruct((bsz, seq, D_WIDTH), F32)
    return pl.pallas_call(
        functools.partial(_d_chunk_body, ts=ts),
        grid=(bsz, nt),
        in_specs=[fwd(D_WIDTH), fwd(D_WIDTH), fwd(D_WIDTH), fwd(LANES),
                  bwd(D_WIDTH), bwd(D_WIDTH), bwd(D_WIDTH), bwd(LANES)],
        out_specs=[fwd(D_WIDTH), bwd(D_WIDTH)],
        out_shape=[out, out],
        scratch_shapes=[pltpu.VMEM((2 * D_HEADS, D_HEAD_DIM, D_HEAD_DIM), F32), pltpu.VMEM((2, ts, LANES), F32)],
        compiler_params=_params(("parallel", "arbitrary")),
        name="d_chunk",
    )(q, k, v, gb, q, k, v, gb)


def _gelu_tanh(x):
    return 0.5 * x * (1.0 + jnp.tanh(np.sqrt(2.0 / np.pi).astype(np.float32) * (x + 0.044715 * (x * x * x))))


def _merge_body(x_ref, o1_ref, l1_ref, o4_ref, l4_ref, o16_ref, l16_ref, ob_ref,
                hf_ref, hb_ref, cg_ref, df_ref, db_ref, dz_ref, gate_ref,
                wbr_ref, wout_ref, ong_ref, y_ref):
    l1, l4, l16 = l1_ref[0], l4_ref[0], l16_ref[0]
    m = jnp.maximum(jnp.maximum(l1, l4), l16)
    e1, e4, e16 = jnp.exp(l1 - m), jnp.exp(l4 - m), jnp.exp(l16 - m)
    oa = (e1 * o1_ref[0] + e4 * o4_ref[0] + e16 * o16_ref[0]) / (e1 + e4 + e16)
    oc = (hf_ref[0] + hb_ref[0]) * _gelu_tanh(cg_ref[0].astype(F32))
    od_sum = df_ref[0] + db_ref[0]
    z = dz_ref[0].astype(F32)
    heads = []
    for h in range(D_HEADS):
        sl = slice(h * LANES, (h + 1) * LANES)
        heads.append(_rms_rows(od_sum[:, sl], ong_ref[...]) * _silu(z[:, sl]))
    od = jnp.concatenate(heads, axis=1)

    def gate(j):
        return _sigmoid(gate_ref[0, :, j * D_MODEL:(j + 1) * D_MODEL].astype(F32))

    r_a, r_b, r_c = A_WIDTH, A_WIDTH + B_HEADS * LANES, A_WIDTH + B_HEADS * LANES + C_WIDTH
    merged = gate(0) * _dot(oa.astype(BF16), wbr_ref[0:r_a, :])
    merged = merged + gate(1) * _dot(ob_ref[0], wbr_ref[r_a:r_b, :])
    merged = merged + gate(2) * _dot(oc.astype(BF16), wbr_ref[r_b:r_c, :])
    merged = merged + gate(3) * _dot(od.astype(BF16), wbr_ref[r_c:, :])
    y_ref[0] = x_ref[0] + _dot(merged.astype(BF16), wout_ref[...])


def _merge(x, a_outs, ob, hf, hb, cg, df, db, dz, gates, w, *, tm=256):
    bsz, seq, _ = x.shape
    tok = lambda wd: pl.BlockSpec((1, tm, wd), lambda b, i: (b, i, 0))
    consts = [w["w_branch"], w["w_out"], w["d_on_g"]]
    ins = [x]
    specs = [tok(D_MODEL)]
    for o, lse in a_outs:
        ins += [o, lse]
        specs += [tok(A_WIDTH), tok(A_WIDTH)]
    ins += [ob, hf, hb, cg, df, db, dz, gates]
    specs += [tok(B_HEADS * LANES), tok(C_WIDTH), tok(C_WIDTH), tok(C_WIDTH), tok(D_WIDTH), tok(D_WIDTH),
              tok(D_WIDTH), tok(4 * D_MODEL)]
    return pl.pallas_call(
        _merge_body,
        grid=(bsz, seq // tm),
        in_specs=specs + [_const_spec(c.shape) for c in consts],
        out_specs=tok(D_MODEL),
        out_shape=jax.ShapeDtypeStruct(x.shape, F32),
        compiler_params=_params(("parallel", "parallel")),
        name="merge",
    )(*ins, *consts)


FF_TILE = 256


def _ffn_body(x_ref, ln_ref, wgu_ref, wdn_ref, y_ref):
    x = x_ref[0]
    xn = _rms_rows(x, ln_ref[...]).astype(BF16)
    acc = x
    for j in range(FF_DIM // FF_TILE):
        g = _dot(xn, wgu_ref[:, j * FF_TILE:(j + 1) * FF_TILE])
        u = _dot(xn, wgu_ref[:, FF_DIM + j * FF_TILE:FF_DIM + (j + 1) * FF_TILE])
        acc = acc + _dot((_silu(g) * u).astype(BF16), wdn_ref[j * FF_TILE:(j + 1) * FF_TILE, :])
    y_ref[0] = acc


def _ffn(x, w, *, tm=512):
    bsz, seq, _ = x.shape
    tok = pl.BlockSpec((1, tm, D_MODEL), lambda b, i: (b, i, 0))
    consts = [w["ln2"], w["w_up"], w["w_down"]]
    return pl.pallas_call(
        _ffn_body,
        grid=(bsz, seq // tm),
        in_specs=[tok] + [_const_spec(c.shape) for c in consts],
        out_specs=tok,
        out_shape=jax.ShapeDtypeStruct(x.shape, F32),
        compiler_params=_params(("parallel", "parallel")),
        name="ffn",
    )(x, *consts)


def _block_diag(wblocks):
    g, i, o = wblocks.shape
    eye = jnp.eye(g, dtype=wblocks.dtype)
    return (eye[:, None, :, None] * wblocks[:, :, None, :]).reshape(g * i, g * o)


def _rope_consts(head_period, rope_start, rope_dim):
    lane = np.arange(LANES) % head_period
    j = lane - rope_start
    in_rope = (j >= 0) & (j < rope_dim)
    half = rope_dim // 2
    inv_freq = 1.0 / (ROPE_THETA ** (jnp.arange(0, rope_dim, 2, dtype=F32) / rope_dim))
    inv_lane = jnp.where(jnp.asarray(in_rope), inv_freq[np.where(in_rope, j % half, 0)], 0.0)
    out = jnp.zeros((SUBLANES, LANES), F32)
    out = out.at[0].set(inv_lane)
    out = out.at[1].set(jnp.asarray(np.where(in_rope & (j < half), -1.0, 0.0), F32))
    out = out.at[2].set(jnp.asarray(np.where(in_rope & (j >= half), 1.0, 0.0), F32))
    return out


def _pad_heads(wmat, heads, dim):
    r = wmat.shape[0]
    w3 = wmat.reshape(r, heads, dim)
    return jnp.pad(w3, ((0, 0), (0, 0), (0, LANES - dim))).reshape(r, heads * LANES)


def _pad_lane_vec(vec, dim):
    return jnp.pad(vec, (0, LANES - dim)).reshape(1, LANES)


def _layer_weights(l, ln1_g, w_in, a_qn_g, a_kn_g, b_qa_g, b_wuq, b_kva_g, b_wukv, b_qn_g, b_kn_g,
                   c_conv_w, c_conv_b, c_wr, c_br, c_wi, c_bi, c_lam, d_conv_w, d_a_log, d_dt_bias, d_on_g,
                   w_branch, w_out, ln2_g, w_up, w_down):
    wi = w_in[l]
    row = lambda v: v.reshape(1, -1)
    w = {}
    w["ln1"] = row(ln1_g[l])
    w["wa"] = wi[:, O_AQ:O_BCQ].astype(BF16)
    zeros = lambda n: jnp.zeros((D_MODEL, n), F32)
    w["wb"] = jnp.concatenate([wi[:, O_BCQ:O_BKR], zeros(B_NOPE_DIM), wi[:, O_BKR:O_CX],
                               zeros(LANES - B_QK_DIM)], axis=1).astype(BF16)
    w["wc"] = wi[:, O_CX:O_DQ].astype(BF16)
    w["wd"] = wi[:, O_DQ:O_DA].astype(BF16)
    w["wab"] = jnp.concatenate([wi[:, O_DA:O_GATE], zeros(LANES - 4 * D_HEADS)], axis=1).astype(BF16)
    w["wg"] = wi[:, O_GATE:O_END].astype(BF16)
    w["gmat"] = _block_diag(jnp.ones((A_HEADS, A_HEAD_DIM, A_HEAD_DIM), F32)).astype(BF16)
    w["aqg"] = row(jnp.tile(a_qn_g[l], A_HEADS))
    w["akg"] = row(jnp.tile(a_kn_g[l], A_HEADS))
    w["arope"] = _rope_consts(A_HEAD_DIM, 0, A_ROPE_DIM)
    w["qag"] = row(b_qa_g[l])
    w["wuq"] = _pad_heads(b_wuq[l], B_HEADS, B_QK_DIM).astype(BF16)
    w["kvag"] = row(b_kva_g[l])
    wukv = b_wukv[l].reshape(B_KV_RANK, B_HEADS, B_NOPE_DIM + B_V_DIM)
    w["wukv"] = jnp.concatenate([
        _pad_heads(wukv[:, :, :B_NOPE_DIM].reshape(B_KV_RANK, -1), B_HEADS, B_NOPE_DIM),
        _pad_heads(wukv[:, :, B_NOPE_DIM:].reshape(B_KV_RANK, -1), B_HEADS, B_V_DIM)], axis=1).astype(BF16)
    w["bqg"] = _pad_lane_vec(b_qn_g[l], B_QK_DIM)
    w["bkg"] = _pad_lane_vec(b_kn_g[l], B_QK_DIM)
    w["brope"] = _rope_consts(LANES, B_NOPE_DIM, B_ROPE_DIM)

    w["c_conv_w"] = jnp.pad(c_conv_w[l], ((0, SUBLANES - 4), (0, 0)))
    w["c_conv_b"] = row(c_conv_b[l])
    w["c_wf"] = jnp.concatenate([_block_diag(c_wr[l, 0]), _block_diag(c_wi[l, 0])], axis=1).astype(BF16)
    w["c_wb"] = jnp.concatenate([_block_diag(c_wr[l, 1]), _block_diag(c_wi[l, 1])], axis=1).astype(BF16)
    w["c_bf"] = row(jnp.concatenate([c_br[l, 0], c_bi[l, 0]]))
    w["c_bb"] = row(jnp.concatenate([c_br[l, 1], c_bi[l, 1]]))
    w["c_lam"] = jnp.pad(c_lam[l], ((0, SUBLANES - 2), (0, 0)))

    w["d_conv_w"] = jnp.pad(d_conv_w[l], ((0, SUBLANES - 4), (0, 0)))
    dconst = jnp.zeros((SUBLANES, LANES), F32)
    dconst = dconst.at[0, 0:2 * D_HEADS].set(d_a_log[l].reshape(-1))
    dconst = dconst.at[1, 0:2 * D_HEADS].set(d_dt_bias[l].reshape(-1))
    w["d_const"] = dconst
    w["d_on_g"] = row(d_on_g[l])

    wbr = w_branch[l]
    wbr_b = jnp.pad(wbr[A_WIDTH:A_WIDTH + B_HEADS * B_V_DIM].reshape(B_HEADS, B_V_DIM, D_MODEL),
                    ((0, 0), (0, LANES - B_V_DIM), (0, 0))).reshape(B_HEADS * LANES, D_MODEL)
    w["w_branch"] = jnp.concatenate([wbr[:A_WIDTH], wbr_b, wbr[A_WIDTH + B_HEADS * B_V_DIM:]], axis=0).astype(BF16)
    w["w_out"] = w_out[l].astype(BF16)
    w["ln2"] = row(ln2_g[l])
    w["w_up"] = w_up[l].astype(BF16)
    w["w_down"] = w_down[l].astype(BF16)
    return w


def _layer(x, w):
    aq, ak, av, bq, bk, bv, cx, cg, dqkv, dz, dab, gates = _in_proj(x, w)
    a_outs = [_a_attn(aq, ak, av, dil) for dil in A_DILATIONS]
    ob = _b_flash(bq, bk, bv)
    hf, hb = _rglru(cx, w)
    dq, dk, dv, dgb = _d_prep(dqkv, dab, w)
    df, db = _d_chunk(dq, dk, dv, dgb)
    x = _merge(x, a_outs, ob, hf, hb, cg, df, db, dz, gates, w)
    return _ffn(x, w)


def kernel(x_prompt, x_sample, ln1_g, w_in, a_qn_g, a_kn_g, b_qa_g, b_wuq, b_kva_g, b_wukv, b_qn_g, b_kn_g,
           c_conv_w, c_conv_b, c_wr, c_br, c_wi, c_bi, c_lam, d_conv_w, d_a_log, d_dt_bias, d_on_g,
           w_branch, w_out, ln2_g, w_up, w_down):
    depth = ln1_g.shape[0]
    weights = [_layer_weights(l, ln1_g, w_in, a_qn_g, a_kn_g, b_qa_g, b_wuq, b_kva_g, b_wukv, b_qn_g, b_kn_g,
                              c_conv_w, c_conv_b, c_wr, c_br, c_wi, c_bi, c_lam, d_conv_w, d_a_log, d_dt_bias,
                              d_on_g, w_branch, w_out, ln2_g, w_up, w_down) for l in range(depth)]

    def trunk(x):
        for w in weights:
            x = _layer(x, w)
        return x

    return (trunk(x_prompt), trunk(x_sample))
```

```python
import functools

import jax
import jax.numpy as jnp
import numpy as np
from jax import lax
from jax.experimental import pallas as pl
from jax.experimental.pallas import tpu as pltpu

F32 = jnp.float32
BF16 = jnp.bfloat16

D_MODEL = 1024
ROPE_THETA = 500000.0
NORM_EPS = 1e-6
NEG_INF = -1e30
LOG2_E = 1.4426950408889634

A_HEADS = 8
A_HEAD_DIM = 64
A_ROPE_DIM = 16
A_DILATIONS = (1, 4, 16)
A_HALF = 64
A_WIDTH = A_HEADS * A_HEAD_DIM

B_HEADS = 8
B_NOPE_DIM = 64
B_ROPE_DIM = 32
B_QK_DIM = B_NOPE_DIM + B_ROPE_DIM
B_V_DIM = 64
B_Q_RANK = 256
B_KV_RANK = 128

C_WIDTH = 512
C_BLOCKS = 8
C_BLOCK_DIM = 64
C_GATE_C = 8.0

D_HEADS = 4
D_HEAD_DIM = 128
D_WIDTH = 512
D_CHUNK = 64

FF_DIM = 2816
LANES = 128
SUBLANES = 8
VMEM_LIMIT = 56 * 1024 * 1024

_OFF = np.cumsum([0, 512, 512, 512, 256, 128, 32, 512, 512, 512, 512, 512, 512, 8, 8, 4096])
(O_AQ, O_AK, O_AV, O_BCQ, O_BCKV, O_BKR, O_CX, O_CG, O_DQ, O_DK, O_DV, O_DZ, O_DA, O_DB, O_GATE, O_END) = [int(v) for v in _OFF]


def _const_spec(shape):
    nd = len(shape)
    return pl.BlockSpec(shape, lambda *_: (0,) * nd, pipeline_mode=pl.Buffered(1))


def _params(sem):
    return pltpu.CompilerParams(dimension_semantics=sem, vmem_limit_bytes=VMEM_LIMIT)


def _dot(a, b):
    return jnp.dot(a, b, preferred_element_type=F32)


def _dot_nt(a, b):
    return lax.dot_general(a, b, (((1,), (1,)), ((), ())), preferred_element_type=F32)


def _dot_tn(a, b):
    return lax.dot_general(a, b, (((0,), (0,)), ((), ())), preferred_element_type=F32)


def _rms_rows(x, g):
    return x * lax.rsqrt(jnp.mean(x * x, axis=-1, keepdims=True) + NORM_EPS) * g


def _split2(x):
    hi = x.astype(BF16)
    lo = (x - hi.astype(F32)).astype(BF16)
    return hi, lo


def _split3(x):
    p1 = x.astype(BF16)
    r1 = x - p1.astype(F32)
    p2 = r1.astype(BF16)
    p3 = (r1 - p2.astype(F32)).astype(BF16)
    return p1, p2, p3


def _sigmoid(x):
    return 1.0 / (1.0 + jnp.exp(-x))


def _silu(x):
    return x * _sigmoid(x)


def _softplus(x):
    return jnp.maximum(x, 0.0) + jnp.log(1.0 + jnp.exp(-jnp.abs(x)))


def _rope_tables(pos0, rows, consts):
    pos = (pos0 + lax.broadcasted_iota(jnp.int32, (rows, LANES), 0)).astype(F32)
    ang = pos * consts[0:1, :]
    c = jnp.cos(ang)
    s = jnp.sin(ang)
    return c, s * consts[1:2, :], s * consts[2:3, :]


def _in_proj_body(x_ref, ln_ref, wa_ref, wb_ref, wc_ref, wd_ref, wab_ref, wg_ref,
                  gmat_ref, aqg_ref, akg_ref, arope_ref,
                  qag_ref, wuq_ref, kvag_ref, wukv_ref, bqg_ref, bkg_ref, brope_ref,
                  aq_ref, ak_ref, av_ref, bq_ref, bk_ref, bv_ref,
                  cx_ref, cg_ref, dqkv_ref, dz_ref, dab_ref, gate_ref, *, tm):
    pos0 = pl.program_id(1) * tm
    xn = _rms_rows(x_ref[0], ln_ref[...]).astype(BF16)

    ca, sa1, sa2 = _rope_tables(pos0, tm, arope_ref[...])
    ca = jnp.concatenate([ca] * 4, axis=1)
    sa1 = jnp.concatenate([sa1] * 4, axis=1)
    sa2 = jnp.concatenate([sa2] * 4, axis=1)
    gmat = gmat_ref[...]

    def a_norm_rope(t, g):
        hi, lo = _split2(t * t)
        ms = (_dot(hi, gmat) + _dot(lo, gmat)) * (1.0 / A_HEAD_DIM)
        y = t * lax.rsqrt(ms + NORM_EPS) * g
        return y * ca + pltpu.roll(y, A_WIDTH - A_ROPE_DIM // 2, 1) * sa1 + pltpu.roll(y, A_ROPE_DIM // 2, 1) * sa2

    q = _dot(xn, wa_ref[:, 0:A_WIDTH])
    aq_ref[0] = (a_norm_rope(q, aqg_ref[...]) * (A_HEAD_DIM ** -0.5)).astype(BF16)
    k = _dot(xn, wa_ref[:, A_WIDTH:2 * A_WIDTH])
    ak_ref[0] = a_norm_rope(k, akg_ref[...]).astype(BF16)
    av_ref[0] = _dot(xn, wa_ref[:, 2 * A_WIDTH:3 * A_WIDTH]).astype(BF16)

    hb = _dot(xn, wb_ref[...])
    cq = _rms_rows(hb[:, 0:B_Q_RANK], qag_ref[...]).astype(BF16)
    ckv = _rms_rows(hb[:, B_Q_RANK:B_Q_RANK + B_KV_RANK], kvag_ref[...]).astype(BF16)
    kr = hb[:, B_Q_RANK + B_KV_RANK:]
    cb, sb1, sb2 = _rope_tables(pos0, tm, brope_ref[...])
    lane = lax.broadcasted_iota(jnp.int32, (tm, LANES), 1)
    ones_col = jnp.where(lane == B_V_DIM, 1.0, 0.0)

    def b_norm_rope(t, g):
        ms = jnp.sum(t * t, axis=-1, keepdims=True) * (1.0 / B_QK_DIM)
        y = t * lax.rsqrt(ms + NORM_EPS) * g
        return y * cb + pltpu.roll(y, LANES - B_ROPE_DIM // 2, 1) * sb1 + pltpu.roll(y, B_ROPE_DIM // 2, 1) * sb2

    for h in range(B_HEADS):
        sl = slice(h * LANES, (h + 1) * LANES)
        qh = _dot(cq, wuq_ref[:, sl])
        bq_ref[0, h] = (b_norm_rope(qh, bqg_ref[...]) * (B_QK_DIM ** -0.5 * LOG2_E)).astype(BF16)
        kh = _dot(ckv, wukv_ref[:, sl]) + kr
        bk_ref[0, h] = b_norm_rope(kh, bkg_ref[...]).astype(BF16)
        vh = _dot(ckv, wukv_ref[:, B_HEADS * LANES + h * LANES:B_HEADS * LANES + (h + 1) * LANES])
        bv_ref[0, h] = (vh + ones_col).astype(BF16)

    hc = _dot(xn, wc_ref[...])
    cx_ref[0] = hc[:, 0:C_WIDTH]
    cg_ref[0] = hc[:, C_WIDTH:].astype(BF16)
    for j in range(3):
        dqkv_ref[0, :, j * D_WIDTH:(j + 1) * D_WIDTH] = _dot(xn, wd_ref[:, j * D_WIDTH:(j + 1) * D_WIDTH])
    dz_ref[0] = _dot(xn, wd_ref[:, 3 * D_WIDTH:]).astype(BF16)
    dab_ref[0] = _dot(xn, wab_ref[...])
    for j in range(4 * D_MODEL // 512):
        gate_ref[0, :, j * 512:(j + 1) * 512] = _dot(xn, wg_ref[:, j * 512:(j + 1) * 512]).astype(BF16)


def _in_proj(x, w, *, tm=256):
    bsz, seq, _ = x.shape
    grid = (bsz, seq // tm)
    tok = lambda width: pl.BlockSpec((1, tm, width), lambda b, i: (b, i, 0))
    head = pl.BlockSpec((1, B_HEADS, tm, LANES), lambda b, i: (b, 0, i, 0))
    consts = [w["ln1"], w["wa"], w["wb"], w["wc"], w["wd"], w["wab"], w["wg"],
              w["gmat"], w["aqg"], w["akg"], w["arope"],
              w["qag"], w["wuq"], w["kvag"], w["wukv"], w["bqg"], w["bkg"], w["brope"]]
    tokf = lambda width, dt: jax.ShapeDtypeStruct((bsz, seq, width), dt)
    headf = jax.ShapeDtypeStruct((bsz, B_HEADS, seq, LANES), BF16)
    return pl.pallas_call(
        functools.partial(_in_proj_body, tm=tm),
        grid=grid,
        in_specs=[tok(D_MODEL)] + [_const_spec(c.shape) for c in consts],
        out_specs=[tok(A_WIDTH), tok(A_WIDTH), tok(A_WIDTH), head, head, head,
                   tok(C_WIDTH), tok(C_WIDTH), tok(3 * D_WIDTH), tok(D_WIDTH), tok(LANES), tok(4 * D_MODEL)],
        out_shape=[tokf(A_WIDTH, BF16), tokf(A_WIDTH, BF16), tokf(A_WIDTH, BF16), headf, headf, headf,
                   tokf(C_WIDTH, F32), tokf(C_WIDTH, BF16), tokf(3 * D_WIDTH, F32), tokf(D_WIDTH, BF16),
                   tokf(LANES, F32), tokf(4 * D_MODEL, BF16)],
        compiler_params=_params(("parallel", "parallel")),
        name="in_proj",
    )(x, *consts)


def _a_attn_body(q_ref, k_ref, kp_ref, kn_ref, v_ref, vp_ref, vn_ref, o_ref, lse_ref, kbuf, vbuf, *, lq, sub_len):
    blk = A_HALF
    i = pl.program_id(2)
    kbuf[0:blk] = kp_ref[0]
    kbuf[blk:blk + lq] = k_ref[0]
    kbuf[blk + lq:] = kn_ref[0]
    vbuf[0:blk] = vp_ref[0]
    vbuf[blk:blk + lq] = v_ref[0]
    vbuf[blk + lq:] = vn_ref[0]
    kw = 3 * blk
    rel = lax.broadcasted_iota(jnp.int32, (blk, kw), 1) - blk - lax.broadcasted_iota(jnp.int32, (blk, kw), 0)
    band = jnp.abs(rel) <= A_HALF
    kcol = lax.broadcasted_iota(jnp.int32, (blk, kw), 1)
    lane = lax.broadcasted_iota(jnp.int32, (blk, LANES), 1)
    first = lane < A_HEAD_DIM

    def sub_block(j, carry):
        r0 = pl.multiple_of(j * blk, blk)
        kpos = i * lq + r0 - blk + kcol
        valid = band & (kpos >= 0) & (kpos < sub_len)
        slabs = [slice(hp * LANES, (hp + 1) * LANES) for hp in range(A_HEADS // 2)]
        v2s = [vbuf[pl.ds(r0, kw), sl] for sl in slabs]
        scores = []
        for sl in slabs:
            q2 = q_ref[0, pl.ds(r0, blk), sl]
            k2 = kbuf[pl.ds(r0, kw), sl]
            zero = jnp.zeros_like(q2)
            scores.append([_dot_nt(qh, k2) for qh in (jnp.where(first, q2, zero), jnp.where(first, zero, q2))])
        probs, stats = [], []
        for pair in scores:
            pp, st = [], []
            for s in pair:
                s = jnp.where(valid, s, NEG_INF)
                m = jnp.max(s, axis=-1, keepdims=True)
                p = jnp.exp(s - m)
                z = jnp.sum(p, axis=-1, keepdims=True)
                pp.append(p.astype(BF16))
                st.append((m, z))
            probs.append(pp)
            stats.append(st)
        outs = [[_dot(p, v2) for p in pp] for pp, v2 in zip(probs, v2s)]
        for sl, (oa, ob), ((ma, za), (mb, zb)) in zip(slabs, outs, stats):
            o_ref[0, pl.ds(r0, blk), sl] = jnp.where(first, oa / za, ob / zb)
            lse_ref[0, pl.ds(r0, blk), sl] = jnp.where(first, ma + jnp.log(za), mb + jnp.log(zb))
        return carry

    lax.fori_loop(0, lq // blk, sub_block, 0, unroll=2)


def _a_attn(q, k, v, dil):
    bsz, seq, width = q.shape
    sub_len = seq // dil
    lq = min(512, sub_len)
    nblk = sub_len // A_HALF
    per = lq // A_HALF
    view = lambda t: t.reshape(bsz, sub_len, dil * width)
    main = pl.BlockSpec((1, lq, width), lambda b, r, i: (b, i, r))
    prev = pl.BlockSpec((1, A_HALF, width), lambda b, r, i: (b, jnp.maximum(i * per - 1, 0), r))
    nxt = pl.BlockSpec((1, A_HALF, width), lambda b, r, i: (b, jnp.minimum((i + 1) * per, nblk - 1), r))
    qv, kv, vv = view(q), view(k), view(v)
    o, lse = pl.pallas_call(
        functools.partial(_a_attn_body, lq=lq, sub_len=sub_len),
        grid=(bsz, dil, sub_len // lq),
        in_specs=[main, main, prev, nxt, main, prev, nxt],
        out_specs=[main, main],
        out_shape=[jax.ShapeDtypeStruct(qv.shape, F32), jax.ShapeDtypeStruct(qv.shape, F32)],
        scratch_shapes=[pltpu.VMEM((lq + 2 * A_HALF, width), BF16), pltpu.VMEM((lq + 2 * A_HALF, width), BF16)],
        compiler_params=_params(("parallel", "parallel", "parallel")),
        name=f"a_attn_d{dil}",
    )(qv, kv, kv, kv, vv, vv, vv)
    return o.reshape(bsz, seq, width), lse.reshape(bsz, seq, width)


def _b_flash_body(q_ref, k_ref, v_ref, o_ref, *, tq, tk, seq, unroll):
    q = q_ref[0, 0]
    nk = seq // tk

    def scores(c):
        return _dot_nt(q, k_ref[0, 0, pl.ds(pl.multiple_of(c * tk, tk), tk), :])

    def update(c, m, acc, s):
        m_new = jnp.maximum(m, jnp.max(s, axis=-1, keepdims=True))
        alpha = jnp.exp2(m - m_new)
        p = jnp.exp2(s - m_new).astype(BF16)
        acc = alpha * acc + _dot(p, v_ref[0, 0, pl.ds(pl.multiple_of(c * tk, tk), tk), :])
        return m_new, acc

    def chunk(c, carry):
        m, acc, s = carry
        s_next = scores(jnp.minimum(c + 1, nk - 1))
        m, acc = update(c, m, acc, s)
        return m, acc, s_next

    m0 = jnp.full((tq, 1), NEG_INF, F32)
    acc0 = jnp.zeros((tq, LANES), F32)
    _, acc, _ = lax.fori_loop(0, nk, chunk, (m0, acc0, scores(0)), unroll=unroll)
    lane = lax.broadcasted_iota(jnp.int32, (tq, LANES), 1)
    denom = jnp.sum(jnp.where(lane == B_V_DIM, acc, 0.0), axis=-1, keepdims=True)
    o_ref[0] = jnp.where(lane < B_V_DIM, acc / denom, 0.0).astype(BF16)


def _b_flash(q, k, v, *, tq=256, tk=512, unroll=16):
    bsz, heads, seq, _ = q.shape
    return pl.pallas_call(
        functools.partial(_b_flash_body, tq=tq, tk=tk, seq=seq, unroll=unroll),
        grid=(bsz, heads, seq // tq),
        in_specs=[pl.BlockSpec((1, 1, tq, LANES), lambda b, h, i: (b, h, i, 0)),
                  pl.BlockSpec((1, 1, seq, LANES), lambda b, h, i: (b, h, 0, 0)),
                  pl.BlockSpec((1, 1, seq, LANES), lambda b, h, i: (b, h, 0, 0))],
        out_specs=pl.BlockSpec((1, tq, LANES), lambda b, h, i: (b, i, h)),
        out_shape=jax.ShapeDtypeStruct((bsz, seq, heads * LANES), BF16),
        compiler_params=_params(("parallel", "parallel", "parallel")),
        name="b_flash",
    )(q, k, v)


def _fill_ext(ext, prev_ref, main_ref, next_ref, is_first, is_last, ts):
    h = SUBLANES
    ext[0:h] = jnp.where(is_first, 0.0, prev_ref[0])
    ext[h:h + ts] = main_ref[0]
    ext[h + ts:] = jnp.where(is_last, 0.0, next_ref[0])


def _conv4(ext, w_ref, ts):
    h = SUBLANES
    out = ext[h - 2:h - 2 + ts] * w_ref[0:1, :]
    for j in range(1, 4):
        out = out + ext[h - 2 + j:h - 2 + j + ts] * w_ref[j:j + 1, :]
    return out


def _halo_specs(ts, width, nt, rev):
    per = ts // SUBLANES
    nblk = nt * per
    t = (lambda i: nt - 1 - i) if rev else (lambda i: i)
    main = pl.BlockSpec((1, ts, width), lambda b, i: (b, t(i), 0))
    prev = pl.BlockSpec((1, SUBLANES, width), lambda b, i: (b, jnp.maximum(t(i) * per - 1, 0), 0))
    nxt = pl.BlockSpec((1, SUBLANES, width), lambda b, i: (b, jnp.minimum((t(i) + 1) * per, nblk - 1), 0))
    return main, prev, nxt


def _scan_tile(a_s, u_s, h_ref, carry_ref, ts, reverse):
    n = ts // SUBLANES
    row = lax.broadcasted_iota(jnp.int32, (SUBLANES, C_WIDTH), 0)

    def group(g, carry):
        gi = (n - 1 - g) if reverse else g
        r0 = pl.multiple_of(gi * SUBLANES, SUBLANES)
        a = a_s[pl.ds(r0, SUBLANES), :]
        b = u_s[pl.ds(r0, SUBLANES), :]
        for sh in (1, 2, 4):
            if reverse:
                keep = row < SUBLANES - sh
                shift = SUBLANES - sh
            else:
                keep = row >= sh
                shift = sh
            a_sh = pltpu.roll(a, shift, 0)
            b_sh = pltpu.roll(b, shift, 0)
            b = jnp.where(keep, a * b_sh + b, b)
            a = jnp.where(keep, a * a_sh, a)
        h = b + a * carry
        h_ref[0, pl.ds(r0, SUBLANES), :] = h
        last = h[0:1, :] if reverse else h[SUBLANES - 1:SUBLANES, :]
        return jnp.broadcast_to(last, (SUBLANES, C_WIDTH))

    carry_ref[...] = lax.fori_loop(0, n, group, carry_ref[...])


def _rglru_body(xf_ref, xfp_ref, xfn_ref, xb_ref, xbp_ref, xbn_ref,
                cw_ref, cb_ref, wf_ref, wb_ref, bf_ref, bb_ref, lam_ref,
                hf_ref, hb_ref, ext, a_s, u_s, carry_f, carry_b, *, ts, nt):
    i = pl.program_id(1)

    @pl.when(i == 0)
    def _():
        carry_f[...] = jnp.zeros_like(carry_f)
        carry_b[...] = jnp.zeros_like(carry_b)

    def direction(d, main, prev, nxt, tile, w_ref, b_ref, h_ref, carry_ref):
        _fill_ext(ext, prev, main, nxt, tile == 0, tile == nt - 1, ts)
        xc = _conv4(ext, cw_ref, ts) + cb_ref[...]
        g = _dot(xc.astype(BF16), w_ref[...]) + b_ref[...]
        r = _sigmoid(g[:, 0:C_WIDTH])
        gi = _sigmoid(g[:, C_WIDTH:])
        log_a = (-C_GATE_C * _softplus(-lam_ref[d:d + 1, :])) * r
        a_s[...] = jnp.exp(log_a)
        u_s[...] = jnp.sqrt(1.0 - jnp.exp(2.0 * log_a)) * (gi * xc)
        _scan_tile(a_s, u_s, h_ref, carry_ref, ts, reverse=(d == 1))

    direction(0, xf_ref, xfp_ref, xfn_ref, i, wf_ref, bf_ref, hf_ref, carry_f)
    direction(1, xb_ref, xbp_ref, xbn_ref, nt - 1 - i, wb_ref, bb_ref, hb_ref, carry_b)


def _rglru(cx, w, *, ts=512):
    bsz, seq, _ = cx.shape
    ts = min(ts, seq)
    nt = seq // ts
    fm, fp, fn = _halo_specs(ts, C_WIDTH, nt, False)
    bm, bp, bn = _halo_specs(ts, C_WIDTH, nt, True)
    consts = [w["c_conv_w"], w["c_conv_b"], w["c_wf"], w["c_wb"], w["c_bf"], w["c_bb"], w["c_lam"]]
    out = jax.ShapeDtypeStruct((bsz, seq, C_WIDTH), F32)
    return pl.pallas_call(
        functools.partial(_rglru_body, ts=ts, nt=nt),
        grid=(bsz, nt),
        in_specs=[fm, fp, fn, bm, bp, bn] + [_const_spec(c.shape) for c in consts],
        out_specs=[fm, bm],
        out_shape=[out, out],
        scratch_shapes=[pltpu.VMEM((ts + 2 * SUBLANES, C_WIDTH), F32), pltpu.VMEM((ts, C_WIDTH), F32),
                        pltpu.VMEM((ts, C_WIDTH), F32), pltpu.VMEM((SUBLANES, C_WIDTH), F32),
                        pltpu.VMEM((SUBLANES, C_WIDTH), F32)],
        compiler_params=_params(("parallel", "arbitrary")),
        name="rglru",
    )(cx, cx, cx, cx, cx, cx, *consts)


def _d_prep_body(x_ref, xp_ref, xn_ref, ab_ref, cw_ref, dconst_ref, q_ref, k_ref, v_ref, gb_ref, ext, *, ts, nt):
    i = pl.program_id(1)
    _fill_ext(ext, xp_ref, x_ref, xn_ref, i == 0, i == nt - 1, ts)
    y = _silu(_conv4(ext, cw_ref, ts))
    for h in range(D_HEADS):
        for j, ref in ((0, q_ref), (1, k_ref)):
            t = y[:, j * D_WIDTH + h * LANES:j * D_WIDTH + (h + 1) * LANES]
            ref[0, :, h * LANES:(h + 1) * LANES] = t * lax.rsqrt(jnp.sum(t * t, axis=-1, keepdims=True) + NORM_EPS)
    v_ref[0] = y[:, 2 * D_WIDTH:]
    ab = ab_ref[0]
    lane = lax.broadcasted_iota(jnp.int32, ab.shape, 1)
    g = -jnp.exp(dconst_ref[0:1, :]) * _softplus(ab + dconst_ref[1:2, :])
    gb_ref[0] = jnp.where(lane < 2 * D_HEADS, g, _sigmoid(ab))


def _d_prep(dqkv, dab, w, *, ts=256):
    bsz, seq, width = dqkv.shape
    ts = min(ts, seq)
    nt = seq // ts
    main, prev, nxt = _halo_specs(ts, width, nt, False)
    tok = lambda wd: pl.BlockSpec((1, ts, wd), lambda b, i: (b, i, 0))
    o512 = jax.ShapeDtypeStruct((bsz, seq, D_WIDTH), F32)
    return pl.pallas_call(
        functools.partial(_d_prep_body, ts=ts, nt=nt),
        grid=(bsz, nt),
        in_specs=[main, prev, nxt, tok(LANES), _const_spec(w["d_conv_w"].shape), _const_spec(w["d_const"].shape)],
        out_specs=[tok(D_WIDTH), tok(D_WIDTH), tok(D_WIDTH), tok(LANES)],
        out_shape=[o512, o512, o512, jax.ShapeDtypeStruct((bsz, seq, LANES), F32)],
        scratch_shapes=[pltpu.VMEM((ts + 2 * SUBLANES, width), F32)],
        compiler_params=_params(("parallel", "parallel")),
        name="d_prep",
    )(dqkv, dqkv, dqkv, dab, w["d_conv_w"], w["d_const"])


def _d_chunk_body(qf_ref, kf_ref, vf_ref, gf_ref, qb_ref, kb_ref, vb_ref, gbk_ref,
                  of_ref, ob_ref, s_ref, gc_s, *, ts):
    i = pl.program_id(1)

    @pl.when(i == 0)
    def _():
        s_ref[...] = jnp.zeros_like(s_ref)

    c = D_CHUNK
    nc = ts // c
    scale = D_HEAD_DIM ** -0.5
    row_t = lax.broadcasted_iota(jnp.int32, (ts, ts), 0)
    col_t = lax.broadcasted_iota(jnp.int32, (ts, ts), 1)
    same = (row_t // c) == (col_t // c)
    ri = lax.broadcasted_iota(jnp.int32, (c, c), 0)
    ci = lax.broadcasted_iota(jnp.int32, (c, c), 1)
    lane = lax.broadcasted_iota(jnp.int32, (c, LANES), 1)
    ones3 = jnp.where(lane < 3, 1.0, 0.0).astype(BF16)

    dirs = ((0, qf_ref, kf_ref, vf_ref, gf_ref, of_ref), (1, qb_ref, kb_ref, vb_ref, gbk_ref, ob_ref))
    for d, q_ref, k_ref, v_ref, g_ref, o_ref in dirs:
        rev = d == 1
        before = (col_t >= row_t) if rev else (col_t <= row_t)
        tri = jnp.where(same & before, 1.0, 0.0).astype(BF16)
        p1, p2, p3 = _split3(g_ref[0])
        gc_s[d] = _dot(tri, p1) + _dot(tri, p2) + _dot(tri, p3)

    def chunk(step, carry):
        chains = []
        for d, q_ref, k_ref, v_ref, g_ref, o_ref in dirs:
            rev = d == 1
            cc = (nc - 1 - step) if rev else step
            r0 = pl.multiple_of(cc * c, c)
            gcb = gc_s[d, pl.ds(r0, c), :]
            gbb = g_ref[0, pl.ds(r0, c), :]
            for h in range(D_HEADS):
                sl = slice(h * LANES, (h + 1) * LANES)
                col = d * D_HEADS + h
                ch = dict(rev=rev, r0=r0, sl=sl, col=col, o_ref=o_ref,
                          incl=(ci >= ri) if rev else (ci <= ri), strict=(ci > ri) if rev else (ci < ri))
                gcol = jnp.sum(jnp.where(lane == col, gcb, 0.0), axis=-1, keepdims=True)
                ch["beta"] = jnp.sum(jnp.where(lane == 2 * D_HEADS + col, gbb, 0.0), axis=-1, keepdims=True)
                ch["gcol"] = gcol
                ch["gtot"] = gcol[0:1, :] if rev else gcol[c - 1:c, :]
                ch["q"] = q_ref[0, pl.ds(r0, c), sl]
                ch["k"] = k_ref[0, pl.ds(r0, c), sl]
                ch["v"] = v_ref[0, pl.ds(r0, c), sl]
                chains.append(ch)
        for ch in chains:
            g1, g2, g3 = (p.astype(F32) for p in _split3(ch["gcol"]))
            pieces = jnp.where(lane == 0, g1, jnp.where(lane == 1, g2, jnp.where(lane == 2, g3, 0.0)))
            ch["grow"] = _dot_nt(ones3, pieces.astype(BF16))
            ch["kbeta"] = ch["k"] * ch["beta"]
            kb16 = ch["k"].astype(BF16)
            ch["kk"] = _dot_nt(ch["kbeta"].astype(BF16), kb16)
            ch["qk"] = _dot_nt((ch["q"] * scale).astype(BF16), kb16)
        for ch in chains:
            decay = jnp.exp(jnp.where(ch["incl"], ch["gcol"] - ch["grow"], NEG_INF))
            lmat = jnp.where(ch["strict"], ch["kk"] * decay, 0.0)
            ch["attn"] = jnp.where(ch["incl"], ch["qk"] * decay, 0.0).astype(BF16)
            ch["nmat"] = -lmat
            x16 = lmat.astype(BF16)
            ch["xpow"] = _dot(x16, x16)
        for it in range(5):
            for ch in chains:
                x16 = ch["xpow"].astype(BF16)
                ch["nmat"] = ch["nmat"] + ch["xpow"] + _dot(ch["nmat"].astype(BF16), x16)
                if it < 4:
                    ch["xpow"] = _dot(x16, x16)
        for ch in chains:
            n16 = ch["nmat"].astype(BF16)
            vb = ch["v"] * ch["beta"]
            wk = ch["kbeta"] * jnp.exp(ch["gcol"])
            ch["u"] = vb + _dot(n16, vb.astype(BF16))
            ch["w"] = (wk + _dot(n16, wk.astype(BF16))).astype(BF16)
        for ch in chains:
            ch["state"] = s_ref[ch["col"]]
            ch["s16"] = ch["state"].astype(BF16)
            ch["vn16"] = (ch["u"] - _dot(ch["w"], ch["s16"])).astype(BF16)
        for ch in chains:
            qdec = (ch["q"] * (scale * jnp.exp(ch["gcol"]))).astype(BF16)
            ch["o_ref"][0, pl.ds(ch["r0"], c), ch["sl"]] = _dot(qdec, ch["s16"]) + _dot(ch["attn"], ch["vn16"])
            kdec = (ch["k"] * jnp.exp(ch["gtot"] - ch["gcol"])).astype(BF16)
            s_ref[ch["col"]] = ch["state"] * jnp.exp(ch["gtot"]) + _dot_tn(kdec, ch["vn16"])
        return carry

    lax.fori_loop(0, nc, chunk, 0)


def _d_chunk(q, k, v, gb, *, ts=256):
    bsz, seq, _ = q.shape
    ts = min(ts, seq)
    nt = seq // ts
    fwd = lambda wd: pl.BlockSpec((1, ts, wd), lambda b, i: (b, i, 0))
    bwd = lambda wd: pl.BlockSpec((1, ts, wd), lambda b, i: (b, nt - 1 - i, 0))
    out = jax.ShapeDtypeStruct((bsz, seq, D_WIDTH), F32)
    return pl.pallas_call(
        functools.partial(_d_chunk_body, ts=ts),
        grid=(bsz, nt),
        in_specs=[fwd(D_WIDTH), fwd(D_WIDTH), fwd(D_WIDTH), fwd(LANES),
                  bwd(D_WIDTH), bwd(D_WIDTH), bwd(D_WIDTH), bwd(LANES)],
        out_specs=[fwd(D_WIDTH), bwd(D_WIDTH)],
        out_shape=[out, out],
        scratch_shapes=[pltpu.VMEM((2 * D_HEADS, D_HEAD_DIM, D_HEAD_DIM), F32), pltpu.VMEM((2, ts, LANES), F32)],
        compiler_params=_params(("parallel", "arbitrary")),
        name="d_chunk",
    )(q, k, v, gb, q, k, v, gb)


def _gelu_tanh(x):
    return 0.5 * x * (1.0 + jnp.tanh(np.sqrt(2.0 / np.pi).astype(np.float32) * (x + 0.044715 * (x * x * x))))


def _merge_body(x_ref, o1_ref, l1_ref, o4_ref, l4_ref, o16_ref, l16_ref, ob_ref,
                hf_ref, hb_ref, cg_ref, df_ref, db_ref, dz_ref, gate_ref,
                wbr_ref, wout_ref, ong_ref, y_ref):
    l1, l4, l16 = l1_ref[0], l4_ref[0], l16_ref[0]
    m = jnp.maximum(jnp.maximum(l1, l4), l16)
    e1, e4, e16 = jnp.exp(l1 - m), jnp.exp(l4 - m), jnp.exp(l16 - m)
    oa = (e1 * o1_ref[0] + e4 * o4_ref[0] + e16 * o16_ref[0]) / (e1 + e4 + e16)
    oc = (hf_ref[0] + hb_ref[0]) * _gelu_tanh(cg_ref[0].astype(F32))
    od_sum = df_ref[0] + db_ref[0]
    z = dz_ref[0].astype(F32)
    heads = []
    for h in range(D_HEADS):
        sl = slice(h * LANES, (h + 1) * LANES)
        heads.append(_rms_rows(od_sum[:, sl], ong_ref[...]) * _silu(z[:, sl]))
    od = jnp.concatenate(heads, axis=1)

    def gate(j):
        return _sigmoid(gate_ref[0, :, j * D_MODEL:(j + 1) * D_MODEL].astype(F32))

    r_a, r_b, r_c = A_WIDTH, A_WIDTH + B_HEADS * LANES, A_WIDTH + B_HEADS * LANES + C_WIDTH
    merged = gate(0) * _dot(oa.astype(BF16), wbr_ref[0:r_a, :])
    merged = merged + gate(1) * _dot(ob_ref[0], wbr_ref[r_a:r_b, :])
    merged = merged + gate(2) * _dot(oc.astype(BF16), wbr_ref[r_b:r_c, :])
    merged = merged + gate(3) * _dot(od.astype(BF16), wbr_ref[r_c:, :])
    y_ref[0] = x_ref[0] + _dot(merged.astype(BF16), wout_ref[...])


def _merge(x, a_outs, ob, hf, hb, cg, df, db, dz, gates, w, *, tm=256):
    bsz, seq, _ = x.shape
    tok = lambda wd: pl.BlockSpec((1, tm, wd), lambda b, i: (b, i, 0))
    consts = [w["w_branch"], w["w_out"], w["d_on_g"]]
    ins = [x]
    specs = [tok(D_MODEL)]
    for o, lse in a_outs:
        ins += [o, lse]
        specs += [tok(A_WIDTH), tok(A_WIDTH)]
    ins += [ob, hf, hb, cg, df, db, dz, gates]
    specs += [tok(B_HEADS * LANES), tok(C_WIDTH), tok(C_WIDTH), tok(C_WIDTH), tok(D_WIDTH), tok(D_WIDTH),
              tok(D_WIDTH), tok(4 * D_MODEL)]
    return pl.pallas_call(
        _merge_body,
        grid=(bsz, seq // tm),
        in_specs=specs + [_const_spec(c.shape) for c in consts],
        out_specs=tok(D_MODEL),
        out_shape=jax.ShapeDtypeStruct(x.shape, F32),
        compiler_params=_params(("parallel", "parallel")),
        name="merge",
    )(*ins, *consts)


FF_TILE = 256


def _ffn_body(x_ref, ln_ref, wgu_ref, wdn_ref, y_ref):
    x = x_ref[0]
    xn = _rms_rows(x, ln_ref[...]).astype(BF16)
    acc = x
    for j in range(FF_DIM // FF_TILE):
        g = _dot(xn, wgu_ref[:, j * FF_TILE:(j + 1) * FF_TILE])
        u = _dot(xn, wgu_ref[:, FF_DIM + j * FF_TILE:FF_DIM + (j + 1) * FF_TILE])
        acc = acc + _dot((_silu(g) * u).astype(BF16), wdn_ref[j * FF_TILE:(j + 1) * FF_TILE, :])
    y_ref[0] = acc


def _ffn(x, w, *, tm=512):
    bsz, seq, _ = x.shape
    tok = pl.BlockSpec((1, tm, D_MODEL), lambda b, i: (b, i, 0))
    consts = [w["ln2"], w["w_up"], w["w_down"]]
    return pl.pallas_call(
        _ffn_body,
        grid=(bsz, seq // tm),
        in_specs=[tok] + [_const_spec(c.shape) for c in consts],
        out_specs=tok,
        out_shape=jax.ShapeDtypeStruct(x.shape, F32),
        compiler_params=_params(("parallel", "parallel")),
        name="ffn",
    )(x, *consts)


def _block_diag(wblocks):
    g, i, o = wblocks.shape
    eye = jnp.eye(g, dtype=wblocks.dtype)
    return (eye[:, None, :, None] * wblocks[:, :, None, :]).reshape(g * i, g * o)


def _rope_consts(head_period, rope_start, rope_dim):
    lane = np.arange(LANES) % head_period
    j = lane - rope_start
    in_rope = (j >= 0) & (j < rope_dim)
    half = rope_dim // 2
    inv_freq = 1.0 / (ROPE_THETA ** (jnp.arange(0, rope_dim, 2, dtype=F32) / rope_dim))
    inv_lane = jnp.where(jnp.asarray(in_rope), inv_freq[np.where(in_rope, j % half, 0)], 0.0)
    out = jnp.zeros((SUBLANES, LANES), F32)
    out = out.at[0].set(inv_lane)
    out = out.at[1].set(jnp.asarray(np.where(in_rope & (j < half), -1.0, 0.0), F32))
    out = out.at[2].set(jnp.asarray(np.where(in_rope & (j >= half), 1.0, 0.0), F32))
    return out


def _pad_heads(wmat, heads, dim):
    r = wmat.shape[0]
    w3 = wmat.reshape(r, heads, dim)
    return jnp.pad(w3, ((0, 0), (0, 0), (0, LANES - dim))).reshape(r, heads * LANES)


def _pad_lane_vec(vec, dim):
    return jnp.pad(vec, (0, LANES - dim)).reshape(1, LANES)


def _layer_weights(l, ln1_g, w_in, a_qn_g, a_kn_g, b_qa_g, b_wuq, b_kva_g, b_wukv, b_qn_g, b_kn_g,
                   c_conv_w, c_conv_b, c_wr, c_br, c_wi, c_bi, c_lam, d_conv_w, d_a_log, d_dt_bias, d_on_g,
                   w_branch, w_out, ln2_g, w_up, w_down):
    wi = w_in[l]
    row = lambda v: v.reshape(1, -1)
    w = {}
    w["ln1"] = row(ln1_g[l])
    w["wa"] = wi[:, O_AQ:O_BCQ].astype(BF16)
    zeros = lambda n: jnp.zeros((D_MODEL, n), F32)
    w["wb"] = jnp.concatenate([wi[:, O_BCQ:O_BKR], zeros(B_NOPE_DIM), wi[:, O_BKR:O_CX],
                               zeros(LANES - B_QK_DIM)], axis=1).astype(BF16)
    w["wc"] = wi[:, O_CX:O_DQ].astype(BF16)
    w["wd"] = wi[:, O_DQ:O_DA].astype(BF16)
    w["wab"] = jnp.concatenate([wi[:, O_DA:O_GATE], zeros(LANES - 4 * D_HEADS)], axis=1).astype(BF16)
    w["wg"] = wi[:, O_GATE:O_END].astype(BF16)
    w["gmat"] = _block_diag(jnp.ones((A_HEADS, A_HEAD_DIM, A_HEAD_DIM), F32)).astype(BF16)
    w["aqg"] = row(jnp.tile(a_qn_g[l], A_HEADS))
    w["akg"] = row(jnp.tile(a_kn_g[l], A_HEADS))
    w["arope"] = _rope_consts(A_HEAD_DIM, 0, A_ROPE_DIM)
    w["qag"] = row(b_qa_g[l])
    w["wuq"] = _pad_heads(b_wuq[l], B_HEADS, B_QK_DIM).astype(BF16)
    w["kvag"] = row(b_kva_g[l])
    wukv = b_wukv[l].reshape(B_KV_RANK, B_HEADS, B_NOPE_DIM + B_V_DIM)
    w["wukv"] = jnp.concatenate([
        _pad_heads(wukv[:, :, :B_NOPE_DIM].reshape(B_KV_RANK, -1), B_HEADS, B_NOPE_DIM),
        _pad_heads(wukv[:, :, B_NOPE_DIM:].reshape(B_KV_RANK, -1), B_HEADS, B_V_DIM)], axis=1).astype(BF16)
    w["bqg"] = _pad_lane_vec(b_qn_g[l], B_QK_DIM)
    w["bkg"] = _pad_lane_vec(b_kn_g[l], B_QK_DIM)
    w["brope"] = _rope_consts(LANES, B_NOPE_DIM, B_ROPE_DIM)

    w["c_conv_w"] = jnp.pad(c_conv_w[l], ((0, SUBLANES - 4), (0, 0)))
    w["c_conv_b"] = row(c_conv_b[l])
    w["c_wf"] = jnp.concatenate([_block_diag(c_wr[l, 0]), _block_diag(c_wi[l, 0])], axis=1).astype(BF16)
    w["c_wb"] = jnp.concatenate([_block_diag(c_wr[l, 1]), _block_diag(c_wi[l, 1])], axis=1).astype(BF16)
    w["c_bf"] = row(jnp.concatenate([c_br[l, 0], c_bi[l, 0]]))
    w["c_bb"] = row(jnp.concatenate([c_br[l, 1], c_bi[l, 1]]))
    w["c_lam"] = jnp.pad(c_lam[l], ((0, SUBLANES - 2), (0, 0)))

    w["d_conv_w"] = jnp.pad(d_conv_w[l], ((0, SUBLANES - 4), (0, 0)))
    dconst = jnp.zeros((SUBLANES, LANES), F32)
    dconst = dconst.at[0, 0:2 * D_HEADS].set(d_a_log[l].reshape(-1))
    dconst = dconst.at[1, 0:2 * D_HEADS].set(d_dt_bias[l].reshape(-1))
    w["d_const"] = dconst
    w["d_on_g"] = row(d_on_g[l])

    wbr = w_branch[l]
    wbr_b = jnp.pad(wbr[A_WIDTH:A_WIDTH + B_HEADS * B_V_DIM].reshape(B_HEADS, B_V_DIM, D_MODEL),
                    ((0, 0), (0, LANES - B_V_DIM), (0, 0))).reshape(B_HEADS * LANES, D_MODEL)
    w["w_branch"] = jnp.concatenate([wbr[:A_WIDTH], wbr_b, wbr[A_WIDTH + B_HEADS * B_V_DIM:]], axis=0).astype(BF16)
    w["w_out"] = w_out[l].astype(BF16)
    w["ln2"] = row(ln2_g[l])
    w["w_up"] = w_up[l].astype(BF16)
    w["w_down"] = w_down[l].astype(BF16)
    return w


def _layer(x, w):
    aq, ak, av, bq, bk, bv, cx, cg, dqkv, dz, dab, gates = _in_proj(x, w)
    a_outs = [_a_attn(aq, ak, av, dil) for dil in A_DILATIONS]
    ob = _b_flash(bq, bk, bv)
    hf, hb = _rglru(cx, w)
    dq, dk, dv, dgb = _d_prep(dqkv, dab, w)
    df, db = _d_chunk(dq, dk, dv, dgb)
    x = _merge(x, a_outs, ob, hf, hb, cg, df, db, dz, gates, w)
    return _ffn(x, w)


def kernel(x_prompt, x_sample, ln1_g, w_in, a_qn_g, a_kn_g, b_qa_g, b_wuq, b_kva_g, b_wukv, b_qn_g, b_kn_g,
           c_conv_w, c_conv_b, c_wr, c_br, c_wi, c_bi, c_lam, d_conv_w, d_a_log, d_dt_bias, d_on_g,
           w_branch, w_out, ln2_g, w_up, w_down):
    depth = ln1_g.shape[0]
    weights = [_layer_weights(l, ln1_g, w_in, a_qn_g, a_kn_g, b_qa_g, b_wuq, b_kva_g, b_wukv, b_qn_g, b_kn_g,
                              c_conv_w, c_conv_b, c_wr, c_br, c_wi, c_bi, c_lam, d_conv_w, d_a_log, d_dt_bias,
                              d_on_g, w_branch, w_out, ln2_g, w_up, w_down) for l in range(depth)]

    def trunk(x):
        for w in weights:
            x = _layer(x, w)
        return x

    return (trunk(x_prompt), trunk(x_sample))
```

```python
import functools

import jax
import jax.numpy as jnp
import numpy as np
from jax import lax
from jax.experimental import pallas as pl
from jax.experimental.pallas import tpu as pltpu

F32 = jnp.float32
BF16 = jnp.bfloat16

D_MODEL = 1024
ROPE_THETA = 500000.0
NORM_EPS = 1e-6
NEG_INF = -1e30
LOG2_E = 1.4426950408889634

A_HEADS = 8
A_HEAD_DIM = 64
A_ROPE_DIM = 16
A_DILATIONS = (1, 4, 16)
A_HALF = 64
A_WIDTH = A_HEADS * A_HEAD_DIM
A_TOKENS = 1024

B_HEADS = 8
B_NOPE_DIM = 64
B_ROPE_DIM = 32
B_QK_DIM = B_NOPE_DIM + B_ROPE_DIM
B_V_DIM = 64
B_Q_RANK = 256
B_KV_RANK = 128

C_WIDTH = 512
C_BLOCKS = 8
C_BLOCK_DIM = 64
C_GATE_C = 8.0

D_HEADS = 4
D_HEAD_DIM = 128
D_WIDTH = 512
D_CHUNK = 64

FF_DIM = 2816
LANES = 128
SUBLANES = 8
VMEM_LIMIT = 56 * 1024 * 1024

_OFF = np.cumsum([0, 512, 512, 512, 256, 128, 32, 512, 512, 512, 512, 512, 512, 8, 8, 4096])
(O_AQ, O_AK, O_AV, O_BCQ, O_BCKV, O_BKR, O_CX, O_CG, O_DQ, O_DK, O_DV, O_DZ, O_DA, O_DB, O_GATE, O_END) = [int(v) for v in _OFF]


def _const_spec(shape):
    nd = len(shape)
    return pl.BlockSpec(shape, lambda *_: (0,) * nd, pipeline_mode=pl.Buffered(1))


def _params(sem):
    return pltpu.CompilerParams(dimension_semantics=sem, vmem_limit_bytes=VMEM_LIMIT)


def _dot(a, b):
    return jnp.dot(a, b, preferred_element_type=F32)


def _dot_nt(a, b):
    return lax.dot_general(a, b, (((1,), (1,)), ((), ())), preferred_element_type=F32)


def _dot_tn(a, b):
    return lax.dot_general(a, b, (((0,), (0,)), ((), ())), preferred_element_type=F32)


def _rms_rows(x, g):
    return x * lax.rsqrt(jnp.mean(x * x, axis=-1, keepdims=True) + NORM_EPS) * g


def _split2(x):
    hi = x.astype(BF16)
    lo = (x - hi.astype(F32)).astype(BF16)
    return hi, lo


def _split3(x):
    p1 = x.astype(BF16)
    r1 = x - p1.astype(F32)
    p2 = r1.astype(BF16)
    p3 = (r1 - p2.astype(F32)).astype(BF16)
    return p1, p2, p3


def _sigmoid(x):
    return 1.0 / (1.0 + jnp.exp(-x))


def _silu(x):
    return x * _sigmoid(x)


def _softplus(x):
    return jnp.maximum(x, 0.0) + jnp.log(1.0 + jnp.exp(-jnp.abs(x)))


def _rope_tables(pos0, rows, consts):
    pos = (pos0 + lax.broadcasted_iota(jnp.int32, (rows, LANES), 0)).astype(F32)
    ang = pos * consts[0:1, :]
    c = jnp.cos(ang)
    s = jnp.sin(ang)
    return c, s * consts[1:2, :], s * consts[2:3, :]


def _in_proj_body(x_ref, ln_ref, wa_ref, wb_ref, wc_ref, wd_ref, wab_ref, wg_ref,
                  gmat_ref, aqg_ref, akg_ref, arope_ref,
                  qag_ref, wuq_ref, kvag_ref, wukv_ref, bqg_ref, bkg_ref, brope_ref,
                  aq1_ref, aq4_ref, aq16_ref, ak1_ref, ak4_ref, ak16_ref, av1_ref, av4_ref, av16_ref,
                  bq_ref, bk_ref, bv_ref,
                  cx_ref, cg_ref, dqkv_ref, dz_ref, dab_ref, gate_ref, perm_q, perm_k, perm_v, *, tm):
    aq_refs = (aq1_ref, aq4_ref, aq16_ref)
    ak_refs = (ak1_ref, ak4_ref, ak16_ref)
    av_refs = (av1_ref, av4_ref, av16_ref)
    pos0 = pl.program_id(1) * tm
    xn = _rms_rows(x_ref[0], ln_ref[...]).astype(BF16)

    ca, sa1, sa2 = _rope_tables(pos0, tm, arope_ref[...])
    ca = jnp.concatenate([ca] * 4, axis=1)
    sa1 = jnp.concatenate([sa1] * 4, axis=1)
    sa2 = jnp.concatenate([sa2] * 4, axis=1)
    gmat = gmat_ref[...]

    def a_norm_rope(t, g):
        hi, lo = _split2(t * t)
        ms = (_dot(hi, gmat) + _dot(lo, gmat)) * (1.0 / A_HEAD_DIM)
        y = t * lax.rsqrt(ms + NORM_EPS) * g
        return y * ca + pltpu.roll(y, A_WIDTH - A_ROPE_DIM // 2, 1) * sa1 + pltpu.roll(y, A_ROPE_DIM // 2, 1) * sa2

    def emit(y, refs, perm):
        refs[0][0] = y.astype(BF16)
        nslab = A_WIDTH // LANES
        for c in range(nslab):
            perm[c] = y[:, c * LANES:(c + 1) * LANES]
        for ref, d in zip(refs[1:], A_DILATIONS[1:]):
            for r in range(d):
                for c in range(nslab):
                    col = r * A_WIDTH + c * LANES
                    ref[0, :, col:col + LANES] = perm[c, pl.ds(r, tm // d, stride=d), :].astype(BF16)

    q = _dot(xn, wa_ref[:, 0:A_WIDTH])
    emit(a_norm_rope(q, aqg_ref[...]) * (A_HEAD_DIM ** -0.5), aq_refs, perm_q)
    k = _dot(xn, wa_ref[:, A_WIDTH:2 * A_WIDTH])
    emit(a_norm_rope(k, akg_ref[...]), ak_refs, perm_k)
    emit(_dot(xn, wa_ref[:, 2 * A_WIDTH:3 * A_WIDTH]), av_refs, perm_v)

    hb = _dot(xn, wb_ref[...])
    cq = _rms_rows(hb[:, 0:B_Q_RANK], qag_ref[...]).astype(BF16)
    ckv = _rms_rows(hb[:, B_Q_RANK:B_Q_RANK + B_KV_RANK], kvag_ref[...]).astype(BF16)
    kr = hb[:, B_Q_RANK + B_KV_RANK:]
    cb, sb1, sb2 = _rope_tables(pos0, tm, brope_ref[...])
    lane = lax.broadcasted_iota(jnp.int32, (tm, LANES), 1)
    ones_col = jnp.where(lane == B_V_DIM, 1.0, 0.0)

    def b_norm_rope(t, g):
        ms = jnp.sum(t * t, axis=-1, keepdims=True) * (1.0 / B_QK_DIM)
        y = t * lax.rsqrt(ms + NORM_EPS) * g
        return y * cb + pltpu.roll(y, LANES - B_ROPE_DIM // 2, 1) * sb1 + pltpu.roll(y, B_ROPE_DIM // 2, 1) * sb2

    for h in range(B_HEADS):
        sl = slice(h * LANES, (h + 1) * LANES)
        qh = _dot(cq, wuq_ref[:, sl])
        bq_ref[0, h] = (b_norm_rope(qh, bqg_ref[...]) * (B_QK_DIM ** -0.5 * LOG2_E)).astype(BF16)
        kh = _dot(ckv, wukv_ref[:, sl]) + kr
        bk_ref[0, h] = b_norm_rope(kh, bkg_ref[...]).astype(BF16)
        vh = _dot(ckv, wukv_ref[:, B_HEADS * LANES + h * LANES:B_HEADS * LANES + (h + 1) * LANES])
        bv_ref[0, h] = (vh + ones_col).astype(BF16)

    hc = _dot(xn, wc_ref[...])
    cx_ref[0] = hc[:, 0:C_WIDTH]
    cg_ref[0] = hc[:, C_WIDTH:].astype(BF16)
    for j in range(3):
        dqkv_ref[0, :, j * D_WIDTH:(j + 1) * D_WIDTH] = _dot(xn, wd_ref[:, j * D_WIDTH:(j + 1) * D_WIDTH])
    dz_ref[0] = _dot(xn, wd_ref[:, 3 * D_WIDTH:]).astype(BF16)
    dab_ref[0] = _dot(xn, wab_ref[...])
    for j in range(4 * D_MODEL // 512):
        gate_ref[0, :, j * 512:(j + 1) * 512] = _dot(xn, wg_ref[:, j * 512:(j + 1) * 512]).astype(BF16)


def _in_proj(x, w, *, tm=256):
    bsz, seq, _ = x.shape
    grid = (bsz, seq // tm)
    tok = lambda width: pl.BlockSpec((1, tm, width), lambda b, i: (b, i, 0))
    head = pl.BlockSpec((1, B_HEADS, tm, LANES), lambda b, i: (b, 0, i, 0))
    consts = [w["ln1"], w["wa"], w["wb"], w["wc"], w["wd"], w["wab"], w["wg"],
              w["gmat"], w["aqg"], w["akg"], w["arope"],
              w["qag"], w["wuq"], w["kvag"], w["wukv"], w["bqg"], w["bkg"], w["brope"]]
    tokf = lambda width, dt: jax.ShapeDtypeStruct((bsz, seq, width), dt)
    headf = jax.ShapeDtypeStruct((bsz, B_HEADS, seq, LANES), BF16)
    a_specs = [pl.BlockSpec((1, tm // d, d * A_WIDTH), lambda b, i: (b, i, 0)) for d in A_DILATIONS] * 3
    a_shapes = [jax.ShapeDtypeStruct((bsz, seq // d, d * A_WIDTH), BF16) for d in A_DILATIONS] * 3
    outs = pl.pallas_call(
        functools.partial(_in_proj_body, tm=tm),
        grid=grid,
        in_specs=[tok(D_MODEL)] + [_const_spec(c.shape) for c in consts],
        out_specs=a_specs + [head, head, head,
                             tok(C_WIDTH), tok(C_WIDTH), tok(3 * D_WIDTH), tok(D_WIDTH), tok(LANES), tok(4 * D_MODEL)],
        out_shape=a_shapes + [headf, headf, headf,
                              tokf(C_WIDTH, F32), tokf(C_WIDTH, BF16), tokf(3 * D_WIDTH, F32), tokf(D_WIDTH, BF16),
                              tokf(LANES, F32), tokf(4 * D_MODEL, BF16)],
        scratch_shapes=[pltpu.VMEM((A_WIDTH // LANES, tm, LANES), F32)] * 3,
        compiler_params=_params(("parallel", "parallel")),
        name="in_proj",
    )(x, *consts)
    nd = len(A_DILATIONS)
    return (outs[0:nd], outs[nd:2 * nd], outs[2 * nd:3 * nd]) + tuple(outs[3 * nd:])


def _a_attn_body(q_ref, k_ref, kp_ref, kn_ref, v_ref, vp_ref, vn_ref, o_ref, lse_ref, *, dil, lq, sub_len):
    blk = A_HALF
    nsub = lq // blk
    i = pl.program_id(1)
    kw = 3 * blk
    rel = lax.broadcasted_iota(jnp.int32, (blk, kw), 1) - blk - lax.broadcasted_iota(jnp.int32, (blk, kw), 0)
    band = jnp.abs(rel) <= A_HALF
    kcol = lax.broadcasted_iota(jnp.int32, (blk, kw), 1)
    lane = lax.broadcasted_iota(jnp.int32, (blk, LANES), 1)
    first = lane < A_HEAD_DIM

    def window(main_ref, prev_ref, next_ref, j, cols):
        parts = []
        for t in (j - 1, j, j + 1):
            if t < 0:
                parts.append(prev_ref[0, :, cols])
            elif t >= nsub:
                parts.append(next_ref[0, :, cols])
            else:
                parts.append(main_ref[0, t * blk:(t + 1) * blk, cols])
        return jnp.concatenate(parts, axis=0)

    for j in range(nsub):
        kpos = i * lq + (j - 1) * blk + kcol
        valid = band & (kpos >= 0) & (kpos < sub_len)
        for r in range(dil):
            cols = [slice(r * A_WIDTH + hp * LANES, r * A_WIDTH + (hp + 1) * LANES) for hp in range(A_HEADS // 2)]
            scores = []
            for cl in cols:
                q2 = q_ref[0, j * blk:(j + 1) * blk, cl]
                k2 = window(k_ref, kp_ref, kn_ref, j, cl)
                zero = jnp.zeros_like(q2)
                scores.append([_dot_nt(qh, k2) for qh in (jnp.where(first, q2, zero), jnp.where(first, zero, q2))])
            probs, stats = [], []
            for pair in scores:
                pp, st = [], []
                for s in pair:
                    s = jnp.where(valid, s, NEG_INF)
                    m = jnp.max(s, axis=-1, keepdims=True)
                    p = jnp.exp(s - m)
                    z = jnp.sum(p, axis=-1, keepdims=True)
                    pp.append(p.astype(BF16))
                    st.append((m, z))
                probs.append(pp)
                stats.append(st)
            outs = [[_dot(p, window(v_ref, vp_ref, vn_ref, j, cl)) for p in pp] for pp, cl in zip(probs, cols)]
            rows = pl.ds(j * blk * dil + r, blk, stride=dil) if dil > 1 else pl.ds(j * blk, blk)
            for hp, ((oa, ob), ((ma, za), (mb, zb))) in enumerate(zip(outs, stats)):
                o_ref[0, hp, rows, :] = jnp.where(first, oa / za, ob / zb)
                lse_ref[0, hp, rows, :] = jnp.where(first, ma + jnp.log(za), mb + jnp.log(zb))


def _a_attn(qv, kv, vv, dil):
    bsz, sub_len, vwidth = qv.shape
    seq = sub_len * dil
    lq = A_TOKENS // dil
    nblk = sub_len // A_HALF
    per = lq // A_HALF
    main = pl.BlockSpec((1, lq, vwidth), lambda b, i: (b, i, 0))
    prev = pl.BlockSpec((1, A_HALF, vwidth), lambda b, i: (b, jnp.maximum(i * per - 1, 0), 0))
    nxt = pl.BlockSpec((1, A_HALF, vwidth), lambda b, i: (b, jnp.minimum((i + 1) * per, nblk - 1), 0))
    nslab = A_WIDTH // LANES
    out = pl.BlockSpec((1, nslab, A_TOKENS, LANES), lambda b, i: (b, 0, i, 0))
    oshape = jax.ShapeDtypeStruct((bsz, nslab, seq, LANES), F32)
    return pl.pallas_call(
        functools.partial(_a_attn_body, dil=dil, lq=lq, sub_len=sub_len),
        grid=(bsz, seq // A_TOKENS),
        in_specs=[main, main, prev, nxt, main, prev, nxt],
        out_specs=[out, out],
        out_shape=[oshape, oshape],
        compiler_params=_params(("parallel", "parallel")),
        name=f"a_attn_d{dil}",
    )(qv, kv, kv, kv, vv, vv, vv)


def _b_flash_body(q_ref, k_ref, v_ref, o_ref, *, tq, tk, seq, unroll):
    q = q_ref[0, 0]
    nk = seq // tk

    def scores(c):
        return _dot_nt(q, k_ref[0, 0, pl.ds(pl.multiple_of(c * tk, tk), tk), :])

    def update(c, m, acc, s):
        m_new = jnp.maximum(m, jnp.max(s, axis=-1, keepdims=True))
        alpha = jnp.exp2(m - m_new)
        p = jnp.exp2(s - m_new).astype(BF16)
        acc = alpha * acc + _dot(p, v_ref[0, 0, pl.ds(pl.multiple_of(c * tk, tk), tk), :])
        return m_new, acc

    def chunk(c, carry):
        m, acc, s = carry
        s_next = scores(jnp.minimum(c + 1, nk - 1))
        m, acc = update(c, m, acc, s)
        return m, acc, s_next

    m = jnp.full((tq, 1), NEG_INF, F32)
    acc = jnp.zeros((tq, LANES), F32)
    if nk <= unroll:
        s = scores(0)
        for c in range(nk):
            s_next = scores(c + 1) if c + 1 < nk else None
            m, acc = update(c, m, acc, s)
            s = s_next
    else:
        _, acc, _ = lax.fori_loop(0, nk, chunk, (m, acc, scores(0)), unroll=unroll)
    lane = lax.broadcasted_iota(jnp.int32, (tq, LANES), 1)
    denom = jnp.sum(jnp.where(lane == B_V_DIM, acc, 0.0), axis=-1, keepdims=True)
    o_ref[0] = jnp.where(lane < B_V_DIM, acc / denom, 0.0).astype(BF16)


def _b_flash(q, k, v, *, tq=256, tk=512, unroll=16):
    bsz, heads, seq, _ = q.shape
    return pl.pallas_call(
        functools.partial(_b_flash_body, tq=tq, tk=tk, seq=seq, unroll=unroll),
        grid=(bsz, heads, seq // tq),
        in_specs=[pl.BlockSpec((1, 1, tq, LANES), lambda b, h, i: (b, h, i, 0)),
                  pl.BlockSpec((1, 1, seq, LANES), lambda b, h, i: (b, h, 0, 0)),
                  pl.BlockSpec((1, 1, seq, LANES), lambda b, h, i: (b, h, 0, 0))],
        out_specs=pl.BlockSpec((1, tq, LANES), lambda b, h, i: (b, i, h)),
        out_shape=jax.ShapeDtypeStruct((bsz, seq, heads * LANES), BF16),
        compiler_params=_params(("parallel", "parallel", "parallel")),
        name="b_flash",
    )(q, k, v)


def _fill_ext(ext, prev_ref, main_ref, next_ref, is_first, is_last, ts):
    h = SUBLANES
    ext[0:h] = jnp.where(is_first, 0.0, prev_ref[0])
    ext[h:h + ts] = main_ref[0]
    ext[h + ts:] = jnp.where(is_last, 0.0, next_ref[0])


def _conv4(ext, w_ref, ts):
    h = SUBLANES
    out = ext[h - 2:h - 2 + ts] * w_ref[0:1, :]
    for j in range(1, 4):
        out = out + ext[h - 2 + j:h - 2 + j + ts] * w_ref[j:j + 1, :]
    return out


def _halo_specs(ts, width, nt, rev):
    per = ts // SUBLANES
    nblk = nt * per
    t = (lambda i: nt - 1 - i) if rev else (lambda i: i)
    main = pl.BlockSpec((1, ts, width), lambda b, i: (b, t(i), 0))
    prev = pl.BlockSpec((1, SUBLANES, width), lambda b, i: (b, jnp.maximum(t(i) * per - 1, 0), 0))
    nxt = pl.BlockSpec((1, SUBLANES, width), lambda b, i: (b, jnp.minimum((t(i) + 1) * per, nblk - 1), 0))
    return main, prev, nxt


def _scan_tile(a_s, u_s, h_ref, carry_ref, ts, reverse):
    n = ts // SUBLANES
    row = lax.broadcasted_iota(jnp.int32, (SUBLANES, C_WIDTH), 0)

    def group(g, carry):
        gi = (n - 1 - g) if reverse else g
        r0 = pl.multiple_of(gi * SUBLANES, SUBLANES)
        a = a_s[pl.ds(r0, SUBLANES), :]
        b = u_s[pl.ds(r0, SUBLANES), :]
        for sh in (1, 2, 4):
            if reverse:
                keep = row < SUBLANES - sh
                shift = SUBLANES - sh
            else:
                keep = row >= sh
                shift = sh
            a_sh = pltpu.roll(a, shift, 0)
            b_sh = pltpu.roll(b, shift, 0)
            b = jnp.where(keep, a * b_sh + b, b)
            a = jnp.where(keep, a * a_sh, a)
        h = b + a * carry
        h_ref[0, pl.ds(r0, SUBLANES), :] = h
        last = h[0:1, :] if reverse else h[SUBLANES - 1:SUBLANES, :]
        return jnp.broadcast_to(last, (SUBLANES, C_WIDTH))

    carry_ref[...] = lax.fori_loop(0, n, group, carry_ref[...])


def _rglru_body(xf_ref, xfp_ref, xfn_ref, xb_ref, xbp_ref, xbn_ref,
                cw_ref, cb_ref, wf_ref, wb_ref, bf_ref, bb_ref, lam_ref,
                hf_ref, hb_ref, ext, a_s, u_s, carry_f, carry_b, *, ts, nt):
    i = pl.program_id(1)

    @pl.when(i == 0)
    def _():
        carry_f[...] = jnp.zeros_like(carry_f)
        carry_b[...] = jnp.zeros_like(carry_b)

    def direction(d, main, prev, nxt, tile, w_ref, b_ref, h_ref, carry_ref):
        _fill_ext(ext, prev, main, nxt, tile == 0, tile == nt - 1, ts)
        xc = _conv4(ext, cw_ref, ts) + cb_ref[...]
        g = _dot(xc.astype(BF16), w_ref[...]) + b_ref[...]
        r = _sigmoid(g[:, 0:C_WIDTH])
        gi = _sigmoid(g[:, C_WIDTH:])
        log_a = (-C_GATE_C * _softplus(-lam_ref[d:d + 1, :])) * r
        a = jnp.exp(log_a)
        a_s[...] = a
        u_s[...] = jnp.sqrt(1.0 - a * a) * (gi * xc)
        _scan_tile(a_s, u_s, h_ref, carry_ref, ts, reverse=(d == 1))

    direction(0, xf_ref, xfp_ref, xfn_ref, i, wf_ref, bf_ref, hf_ref, carry_f)
    direction(1, xb_ref, xbp_ref, xbn_ref, nt - 1 - i, wb_ref, bb_ref, hb_ref, carry_b)


def _rglru(cx, w, *, ts=512):
    bsz, seq, _ = cx.shape
    ts = min(ts, seq)
    nt = seq // ts
    fm, fp, fn = _halo_specs(ts, C_WIDTH, nt, False)
    bm, bp, bn = _halo_specs(ts, C_WIDTH, nt, True)
    consts = [w["c_conv_w"], w["c_conv_b"], w["c_wf"], w["c_wb"], w["c_bf"], w["c_bb"], w["c_lam"]]
    out = jax.ShapeDtypeStruct((bsz, seq, C_WIDTH), F32)
    return pl.pallas_call(
        functools.partial(_rglru_body, ts=ts, nt=nt),
        grid=(bsz, nt),
        in_specs=[fm, fp, fn, bm, bp, bn] + [_const_spec(c.shape) for c in consts],
        out_specs=[fm, bm],
        out_shape=[out, out],
        scratch_shapes=[pltpu.VMEM((ts + 2 * SUBLANES, C_WIDTH), F32), pltpu.VMEM((ts, C_WIDTH), F32),
                        pltpu.VMEM((ts, C_WIDTH), F32), pltpu.VMEM((SUBLANES, C_WIDTH), F32),
                        pltpu.VMEM((SUBLANES, C_WIDTH), F32)],
        compiler_params=_params(("parallel", "arbitrary")),
        name="rglru",
    )(cx, cx, cx, cx, cx, cx, *consts)


def _d_prep_body(x_ref, xp_ref, xn_ref, ab_ref, cw_ref, dconst_ref, q_ref, k_ref, v_ref, gb_ref, ext, *, ts, nt):
    i = pl.program_id(1)
    _fill_ext(ext, xp_ref, x_ref, xn_ref, i == 0, i == nt - 1, ts)
    y = _silu(_conv4(ext, cw_ref, ts))
    for h in range(D_HEADS):
        for j, ref in ((0, q_ref), (1, k_ref)):
            t = y[:, j * D_WIDTH + h * LANES:j * D_WIDTH + (h + 1) * LANES]
            ref[0, :, h * LANES:(h + 1) * LANES] = t * lax.rsqrt(jnp.sum(t * t, axis=-1, keepdims=True) + NORM_EPS)
    v_ref[0] = y[:, 2 * D_WIDTH:]
    ab = ab_ref[0]
    lane = lax.broadcasted_iota(jnp.int32, ab.shape, 1)
    g = -jnp.exp(dconst_ref[0:1, :]) * _softplus(ab + dconst_ref[1:2, :])
    gb_ref[0] = jnp.where(lane < 2 * D_HEADS, g, _sigmoid(ab))


def _d_prep(dqkv, dab, w, *, ts=256):
    bsz, seq, width = dqkv.shape
    ts = min(ts, seq)
    nt = seq // ts
    main, prev, nxt = _halo_specs(ts, width, nt, False)
    tok = lambda wd: pl.BlockSpec((1, ts, wd), lambda b, i: (b, i, 0))
    o512 = jax.ShapeDtypeStruct((bsz, seq, D_WIDTH), F32)
    return pl.pallas_call(
        functools.partial(_d_prep_body, ts=ts, nt=nt),
        grid=(bsz, nt),
        in_specs=[main, prev, nxt, tok(LANES), _const_spec(w["d_conv_w"].shape), _const_spec(w["d_const"].shape)],
        out_specs=[tok(D_WIDTH), tok(D_WIDTH), tok(D_WIDTH), tok(LANES)],
        out_shape=[o512, o512, o512, jax.ShapeDtypeStruct((bsz, seq, LANES), F32)],
        scratch_shapes=[pltpu.VMEM((ts + 2 * SUBLANES, width), F32)],
        compiler_params=_params(("parallel", "parallel")),
        name="d_prep",
    )(dqkv, dqkv, dqkv, dab, w["d_conv_w"], w["d_const"])


def _d_chunk_body(qf_ref, kf_ref, vf_ref, gf_ref, qb_ref, kb_ref, vb_ref, gbk_ref,
                  of_ref, ob_ref, s_ref, gc_s, u_s, w_s, at_s, qd_s, kd_s, eg_s, *, ts):
    i = pl.program_id(1)

    @pl.when(i == 0)
    def _():
        s_ref[...] = jnp.zeros_like(s_ref)

    c = D_CHUNK
    nc = ts // c
    scale = D_HEAD_DIM ** -0.5
    row_t = lax.broadcasted_iota(jnp.int32, (ts, ts), 0)
    col_t = lax.broadcasted_iota(jnp.int32, (ts, ts), 1)
    same = (row_t // c) == (col_t // c)
    ri = lax.broadcasted_iota(jnp.int32, (c, c), 0)
    ci = lax.broadcasted_iota(jnp.int32, (c, c), 1)
    lane = lax.broadcasted_iota(jnp.int32, (c, LANES), 1)
    ones3 = jnp.where(lane < 3, 1.0, 0.0).astype(BF16)

    dirs = ((0, qf_ref, kf_ref, vf_ref, gf_ref, of_ref), (1, qb_ref, kb_ref, vb_ref, gbk_ref, ob_ref))
    for d, q_ref, k_ref, v_ref, g_ref, o_ref in dirs:
        rev = d == 1
        before = (col_t >= row_t) if rev else (col_t <= row_t)
        tri = jnp.where(same & before, 1.0, 0.0).astype(BF16)
        p1, p2, p3 = _split3(g_ref[0])
        gc_s[d] = _dot(tri, p1) + _dot(tri, p2) + _dot(tri, p3)

    group = 4

    def prepare(gstep, carry):
        chains = []
        for d, q_ref, k_ref, v_ref, g_ref, o_ref in dirs:
            rev = d == 1
            for gi in range(group):
                cc = gstep * group + gi
                r0 = pl.multiple_of(cc * c, c)
                gcb = gc_s[d, pl.ds(r0, c), :]
                gbb = g_ref[0, pl.ds(r0, c), :]
                for h in range(D_HEADS):
                    sl = slice(h * LANES, (h + 1) * LANES)
                    col = d * D_HEADS + h
                    ch = dict(slot=(d * nc + cc) * D_HEADS + h,
                              incl=(ci >= ri) if rev else (ci <= ri), strict=(ci > ri) if rev else (ci < ri))
                    gcol = jnp.sum(jnp.where(lane == col, gcb, 0.0), axis=-1, keepdims=True)
                    ch["beta"] = jnp.sum(jnp.where(lane == 2 * D_HEADS + col, gbb, 0.0), axis=-1, keepdims=True)
                    ch["gcol"] = gcol
                    ch["gtot"] = gcol[0:1, :] if rev else gcol[c - 1:c, :]
                    ch["q"] = q_ref[0, pl.ds(r0, c), sl]
                    ch["k"] = k_ref[0, pl.ds(r0, c), sl]
                    ch["v"] = v_ref[0, pl.ds(r0, c), sl]
                    chains.append(ch)
        for ch in chains:
            g1, g2, g3 = (p.astype(F32) for p in _split3(ch["gcol"]))
            pieces = jnp.where(lane == 0, g1, jnp.where(lane == 1, g2, jnp.where(lane == 2, g3, 0.0)))
            ch["grow"] = _dot_nt(ones3, pieces.astype(BF16))
            ch["kbeta"] = ch["k"] * ch["beta"]
            kb16 = ch["k"].astype(BF16)
            ch["kk"] = _dot_nt(ch["kbeta"].astype(BF16), kb16)
            ch["qk"] = _dot_nt((ch["q"] * scale).astype(BF16), kb16)
        for ch in chains:
            decay = jnp.exp(jnp.where(ch["incl"], ch["gcol"] - ch["grow"], NEG_INF))
            lmat = jnp.where(ch["strict"], ch["kk"] * decay, 0.0)
            ch["attn"] = jnp.where(ch["incl"], ch["qk"] * decay, 0.0).astype(BF16)
            ch["nmat"] = -lmat
            x16 = lmat.astype(BF16)
            ch["xpow"] = _dot(x16, x16)
        for it in range(5):
            for ch in chains:
                x16 = ch["xpow"].astype(BF16)
                ch["nmat"] = ch["nmat"] + ch["xpow"] + _dot(ch["nmat"].astype(BF16), x16)
                if it < 4:
                    ch["xpow"] = _dot(x16, x16)
        for ch in chains:
            n16 = ch["nmat"].astype(BF16)
            vb = ch["v"] * ch["beta"]
            wk = ch["kbeta"] * jnp.exp(ch["gcol"])
            slot = ch["slot"]
            u_s[slot] = vb + _dot(n16, vb.astype(BF16))
            w_s[slot] = (wk + _dot(n16, wk.astype(BF16))).astype(BF16)
            at_s[slot] = ch["attn"]
            qd_s[slot] = (ch["q"] * (scale * jnp.exp(ch["gcol"]))).astype(BF16)
            kd_s[slot] = (ch["k"] * jnp.exp(ch["gtot"] - ch["gcol"])).astype(BF16)
            eg_s[slot] = jnp.broadcast_to(jnp.exp(ch["gtot"]), (1, LANES))
        return carry

    lax.fori_loop(0, nc // group, prepare, 0)

    def advance(step, carry):
        chains = []
        for d, q_ref, k_ref, v_ref, g_ref, o_ref in dirs:
            cc = (nc - 1 - step) if d == 1 else step
            for h in range(D_HEADS):
                chains.append(dict(slot=(d * nc + cc) * D_HEADS + h, col=d * D_HEADS + h, o_ref=o_ref,
                                   r0=pl.multiple_of(cc * c, c), sl=slice(h * LANES, (h + 1) * LANES)))
        for ch in chains:
            ch["state"] = s_ref[ch["col"]]
            ch["s16"] = ch["state"].astype(BF16)
            ch["vn16"] = (u_s[ch["slot"]] - _dot(w_s[ch["slot"]], ch["s16"])).astype(BF16)
        for ch in chains:
            slot = ch["slot"]
            ch["o_ref"][0, pl.ds(ch["r0"], c), ch["sl"]] = _dot(qd_s[slot], ch["s16"]) + _dot(at_s[slot], ch["vn16"])
            s_ref[ch["col"]] = ch["state"] * eg_s[slot] + _dot_tn(kd_s[slot], ch["vn16"])
        return carry

    lax.fori_loop(0, nc, advance, 0)


def _d_chunk(q, k, v, gb, *, ts=256):
    bsz, seq, _ = q.shape
    ts = min(ts, seq)
    nt = seq // ts
    fwd = lambda wd: pl.BlockSpec((1, ts, wd), lambda b, i: (b, i, 0))
    bwd = lambda wd: pl.BlockSpec((1, ts, wd), lambda b, i: (b, nt - 1 - i, 0))
    out = jax.ShapeDtypeStruct((bsz, seq, D_WIDTH), F32)
    nslot = 2 * (ts // D_CHUNK) * D_HEADS
    return pl.pallas_call(
        functools.partial(_d_chunk_body, ts=ts),
        grid=(bsz, nt),
        in_specs=[fwd(D_WIDTH), fwd(D_WIDTH), fwd(D_WIDTH), fwd(LANES),
                  bwd(D_WIDTH), bwd(D_WIDTH), bwd(D_WIDTH), bwd(LANES)],
        out_specs=[fwd(D_WIDTH), bwd(D_WIDTH)],
        out_shape=[out, out],
        scratch_shapes=[pltpu.VMEM((2 * D_HEADS, D_HEAD_DIM, D_HEAD_DIM), F32), pltpu.VMEM((2, ts, LANES), F32),
                        pltpu.VMEM((nslot, D_CHUNK, D_HEAD_DIM), F32), pltpu.VMEM((nslot, D_CHUNK, D_HEAD_DIM), BF16),
                        pltpu.VMEM((nslot, D_CHUNK, D_CHUNK), BF16), pltpu.VMEM((nslot, D_CHUNK, D_HEAD_DIM), BF16),
                        pltpu.VMEM((nslot, D_CHUNK, D_HEAD_DIM), BF16), pltpu.VMEM((nslot, 1, LANES), F32)],
        compiler_params=_params(("parallel", "arbitrary")),
        name="d_chunk",
    )(q, k, v, gb, q, k, v, gb)


def _gelu_tanh(x):
    return 0.5 * x * (1.0 + jnp.tanh(np.sqrt(2.0 / np.pi).astype(np.float32) * (x + 0.044715 * (x * x * x))))


def _merge_body(x_ref, o1_ref, l1_ref, o4_ref, l4_ref, o16_ref, l16_ref, ob_ref,
                hf_ref, hb_ref, cg_ref, df_ref, db_ref, dz_ref, gate_ref,
                wbr_ref, wout_ref, ong_ref, y_ref):
    slabs = []
    for c in range(A_WIDTH // LANES):
        l1, l4, l16 = l1_ref[0, c], l4_ref[0, c], l16_ref[0, c]
        m = jnp.maximum(jnp.maximum(l1, l4), l16)
        e1, e4, e16 = jnp.exp(l1 - m), jnp.exp(l4 - m), jnp.exp(l16 - m)
        slabs.append((e1 * o1_ref[0, c] + e4 * o4_ref[0, c] + e16 * o16_ref[0, c]) / (e1 + e4 + e16))
    oa = jnp.concatenate(slabs, axis=1)
    oc = (hf_ref[0] + hb_ref[0]) * _gelu_tanh(cg_ref[0].astype(F32))
    od_sum = df_ref[0] + db_ref[0]
    z = dz_ref[0].astype(F32)
    heads = []
    for h in range(D_HEADS):
        sl = slice(h * LANES, (h + 1) * LANES)
        heads.append(_rms_rows(od_sum[:, sl], ong_ref[...]) * _silu(z[:, sl]))
    od = jnp.concatenate(heads, axis=1)

    def gate(j):
        return _sigmoid(gate_ref[0, :, j * D_MODEL:(j + 1) * D_MODEL].astype(F32))

    r_a, r_b, r_c = A_WIDTH, A_WIDTH + B_HEADS * LANES, A_WIDTH + B_HEADS * LANES + C_WIDTH
    merged = gate(0) * _dot(oa.astype(BF16), wbr_ref[0:r_a, :])
    merged = merged + gate(1) * _dot(ob_ref[0], wbr_ref[r_a:r_b, :])
    merged = merged + gate(2) * _dot(oc.astype(BF16), wbr_ref[r_b:r_c, :])
    merged = merged + gate(3) * _dot(od.astype(BF16), wbr_ref[r_c:, :])
    y_ref[0] = x_ref[0] + _dot(merged.astype(BF16), wout_ref[...])


def _merge(x, a_outs, ob, hf, hb, cg, df, db, dz, gates, w, *, tm=256):
    bsz, seq, _ = x.shape
    tok = lambda wd: pl.BlockSpec((1, tm, wd), lambda b, i: (b, i, 0))
    consts = [w["w_branch"], w["w_out"], w["d_on_g"]]
    ins = [x]
    specs = [tok(D_MODEL)]
    slab = pl.BlockSpec((1, A_WIDTH // LANES, tm, LANES), lambda b, i: (b, 0, i, 0))
    for o, lse in a_outs:
        ins += [o, lse]
        specs += [slab, slab]
    ins += [ob, hf, hb, cg, df, db, dz, gates]
    specs += [tok(B_HEADS * LANES), tok(C_WIDTH), tok(C_WIDTH), tok(C_WIDTH), tok(D_WIDTH), tok(D_WIDTH),
              tok(D_WIDTH), tok(4 * D_MODEL)]
    return pl.pallas_call(
        _merge_body,
        grid=(bsz, seq // tm),
        in_specs=specs + [_const_spec(c.shape) for c in consts],
        out_specs=tok(D_MODEL),
        out_shape=jax.ShapeDtypeStruct(x.shape, F32),
        compiler_params=_params(("parallel", "parallel")),
        name="merge",
    )(*ins, *consts)


FF_TILE = 256


def _ffn_body(x_ref, ln_ref, wgu_ref, wdn_ref, y_ref):
    x = x_ref[0]
    xn = _rms_rows(x, ln_ref[...]).astype(BF16)
    acc = x
    for j in range(FF_DIM // FF_TILE):
        g = _dot(xn, wgu_ref[:, j * FF_TILE:(j + 1) * FF_TILE])
        u = _dot(xn, wgu_ref[:, FF_DIM + j * FF_TILE:FF_DIM + (j + 1) * FF_TILE])
        acc = acc + _dot((_silu(g) * u).astype(BF16), wdn_ref[j * FF_TILE:(j + 1) * FF_TILE, :])
    y_ref[0] = acc


def _ffn(x, w, *, tm=512):
    bsz, seq, _ = x.shape
    tok = pl.BlockSpec((1, tm, D_MODEL), lambda b, i: (b, i, 0))
    consts = [w["ln2"], w["w_up"], w["w_down"]]
    return pl.pallas_call(
        _ffn_body,
        grid=(bsz, seq // tm),
        in_specs=[tok] + [_const_spec(c.shape) for c in consts],
        out_specs=tok,
        out_shape=jax.ShapeDtypeStruct(x.shape, F32),
        compiler_params=_params(("parallel", "parallel")),
        name="ffn",
    )(x, *consts)


def _block_diag(wblocks):
    g, i, o = wblocks.shape
    eye = jnp.eye(g, dtype=wblocks.dtype)
    return (eye[:, None, :, None] * wblocks[:, :, None, :]).reshape(g * i, g * o)


def _rope_consts(head_period, rope_start, rope_dim):
    lane = np.arange(LANES) % head_period
    j = lane - rope_start
    in_rope = (j >= 0) & (j < rope_dim)
    half = rope_dim // 2
    inv_freq = 1.0 / (ROPE_THETA ** (jnp.arange(0, rope_dim, 2, dtype=F32) / rope_dim))
    inv_lane = jnp.where(jnp.asarray(in_rope), inv_freq[np.where(in_rope, j % half, 0)], 0.0)
    out = jnp.zeros((SUBLANES, LANES), F32)
    out = out.at[0].set(inv_lane)
    out = out.at[1].set(jnp.asarray(np.where(in_rope & (j < half), -1.0, 0.0), F32))
    out = out.at[2].set(jnp.asarray(np.where(in_rope & (j >= half), 1.0, 0.0), F32))
    return out


def _pad_heads(wmat, heads, dim):
    r = wmat.shape[0]
    w3 = wmat.reshape(r, heads, dim)
    return jnp.pad(w3, ((0, 0), (0, 0), (0, LANES - dim))).reshape(r, heads * LANES)


def _pad_lane_vec(vec, dim):
    return jnp.pad(vec, (0, LANES - dim)).reshape(1, LANES)


def _layer_weights(l, ln1_g, w_in, a_qn_g, a_kn_g, b_qa_g, b_wuq, b_kva_g, b_wukv, b_qn_g, b_kn_g,
                   c_conv_w, c_conv_b, c_wr, c_br, c_wi, c_bi, c_lam, d_conv_w, d_a_log, d_dt_bias, d_on_g,
                   w_branch, w_out, ln2_g, w_up, w_down):
    wi = w_in[l]
    row = lambda v: v.reshape(1, -1)
    w = {}
    w["ln1"] = row(ln1_g[l])
    w["wa"] = wi[:, O_AQ:O_BCQ].astype(BF16)
    zeros = lambda n: jnp.zeros((D_MODEL, n), F32)
    w["wb"] = jnp.concatenate([wi[:, O_BCQ:O_BKR], zeros(B_NOPE_DIM), wi[:, O_BKR:O_CX],
                               zeros(LANES - B_QK_DIM)], axis=1).astype(BF16)
    w["wc"] = wi[:, O_CX:O_DQ].astype(BF16)
    w["wd"] = wi[:, O_DQ:O_DA].astype(BF16)
    w["wab"] = jnp.concatenate([wi[:, O_DA:O_GATE], zeros(LANES - 4 * D_HEADS)], axis=1).astype(BF16)
    w["wg"] = wi[:, O_GATE:O_END].astype(BF16)
    w["gmat"] = _block_diag(jnp.ones((A_HEADS, A_HEAD_DIM, A_HEAD_DIM), F32)).astype(BF16)
    w["aqg"] = row(jnp.tile(a_qn_g[l], A_HEADS))
    w["akg"] = row(jnp.tile(a_kn_g[l], A_HEADS))
    w["arope"] = _rope_consts(A_HEAD_DIM, 0, A_ROPE_DIM)
    w["qag"] = row(b_qa_g[l])
    w["wuq"] = _pad_heads(b_wuq[l], B_HEADS, B_QK_DIM).astype(BF16)
    w["kvag"] = row(b_kva_g[l])
    wukv = b_wukv[l].reshape(B_KV_RANK, B_HEADS, B_NOPE_DIM + B_V_DIM)
    w["wukv"] = jnp.concatenate([
        _pad_heads(wukv[:, :, :B_NOPE_DIM].reshape(B_KV_RANK, -1), B_HEADS, B_NOPE_DIM),
        _pad_heads(wukv[:, :, B_NOPE_DIM:].reshape(B_KV_RANK, -1), B_HEADS, B_V_DIM)], axis=1).astype(BF16)
    w["bqg"] = _pad_lane_vec(b_qn_g[l], B_QK_DIM)
    w["bkg"] = _pad_lane_vec(b_kn_g[l], B_QK_DIM)
    w["brope"] = _rope_consts(LANES, B_NOPE_DIM, B_ROPE_DIM)

    w["c_conv_w"] = jnp.pad(c_conv_w[l], ((0, SUBLANES - 4), (0, 0)))
    w["c_conv_b"] = row(c_conv_b[l])
    w["c_wf"] = jnp.concatenate([_block_diag(c_wr[l, 0]), _block_diag(c_wi[l, 0])], axis=1).astype(BF16)
    w["c_wb"] = jnp.concatenate([_block_diag(c_wr[l, 1]), _block_diag(c_wi[l, 1])], axis=1).astype(BF16)
    w["c_bf"] = row(jnp.concatenate([c_br[l, 0], c_bi[l, 0]]))
    w["c_bb"] = row(jnp.concatenate([c_br[l, 1], c_bi[l, 1]]))
    w["c_lam"] = jnp.pad(c_lam[l], ((0, SUBLANES - 2), (0, 0)))

    w["d_conv_w"] = jnp.pad(d_conv_w[l], ((0, SUBLANES - 4), (0, 0)))
    dconst = jnp.zeros((SUBLANES, LANES), F32)
    dconst = dconst.at[0, 0:2 * D_HEADS].set(d_a_log[l].reshape(-1))
    dconst = dconst.at[1, 0:2 * D_HEADS].set(d_dt_bias[l].reshape(-1))
    w["d_const"] = dconst
    w["d_on_g"] = row(d_on_g[l])

    wbr = w_branch[l]
    wbr_b = jnp.pad(wbr[A_WIDTH:A_WIDTH + B_HEADS * B_V_DIM].reshape(B_HEADS, B_V_DIM, D_MODEL),
                    ((0, 0), (0, LANES - B_V_DIM), (0, 0))).reshape(B_HEADS * LANES, D_MODEL)
    w["w_branch"] = jnp.concatenate([wbr[:A_WIDTH], wbr_b, wbr[A_WIDTH + B_HEADS * B_V_DIM:]], axis=0).astype(BF16)
    w["w_out"] = w_out[l].astype(BF16)
    w["ln2"] = row(ln2_g[l])
    w["w_up"] = w_up[l].astype(BF16)
    w["w_down"] = w_down[l].astype(BF16)
    return w


def _layer(x, w):
    aqs, aks, avs, bq, bk, bv, cx, cg, dqkv, dz, dab, gates = _in_proj(x, w)
    a_outs = [_a_attn(q, k, v, dil) for q, k, v, dil in zip(aqs, aks, avs, A_DILATIONS)]
    ob = _b_flash(bq, bk, bv)
    hf, hb = _rglru(cx, w)
    dq, dk, dv, dgb = _d_prep(dqkv, dab, w)
    df, db = _d_chunk(dq, dk, dv, dgb)
    x = _merge(x, a_outs, ob, hf, hb, cg, df, db, dz, gates, w)
    return _ffn(x, w)


def kernel(x_prompt, x_sample, ln1_g, w_in, a_qn_g, a_kn_g, b_qa_g, b_wuq, b_kva_g, b_wukv, b_qn_g, b_kn_g,
           c_conv_w, c_conv_b, c_wr, c_br, c_wi, c_bi, c_lam, d_conv_w, d_a_log, d_dt_bias, d_on_g,
           w_branch, w_out, ln2_g, w_up, w_down):
    depth = ln1_g.shape[0]
    weights = [_layer_weights(l, ln1_g, w_in, a_qn_g, a_kn_g, b_qa_g, b_wuq, b_kva_g, b_wukv, b_qn_g, b_kn_g,
                              c_conv_w, c_conv_b, c_wr, c_br, c_wi, c_bi, c_lam, d_conv_w, d_a_log, d_dt_bias,
                              d_on_g, w_branch, w_out, ln2_g, w_up, w_down) for l in range(depth)]

    def trunk(x):
        for w in weights:
            x = _layer(x, w)
        return x

    return (trunk(x_prompt), trunk(x_sample))
```

```python
import functools

import jax
import jax.numpy as jnp
import numpy as np
from jax import lax
from jax.experimental import pallas as pl
from jax.experimental.pallas import tpu as pltpu

F32 = jnp.float32
BF16 = jnp.bfloat16

D_MODEL = 1024
ROPE_THETA = 500000.0
NORM_EPS = 1e-6
NEG_INF = -1e30
LOG2_E = 1.4426950408889634

A_HEADS = 8
A_HEAD_DIM = 64
A_ROPE_DIM = 16
A_DILATIONS = (1, 4, 16)
A_HALF = 64
A_WIDTH = A_HEADS * A_HEAD_DIM
A_QBLK = 128
A_SLAB_GROUPS = 4
A_TOKENS = 2048

B_HEADS = 8
B_NOPE_DIM = 64
B_ROPE_DIM = 32
B_QK_DIM = B_NOPE_DIM + B_ROPE_DIM
B_V_DIM = 64
B_Q_RANK = 256
B_KV_RANK = 128

C_WIDTH = 512
C_BLOCKS = 8
C_BLOCK_DIM = 64
C_GATE_C = 8.0

D_HEADS = 4
D_HEAD_DIM = 128
D_WIDTH = 512
D_CHUNK = 64

FF_DIM = 2816
LANES = 128
SUBLANES = 8
VMEM_LIMIT = 56 * 1024 * 1024
IN_TILE = 256

_OFF = np.cumsum([0, 512, 512, 512, 256, 128, 32, 512, 512, 512, 512, 512, 512, 8, 8, 4096])
(O_AQ, O_AK, O_AV, O_BCQ, O_BCKV, O_BKR, O_CX, O_CG, O_DQ, O_DK, O_DV, O_DZ, O_DA, O_DB, O_GATE, O_END) = [int(v) for v in _OFF]


def _const_spec(shape):
    nd = len(shape)
    return pl.BlockSpec(shape, lambda *_: (0,) * nd, pipeline_mode=pl.Buffered(1))


def _params(sem):
    return pltpu.CompilerParams(dimension_semantics=sem, vmem_limit_bytes=VMEM_LIMIT)


def _dot(a, b):
    return jnp.dot(a, b, preferred_element_type=F32)


def _dot_nt(a, b):
    return lax.dot_general(a, b, (((1,), (1,)), ((), ())), preferred_element_type=F32)


def _dot_tn(a, b):
    return lax.dot_general(a, b, (((0,), (0,)), ((), ())), preferred_element_type=F32)


def _rms_rows(x, g):
    return x * lax.rsqrt(jnp.mean(x * x, axis=-1, keepdims=True) + NORM_EPS) * g


def _split2(x):
    hi = x.astype(BF16)
    lo = (x - hi.astype(F32)).astype(BF16)
    return hi, lo


def _split3(x):
    p1 = x.astype(BF16)
    r1 = x - p1.astype(F32)
    p2 = r1.astype(BF16)
    p3 = (r1 - p2.astype(F32)).astype(BF16)
    return p1, p2, p3


def _sigmoid(x):
    return 1.0 / (1.0 + jnp.exp(-x))


def _silu(x):
    return x * _sigmoid(x)


def _softplus(x):
    return jnp.maximum(x, 0.0) + jnp.log(1.0 + jnp.exp(-jnp.abs(x)))


def _rope_tables(pos0, consts, off_ref):
    base = pos0.astype(F32) * consts[0:1, :]
    cb, sb = jnp.cos(base), jnp.sin(base)
    co, so = off_ref[0], off_ref[1]
    c = cb * co - sb * so
    s = sb * co + cb * so
    return c, s * consts[1:2, :], s * consts[2:3, :]


def _in_proj_body(x_ref, ln_ref, wa_ref, wb_ref, wc_ref, wd_ref, wab_ref, wg_ref,
                  gmat_ref, aqg_ref, akg_ref, arope_ref, aoff_ref,
                  qag_ref, wuq_ref, kvag_ref, wukv_ref, bqg_ref, bkg_ref, brope_ref, boff_ref,
                  aq1_ref, aq4_ref, aq16_ref, ak1_ref, ak4_ref, ak16_ref, av1_ref, av4_ref, av16_ref,
                  bq_ref, bk_ref, bv_ref,
                  cx_ref, cg_ref, dqkv_ref, dz_ref, dab_ref, gate_ref, perm_q, perm_k, perm_v, *, tm):
    aq_refs = (aq1_ref, aq4_ref, aq16_ref)
    ak_refs = (ak1_ref, ak4_ref, ak16_ref)
    av_refs = (av1_ref, av4_ref, av16_ref)
    pos0 = pl.program_id(1) * tm
    xn = _rms_rows(x_ref[0], ln_ref[...]).astype(BF16)

    ca, sa1, sa2 = _rope_tables(pos0, arope_ref[...], aoff_ref)
    ca = jnp.concatenate([ca] * 4, axis=1)
    sa1 = jnp.concatenate([sa1] * 4, axis=1)
    sa2 = jnp.concatenate([sa2] * 4, axis=1)
    gmat = gmat_ref[...]

    def a_norm_rope(t, g):
        hi, lo = _split2(t * t)
        ms = (_dot(hi, gmat) + _dot(lo, gmat)) * (1.0 / A_HEAD_DIM)
        y = t * lax.rsqrt(ms + NORM_EPS) * g
        return y * ca + pltpu.roll(y, A_WIDTH - A_ROPE_DIM // 2, 1) * sa1 + pltpu.roll(y, A_ROPE_DIM // 2, 1) * sa2

    def emit(y, refs, perm):
        refs[0][0] = y.astype(BF16)
        nslab = A_WIDTH // LANES
        for c in range(nslab):
            perm[c] = y[:, c * LANES:(c + 1) * LANES]
        for ref, d in zip(refs[1:], A_DILATIONS[1:]):
            for r in range(d):
                for c in range(nslab):
                    col = r * A_WIDTH + c * LANES
                    ref[0, :, col:col + LANES] = perm[c, pl.ds(r, tm // d, stride=d), :].astype(BF16)

    q = _dot(xn, wa_ref[:, 0:A_WIDTH])
    emit(a_norm_rope(q, aqg_ref[...]) * (A_HEAD_DIM ** -0.5 * LOG2_E), aq_refs, perm_q)
    k = _dot(xn, wa_ref[:, A_WIDTH:2 * A_WIDTH])
    emit(a_norm_rope(k, akg_ref[...]), ak_refs, perm_k)
    emit(_dot(xn, wa_ref[:, 2 * A_WIDTH:3 * A_WIDTH]), av_refs, perm_v)

    hb = _dot(xn, wb_ref[...])
    cq = _rms_rows(hb[:, 0:B_Q_RANK], qag_ref[...]).astype(BF16)
    ckv = _rms_rows(hb[:, B_Q_RANK:B_Q_RANK + B_KV_RANK], kvag_ref[...]).astype(BF16)
    kr = hb[:, B_Q_RANK + B_KV_RANK:]
    cb, sb1, sb2 = _rope_tables(pos0, brope_ref[...], boff_ref)
    lane = lax.broadcasted_iota(jnp.int32, (tm, LANES), 1)
    ones_col = jnp.where(lane == B_V_DIM, 1.0, 0.0)

    def b_norm_rope(t, g):
        ms = jnp.sum(t * t, axis=-1, keepdims=True) * (1.0 / B_QK_DIM)
        y = t * lax.rsqrt(ms + NORM_EPS) * g
        return y * cb + pltpu.roll(y, LANES - B_ROPE_DIM // 2, 1) * sb1 + pltpu.roll(y, B_ROPE_DIM // 2, 1) * sb2

    for h in range(B_HEADS):
        sl = slice(h * LANES, (h + 1) * LANES)
        qh = _dot(cq, wuq_ref[:, sl])
        bq_ref[0, h] = (b_norm_rope(qh, bqg_ref[...]) * (B_QK_DIM ** -0.5 * LOG2_E)).astype(BF16)
        kh = _dot(ckv, wukv_ref[:, sl]) + kr
        bk_ref[0, h] = b_norm_rope(kh, bkg_ref[...]).astype(BF16)
        vh = _dot(ckv, wukv_ref[:, B_HEADS * LANES + h * LANES:B_HEADS * LANES + (h + 1) * LANES])
        bv_ref[0, h] = (vh + ones_col).astype(BF16)

    hc = _dot(xn, wc_ref[...])
    cx_ref[0] = hc[:, 0:C_WIDTH]
    cg_ref[0] = hc[:, C_WIDTH:].astype(BF16)
    for j in range(3):
        dqkv_ref[0, :, j * D_WIDTH:(j + 1) * D_WIDTH] = _dot(xn, wd_ref[:, j * D_WIDTH:(j + 1) * D_WIDTH])
    dz_ref[0] = _dot(xn, wd_ref[:, 3 * D_WIDTH:]).astype(BF16)
    dab_ref[0] = _dot(xn, wab_ref[...])
    for j in range(4 * D_MODEL // 512):
        gate_ref[0, :, j * 512:(j + 1) * 512] = _dot(xn, wg_ref[:, j * 512:(j + 1) * 512]).astype(BF16)


def _in_proj(x, w, *, tm=IN_TILE):
    bsz, seq, _ = x.shape
    grid = (bsz, seq // tm)
    tok = lambda width: pl.BlockSpec((1, tm, width), lambda b, i: (b, i, 0))
    head = pl.BlockSpec((1, B_HEADS, tm, LANES), lambda b, i: (b, 0, i, 0))
    consts = [w["ln1"], w["wa"], w["wb"], w["wc"], w["wd"], w["wab"], w["wg"],
              w["gmat"], w["aqg"], w["akg"], w["arope"], w["aoff"],
              w["qag"], w["wuq"], w["kvag"], w["wukv"], w["bqg"], w["bkg"], w["brope"], w["boff"]]
    assert tm == IN_TILE
    tokf = lambda width, dt: jax.ShapeDtypeStruct((bsz, seq, width), dt)
    headf = jax.ShapeDtypeStruct((bsz, B_HEADS, seq, LANES), BF16)
    a_specs = [pl.BlockSpec((1, tm // d, d * A_WIDTH), lambda b, i: (b, i, 0)) for d in A_DILATIONS] * 3
    a_shapes = [jax.ShapeDtypeStruct((bsz, seq // d, d * A_WIDTH), BF16) for d in A_DILATIONS] * 3
    outs = pl.pallas_call(
        functools.partial(_in_proj_body, tm=tm),
        grid=grid,
        in_specs=[tok(D_MODEL)] + [_const_spec(c.shape) for c in consts],
        out_specs=a_specs + [head, head, head,
                             tok(C_WIDTH), tok(C_WIDTH), tok(3 * D_WIDTH), tok(D_WIDTH), tok(LANES), tok(4 * D_MODEL)],
        out_shape=a_shapes + [headf, headf, headf,
                              tokf(C_WIDTH, F32), tokf(C_WIDTH, BF16), tokf(3 * D_WIDTH, F32), tokf(D_WIDTH, BF16),
                              tokf(LANES, F32), tokf(4 * D_MODEL, BF16)],
        scratch_shapes=[pltpu.VMEM((A_WIDTH // LANES, tm, LANES), F32)] * 3,
        compiler_params=_params(("parallel", "parallel")),
        name="in_proj",
    )(x, *consts)
    nd = len(A_DILATIONS)
    return (outs[0:nd], outs[nd:2 * nd], outs[2 * nd:3 * nd]) + tuple(outs[3 * nd:])


def _a_attn_body(q_ref, k_ref, kp_ref, kn_ref, v_ref, vp_ref, vn_ref, o_ref, lse_ref, *, dil, lq, sub_len):
    blk = A_QBLK
    half = A_HALF
    nsub = lq // blk
    npiece = lq // half
    i = pl.program_id(1)
    kw = blk + 2 * half
    rel = lax.broadcasted_iota(jnp.int32, (blk, kw), 1) - half - lax.broadcasted_iota(jnp.int32, (blk, kw), 0)
    band = jnp.abs(rel) <= A_HALF
    kcol = lax.broadcasted_iota(jnp.int32, (blk, kw), 1)
    lane = lax.broadcasted_iota(jnp.int32, (blk, LANES), 1)
    first = lane < A_HEAD_DIM

    def window(main_ref, prev_ref, next_ref, j, cols):
        parts = []
        first_piece = j * (blk // half) - 1
        for t in range(first_piece, first_piece + kw // half):
            if t < 0:
                parts.append(prev_ref[0, :, cols])
            elif t >= npiece:
                parts.append(next_ref[0, :, cols])
            else:
                parts.append(main_ref[0, t * half:(t + 1) * half, cols])
        return jnp.concatenate(parts, axis=0)

    valids = []
    for j in range(nsub):
        kpos = i * lq + j * blk - half + kcol
        valids.append(band & (kpos >= 0) & (kpos < sub_len))

    nslab = A_HEADS // 2
    per_group = nslab // A_SLAB_GROUPS

    def slabs_of(g):
        return range(g * per_group, (g + 1) * per_group)

    def cols_of(r, g):
        return [slice(r * A_WIDTH + hp * LANES, r * A_WIDTH + (hp + 1) * LANES) for hp in slabs_of(g)]

    def stage_scores(j, r, g):
        scores = []
        for cl in cols_of(r, g):
            q2 = q_ref[0, j * blk:(j + 1) * blk, cl]
            k2 = window(k_ref, kp_ref, kn_ref, j, cl)
            zero = jnp.zeros_like(q2)
            scores.append([_dot_nt(qh, k2) for qh in (jnp.where(first, q2, zero), jnp.where(first, zero, q2))])
        return scores

    def stage_softmax(j, scores):
        masked = [[jnp.where(valids[j], s, NEG_INF) for s in pair] for pair in scores]
        maxes = [[jnp.max(s, axis=-1, keepdims=True) for s in pair] for pair in masked]
        probs = [[jnp.exp2(s - m).astype(BF16) for s, m in zip(ps, ms)] for ps, ms in zip(masked, maxes)]
        return probs, maxes

    ones_cols = jnp.ones((kw, LANES), BF16)

    def stage_values(j, r, g, probs):
        outs = []
        for pp, cl in zip(probs, cols_of(r, g)):
            v_ext = jnp.concatenate([window(v_ref, vp_ref, vn_ref, j, cl), ones_cols], axis=1)
            outs.append([_dot(p, v_ext) for p in pp])
        return outs

    def stage_store(j, r, g, outs, maxes):
        rows = pl.ds(j * blk * dil + r, blk, stride=dil) if dil > 1 else pl.ds(j * blk, blk)
        for hp, (oa, ob), (ma, mb) in zip(slabs_of(g), outs, maxes):
            za, zb = oa[:, LANES:], ob[:, LANES:]
            o_ref[0, hp, rows, :] = jnp.where(first, oa[:, :LANES] / za, ob[:, :LANES] / zb)
            lse_ref[0, hp, rows, :] = jnp.where(first, ma + jnp.log(za) * LOG2_E, mb + jnp.log(zb) * LOG2_E)

    problems = [(j, r, g) for j in range(nsub) for r in range(dil) for g in range(A_SLAB_GROUPS)]
    n = len(problems)
    sc, pr, st, ou = {}, {}, {}, {}
    for t in range(n + 3):
        if t < n:
            sc[t] = stage_scores(*problems[t])
        if 0 <= t - 1 < n:
            pr[t - 1], st[t - 1] = stage_softmax(problems[t - 1][0], sc.pop(t - 1))
        if 0 <= t - 2 < n:
            ou[t - 2] = stage_values(*problems[t - 2], pr.pop(t - 2))
        if 0 <= t - 3 < n:
            stage_store(*problems[t - 3], ou.pop(t - 3), st.pop(t - 3))


def _a_attn(qv, kv, vv, dil):
    bsz, sub_len, vwidth = qv.shape
    seq = sub_len * dil
    lq = A_TOKENS // dil
    nblk = sub_len // A_HALF
    per = lq // A_HALF
    main = pl.BlockSpec((1, lq, vwidth), lambda b, i: (b, i, 0))
    prev = pl.BlockSpec((1, A_HALF, vwidth), lambda b, i: (b, jnp.maximum(i * per - 1, 0), 0))
    nxt = pl.BlockSpec((1, A_HALF, vwidth), lambda b, i: (b, jnp.minimum((i + 1) * per, nblk - 1), 0))
    nslab = A_WIDTH // LANES
    out = pl.BlockSpec((1, nslab, A_TOKENS, LANES), lambda b, i: (b, 0, i, 0))
    oshape = jax.ShapeDtypeStruct((bsz, nslab, seq, LANES), F32)
    return pl.pallas_call(
        functools.partial(_a_attn_body, dil=dil, lq=lq, sub_len=sub_len),
        grid=(bsz, seq // A_TOKENS),
        in_specs=[main, main, prev, nxt, main, prev, nxt],
        out_specs=[out, out],
        out_shape=[oshape, oshape],
        compiler_params=_params(("parallel", "parallel")),
        name=f"a_attn_d{dil}",
    )(qv, kv, kv, kv, vv, vv, vv)


def _b_flash_body(q_ref, k_ref, v_ref, o_ref, *, tq, tk, seq, unroll):
    q = q_ref[0, 0]
    nk = seq // tk

    def scores(c):
        return _dot_nt(q, k_ref[0, 0, pl.ds(pl.multiple_of(c * tk, tk), tk), :])

    def update(c, m, acc, s):
        m_new = jnp.maximum(m, jnp.max(s, axis=-1, keepdims=True))
        alpha = jnp.exp2(m - m_new)
        p = jnp.exp2(s - m_new).astype(BF16)
        acc = alpha * acc + _dot(p, v_ref[0, 0, pl.ds(pl.multiple_of(c * tk, tk), tk), :])
        return m_new, acc

    def chunk(c, carry):
        m, acc, s = carry
        s_next = scores(jnp.minimum(c + 1, nk - 1))
        m, acc = update(c, m, acc, s)
        return m, acc, s_next

    m = jnp.full((tq, 1), NEG_INF, F32)
    acc = jnp.zeros((tq, LANES), F32)
    if nk <= unroll:
        s = scores(0)
        for c in range(nk):
            s_next = scores(c + 1) if c + 1 < nk else None
            m, acc = update(c, m, acc, s)
            s = s_next
    else:
        _, acc, _ = lax.fori_loop(0, nk, chunk, (m, acc, scores(0)), unroll=unroll)
    lane = lax.broadcasted_iota(jnp.int32, (tq, LANES), 1)
    denom = jnp.sum(jnp.where(lane == B_V_DIM, acc, 0.0), axis=-1, keepdims=True)
    o_ref[0] = jnp.where(lane < B_V_DIM, acc / denom, 0.0).astype(BF16)


def _b_flash(q, k, v, *, tq=512, tk=512, unroll=32):
    bsz, heads, seq, _ = q.shape
    return pl.pallas_call(
        functools.partial(_b_flash_body, tq=tq, tk=tk, seq=seq, unroll=unroll),
        grid=(bsz, heads, seq // tq),
        in_specs=[pl.BlockSpec((1, 1, tq, LANES), lambda b, h, i: (b, h, i, 0)),
                  pl.BlockSpec((1, 1, seq, LANES), lambda b, h, i: (b, h, 0, 0)),
                  pl.BlockSpec((1, 1, seq, LANES), lambda b, h, i: (b, h, 0, 0))],
        out_specs=pl.BlockSpec((1, tq, LANES), lambda b, h, i: (b, i, h)),
        out_shape=jax.ShapeDtypeStruct((bsz, seq, heads * LANES), BF16),
        compiler_params=_params(("parallel", "parallel", "parallel")),
        name="b_flash",
    )(q, k, v)


def _fill_ext(ext, prev_ref, main_ref, next_ref, is_first, is_last, ts):
    h = SUBLANES
    ext[0:h] = jnp.where(is_first, 0.0, prev_ref[0])
    ext[h:h + ts] = main_ref[0]
    ext[h + ts:] = jnp.where(is_last, 0.0, next_ref[0])


def _conv4(ext, w_ref, ts):
    h = SUBLANES
    out = ext[h - 2:h - 2 + ts] * w_ref[0:1, :]
    for j in range(1, 4):
        out = out + ext[h - 2 + j:h - 2 + j + ts] * w_ref[j:j + 1, :]
    return out


def _halo_specs(ts, width, nt, rev):
    per = ts // SUBLANES
    nblk = nt * per
    t = (lambda i: nt - 1 - i) if rev else (lambda i: i)
    main = pl.BlockSpec((1, ts, width), lambda b, i: (b, t(i), 0))
    prev = pl.BlockSpec((1, SUBLANES, width), lambda b, i: (b, jnp.maximum(t(i) * per - 1, 0), 0))
    nxt = pl.BlockSpec((1, SUBLANES, width), lambda b, i: (b, jnp.minimum((t(i) + 1) * per, nblk - 1), 0))
    return main, prev, nxt


def _scan_tile(a_s, u_s, h_ref, carry_ref, ts, reverse):
    n = ts // SUBLANES
    row = lax.broadcasted_iota(jnp.int32, (SUBLANES, C_WIDTH), 0)

    def group(g, carry):
        gi = (n - 1 - g) if reverse else g
        r0 = pl.multiple_of(gi * SUBLANES, SUBLANES)
        a = a_s[pl.ds(r0, SUBLANES), :]
        b = u_s[pl.ds(r0, SUBLANES), :]
        for sh in (1, 2, 4):
            if reverse:
                keep = row < SUBLANES - sh
                shift = SUBLANES - sh
            else:
                keep = row >= sh
                shift = sh
            a_sh = pltpu.roll(a, shift, 0)
            b_sh = pltpu.roll(b, shift, 0)
            b = jnp.where(keep, a * b_sh + b, b)
            a = jnp.where(keep, a * a_sh, a)
        h = b + a * carry
        h_ref[0, pl.ds(r0, SUBLANES), :] = h
        last = h[0:1, :] if reverse else h[SUBLANES - 1:SUBLANES, :]
        return jnp.broadcast_to(last, (SUBLANES, C_WIDTH))

    carry_ref[...] = lax.fori_loop(0, n, group, carry_ref[...])


def _rglru_body(xf_ref, xfp_ref, xfn_ref, xb_ref, xbp_ref, xbn_ref,
                cw_ref, cb_ref, wf_ref, wb_ref, bf_ref, bb_ref, lam_ref,
                hf_ref, hb_ref, ext, a_s, u_s, carry_f, carry_b, *, ts, nt):
    i = pl.program_id(1)

    @pl.when(i == 0)
    def _():
        carry_f[...] = jnp.zeros_like(carry_f)
        carry_b[...] = jnp.zeros_like(carry_b)

    def direction(d, main, prev, nxt, tile, w_ref, b_ref, h_ref, carry_ref):
        _fill_ext(ext, prev, main, nxt, tile == 0, tile == nt - 1, ts)
        xc = _conv4(ext, cw_ref, ts) + cb_ref[...]
        g = _dot(xc.astype(BF16), w_ref[...]) + b_ref[...]
        r = _sigmoid(g[:, 0:C_WIDTH])
        gi = _sigmoid(g[:, C_WIDTH:])
        log_a = (-C_GATE_C * _softplus(-lam_ref[d:d + 1, :])) * r
        a = jnp.exp(log_a)
        a_s[...] = a
        u_s[...] = jnp.sqrt(1.0 - a * a) * (gi * xc)
        _scan_tile(a_s, u_s, h_ref, carry_ref, ts, reverse=(d == 1))

    direction(0, xf_ref, xfp_ref, xfn_ref, i, wf_ref, bf_ref, hf_ref, carry_f)
    direction(1, xb_ref, xbp_ref, xbn_ref, nt - 1 - i, wb_ref, bb_ref, hb_ref, carry_b)


def _rglru(cx, w, *, ts=512):
    bsz, seq, _ = cx.shape
    ts = min(ts, seq)
    nt = seq // ts
    fm, fp, fn = _halo_specs(ts, C_WIDTH, nt, False)
    bm, bp, bn = _halo_specs(ts, C_WIDTH, nt, True)
    consts = [w["c_conv_w"], w["c_conv_b"], w["c_wf"], w["c_wb"], w["c_bf"], w["c_bb"], w["c_lam"]]
    out = jax.ShapeDtypeStruct((bsz, seq, C_WIDTH), F32)
    return pl.pallas_call(
        functools.partial(_rglru_body, ts=ts, nt=nt),
        grid=(bsz, nt),
        in_specs=[fm, fp, fn, bm, bp, bn] + [_const_spec(c.shape) for c in consts],
        out_specs=[fm, bm],
        out_shape=[out, out],
        scratch_shapes=[pltpu.VMEM((ts + 2 * SUBLANES, C_WIDTH), F32), pltpu.VMEM((ts, C_WIDTH), F32),
                        pltpu.VMEM((ts, C_WIDTH), F32), pltpu.VMEM((SUBLANES, C_WIDTH), F32),
                        pltpu.VMEM((SUBLANES, C_WIDTH), F32)],
        compiler_params=_params(("parallel", "arbitrary")),
        name="rglru",
    )(cx, cx, cx, cx, cx, cx, *consts)


def _d_prep_body(x_ref, xp_ref, xn_ref, ab_ref, cw_ref, dconst_ref, q_ref, k_ref, v_ref, gb_ref, ext, *, ts, nt):
    i = pl.program_id(1)
    _fill_ext(ext, xp_ref, x_ref, xn_ref, i == 0, i == nt - 1, ts)
    y = _silu(_conv4(ext, cw_ref, ts))
    for h in range(D_HEADS):
        for j, ref in ((0, q_ref), (1, k_ref)):
            t = y[:, j * D_WIDTH + h * LANES:j * D_WIDTH + (h + 1) * LANES]
            ref[0, :, h * LANES:(h + 1) * LANES] = t * lax.rsqrt(jnp.sum(t * t, axis=-1, keepdims=True) + NORM_EPS)
    v_ref[0] = y[:, 2 * D_WIDTH:]
    ab = ab_ref[0]
    lane = lax.broadcasted_iota(jnp.int32, ab.shape, 1)
    g = -jnp.exp(dconst_ref[0:1, :]) * _softplus(ab + dconst_ref[1:2, :])
    gb_ref[0] = jnp.where(lane < 2 * D_HEADS, g, _sigmoid(ab))


def _d_prep(dqkv, dab, w, *, ts=256):
    bsz, seq, width = dqkv.shape
    ts = min(ts, seq)
    nt = seq // ts
    main, prev, nxt = _halo_specs(ts, width, nt, False)
    tok = lambda wd: pl.BlockSpec((1, ts, wd), lambda b, i: (b, i, 0))
    o512 = jax.ShapeDtypeStruct((bsz, seq, D_WIDTH), F32)
    return pl.pallas_call(
        functools.partial(_d_prep_body, ts=ts, nt=nt),
        grid=(bsz, nt),
        in_specs=[main, prev, nxt, tok(LANES), _const_spec(w["d_conv_w"].shape), _const_spec(w["d_const"].shape)],
        out_specs=[tok(D_WIDTH), tok(D_WIDTH), tok(D_WIDTH), tok(LANES)],
        out_shape=[o512, o512, o512, jax.ShapeDtypeStruct((bsz, seq, LANES), F32)],
        scratch_shapes=[pltpu.VMEM((ts + 2 * SUBLANES, width), F32)],
        compiler_params=_params(("parallel", "parallel")),
        name="d_prep",
    )(dqkv, dqkv, dqkv, dab, w["d_conv_w"], w["d_const"])


def _d_chunk_body(qf_ref, kf_ref, vf_ref, gf_ref, qb_ref, kb_ref, vb_ref, gbk_ref,
                  of_ref, ob_ref, s_ref, gc_s, u_s, w_s, at_s, qd_s, kd_s, eg_s, *, ts):
    i = pl.program_id(1)

    @pl.when(i == 0)
    def _():
        s_ref[...] = jnp.zeros_like(s_ref)

    c = D_CHUNK
    nc = ts // c
    scale = D_HEAD_DIM ** -0.5
    row_t = lax.broadcasted_iota(jnp.int32, (ts, ts), 0)
    col_t = lax.broadcasted_iota(jnp.int32, (ts, ts), 1)
    same = (row_t // c) == (col_t // c)
    ri = lax.broadcasted_iota(jnp.int32, (c, c), 0)
    ci = lax.broadcasted_iota(jnp.int32, (c, c), 1)
    lane = lax.broadcasted_iota(jnp.int32, (c, LANES), 1)
    ones3 = jnp.where(lane < 3, 1.0, 0.0).astype(BF16)

    dirs = ((0, qf_ref, kf_ref, vf_ref, gf_ref, of_ref), (1, qb_ref, kb_ref, vb_ref, gbk_ref, ob_ref))
    for d, q_ref, k_ref, v_ref, g_ref, o_ref in dirs:
        rev = d == 1
        before = (col_t >= row_t) if rev else (col_t <= row_t)
        tri = jnp.where(same & before, 1.0, 0.0).astype(BF16)
        p1, p2, p3 = _split3(g_ref[0])
        gc_s[d] = _dot(tri, p1) + _dot(tri, p2) + _dot(tri, p3)

    group = 4

    def prepare(gstep, carry):
        chains = []
        for d, q_ref, k_ref, v_ref, g_ref, o_ref in dirs:
            rev = d == 1
            for gi in range(group):
                cc = gstep * group + gi
                r0 = pl.multiple_of(cc * c, c)
                gcb = gc_s[d, pl.ds(r0, c), :]
                gbb = g_ref[0, pl.ds(r0, c), :]
                for h in range(D_HEADS):
                    sl = slice(h * LANES, (h + 1) * LANES)
                    col = d * D_HEADS + h
                    ch = dict(slot=(d * nc + cc) * D_HEADS + h,
                              incl=(ci >= ri) if rev else (ci <= ri), strict=(ci > ri) if rev else (ci < ri))
                    gcol = jnp.sum(jnp.where(lane == col, gcb, 0.0), axis=-1, keepdims=True)
                    ch["beta"] = jnp.sum(jnp.where(lane == 2 * D_HEADS + col, gbb, 0.0), axis=-1, keepdims=True)
                    ch["gcol"] = gcol
                    ch["gtot"] = gcol[0:1, :] if rev else gcol[c - 1:c, :]
                    ch["q"] = q_ref[0, pl.ds(r0, c), sl]
                    ch["k"] = k_ref[0, pl.ds(r0, c), sl]
                    ch["v"] = v_ref[0, pl.ds(r0, c), sl]
                    chains.append(ch)
        for ch in chains:
            g1, g2, g3 = (p.astype(F32) for p in _split3(ch["gcol"]))
            pieces = jnp.where(lane == 0, g1, jnp.where(lane == 1, g2, jnp.where(lane == 2, g3, 0.0)))
            ch["grow"] = _dot_nt(ones3, pieces.astype(BF16))
            ch["kbeta"] = ch["k"] * ch["beta"]
            kb16 = ch["k"].astype(BF16)
            ch["kk"] = _dot_nt(ch["kbeta"].astype(BF16), kb16)
            ch["qk"] = _dot_nt((ch["q"] * scale).astype(BF16), kb16)
        for ch in chains:
            decay = jnp.exp(jnp.where(ch["incl"], ch["gcol"] - ch["grow"], NEG_INF))
            lmat = jnp.where(ch["strict"], ch["kk"] * decay, 0.0)
            ch["attn"] = jnp.where(ch["incl"], ch["qk"] * decay, 0.0).astype(BF16)
            ch["nmat"] = -lmat
            x16 = lmat.astype(BF16)
            ch["xpow"] = _dot(x16, x16)
        for it in range(5):
            for ch in chains:
                x16 = ch["xpow"].astype(BF16)
                ch["nmat"] = ch["nmat"] + ch["xpow"] + _dot(ch["nmat"].astype(BF16), x16)
                if it < 4:
                    ch["xpow"] = _dot(x16, x16)
        for ch in chains:
            n16 = ch["nmat"].astype(BF16)
            vb = ch["v"] * ch["beta"]
            wk = ch["kbeta"] * jnp.exp(ch["gcol"])
            slot = ch["slot"]
            u_s[slot] = vb + _dot(n16, vb.astype(BF16))
            w_s[slot] = (wk + _dot(n16, wk.astype(BF16))).astype(BF16)
            at_s[slot] = ch["attn"]
            qd_s[slot] = (ch["q"] * (scale * jnp.exp(ch["gcol"]))).astype(BF16)
            kd_s[slot] = (ch["k"] * jnp.exp(ch["gtot"] - ch["gcol"])).astype(BF16)
            eg_s[slot] = jnp.broadcast_to(jnp.exp(ch["gtot"]), (1, LANES))
        return carry

    lax.fori_loop(0, nc // group, prepare, 0)

    def advance(step, carry):
        chains = []
        for d, q_ref, k_ref, v_ref, g_ref, o_ref in dirs:
            cc = (nc - 1 - step) if d == 1 else step
            for h in range(D_HEADS):
                chains.append(dict(slot=(d * nc + cc) * D_HEADS + h, col=d * D_HEADS + h, o_ref=o_ref,
                                   r0=pl.multiple_of(cc * c, c), sl=slice(h * LANES, (h + 1) * LANES)))
        for ch in chains:
            ch["state"] = s_ref[ch["col"]]
            ch["s16"] = ch["state"].astype(BF16)
            ch["vn16"] = (u_s[ch["slot"]] - _dot(w_s[ch["slot"]], ch["s16"])).astype(BF16)
        for ch in chains:
            slot = ch["slot"]
            ch["o_ref"][0, pl.ds(ch["r0"], c), ch["sl"]] = _dot(qd_s[slot], ch["s16"]) + _dot(at_s[slot], ch["vn16"])
            s_ref[ch["col"]] = ch["state"] * eg_s[slot] + _dot_tn(kd_s[slot], ch["vn16"])
        return carry

    lax.fori_loop(0, nc, advance, 0)


def _d_chunk(q, k, v, gb, *, ts=256):
    bsz, seq, _ = q.shape
    ts = min(ts, seq)
    nt = seq // ts
    fwd = lambda wd: pl.BlockSpec((1, ts, wd), lambda b, i: (b, i, 0))
    bwd = lambda wd: pl.BlockSpec((1, ts, wd), lambda b, i: (b, nt - 1 - i, 0))
    out = jax.ShapeDtypeStruct((bsz, seq, D_WIDTH), F32)
    nslot = 2 * (ts // D_CHUNK) * D_HEADS
    return pl.pallas_call(
        functools.partial(_d_chunk_body, ts=ts),
        grid=(bsz, nt),
        in_specs=[fwd(D_WIDTH), fwd(D_WIDTH), fwd(D_WIDTH), fwd(LANES),
                  bwd(D_WIDTH), bwd(D_WIDTH), bwd(D_WIDTH), bwd(LANES)],
        out_specs=[fwd(D_WIDTH), bwd(D_WIDTH)],
        out_shape=[out, out],
        scratch_shapes=[pltpu.VMEM((2 * D_HEADS, D_HEAD_DIM, D_HEAD_DIM), F32), pltpu.VMEM((2, ts, LANES), F32),
                        pltpu.VMEM((nslot, D_CHUNK, D_HEAD_DIM), F32), pltpu.VMEM((nslot, D_CHUNK, D_HEAD_DIM), BF16),
                        pltpu.VMEM((nslot, D_CHUNK, D_CHUNK), BF16), pltpu.VMEM((nslot, D_CHUNK, D_HEAD_DIM), BF16),
                        pltpu.VMEM((nslot, D_CHUNK, D_HEAD_DIM), BF16), pltpu.VMEM((nslot, 1, LANES), F32)],
        compiler_params=_params(("parallel", "arbitrary")),
        name="d_chunk",
    )(q, k, v, gb, q, k, v, gb)


def _gelu_tanh(x):
    return 0.5 * x * (1.0 + jnp.tanh(np.sqrt(2.0 / np.pi).astype(np.float32) * (x + 0.044715 * (x * x * x))))


def _merge_body(x_ref, o1_ref, l1_ref, o4_ref, l4_ref, o16_ref, l16_ref, ob_ref,
                hf_ref, hb_ref, cg_ref, df_ref, db_ref, dz_ref, gate_ref,
                wbr_ref, wout_ref, ong_ref, y_ref):
    slabs = []
    for c in range(A_WIDTH // LANES):
        l1, l4, l16 = l1_ref[0, c], l4_ref[0, c], l16_ref[0, c]
        m = jnp.maximum(jnp.maximum(l1, l4), l16)
        e1, e4, e16 = jnp.exp2(l1 - m), jnp.exp2(l4 - m), jnp.exp2(l16 - m)
        slabs.append((e1 * o1_ref[0, c] + e4 * o4_ref[0, c] + e16 * o16_ref[0, c]) / (e1 + e4 + e16))
    oa = jnp.concatenate(slabs, axis=1)
    oc = (hf_ref[0] + hb_ref[0]) * _gelu_tanh(cg_ref[0].astype(F32))
    od_sum = df_ref[0] + db_ref[0]
    z = dz_ref[0].astype(F32)
    heads = []
    for h in range(D_HEADS):
        sl = slice(h * LANES, (h + 1) * LANES)
        heads.append(_rms_rows(od_sum[:, sl], ong_ref[...]) * _silu(z[:, sl]))
    od = jnp.concatenate(heads, axis=1)

    def gate(j):
        return _sigmoid(gate_ref[0, :, j * D_MODEL:(j + 1) * D_MODEL].astype(F32))

    r_a, r_b, r_c = A_WIDTH, A_WIDTH + B_HEADS * LANES, A_WIDTH + B_HEADS * LANES + C_WIDTH
    merged = gate(0) * _dot(oa.astype(BF16), wbr_ref[0:r_a, :])
    merged = merged + gate(1) * _dot(ob_ref[0], wbr_ref[r_a:r_b, :])
    merged = merged + gate(2) * _dot(oc.astype(BF16), wbr_ref[r_b:r_c, :])
    merged = merged + gate(3) * _dot(od.astype(BF16), wbr_ref[r_c:, :])
    y_ref[0] = x_ref[0] + _dot(merged.astype(BF16), wout_ref[...])


def _merge(x, a_outs, ob, hf, hb, cg, df, db, dz, gates, w, *, tm=256):
    bsz, seq, _ = x.shape
    tok = lambda wd: pl.BlockSpec((1, tm, wd), lambda b, i: (b, i, 0))
    consts = [w["w_branch"], w["w_out"], w["d_on_g"]]
    ins = [x]
    specs = [tok(D_MODEL)]
    slab = pl.BlockSpec((1, A_WIDTH // LANES, tm, LANES), lambda b, i: (b, 0, i, 0))
    for o, lse in a_outs:
        ins += [o, lse]
        specs += [slab, slab]
    ins += [ob, hf, hb, cg, df, db, dz, gates]
    specs += [tok(B_HEADS * LANES), tok(C_WIDTH), tok(C_WIDTH), tok(C_WIDTH), tok(D_WIDTH), tok(D_WIDTH),
              tok(D_WIDTH), tok(4 * D_MODEL)]
    return pl.pallas_call(
        _merge_body,
        grid=(bsz, seq // tm),
        in_specs=specs + [_const_spec(c.shape) for c in consts],
        out_specs=tok(D_MODEL),
        out_shape=jax.ShapeDtypeStruct(x.shape, F32),
        compiler_params=_params(("parallel", "parallel")),
        name="merge",
    )(*ins, *consts)


FF_TILE = 256


def _ffn_body(x_ref, ln_ref, wgu_ref, wdn_ref, y_ref):
    x = x_ref[0]
    xn = _rms_rows(x, ln_ref[...]).astype(BF16)
    acc = x
    for j in range(FF_DIM // FF_TILE):
        g = _dot(xn, wgu_ref[:, j * FF_TILE:(j + 1) * FF_TILE])
        u = _dot(xn, wgu_ref[:, FF_DIM + j * FF_TILE:FF_DIM + (j + 1) * FF_TILE])
        acc = acc + _dot((_silu(g) * u).astype(BF16), wdn_ref[j * FF_TILE:(j + 1) * FF_TILE, :])
    y_ref[0] = acc


def _ffn(x, w, *, tm=512):
    bsz, seq, _ = x.shape
    tok = pl.BlockSpec((1, tm, D_MODEL), lambda b, i: (b, i, 0))
    consts = [w["ln2"], w["w_up"], w["w_down"]]
    return pl.pallas_call(
        _ffn_body,
        grid=(bsz, seq // tm),
        in_specs=[tok] + [_const_spec(c.shape) for c in consts],
        out_specs=tok,
        out_shape=jax.ShapeDtypeStruct(x.shape, F32),
        compiler_params=_params(("parallel", "parallel")),
        name="ffn",
    )(x, *consts)


def _block_diag(wblocks):
    g, i, o = wblocks.shape
    eye = jnp.eye(g, dtype=wblocks.dtype)
    return (eye[:, None, :, None] * wblocks[:, :, None, :]).reshape(g * i, g * o)


def _rope_offsets(consts, rows):
    ang = jnp.arange(rows, dtype=F32)[:, None] * consts[0][None, :]
    return jnp.stack([jnp.cos(ang), jnp.sin(ang)], axis=0)


def _rope_consts(head_period, rope_start, rope_dim):
    lane = np.arange(LANES) % head_period
    j = lane - rope_start
    in_rope = (j >= 0) & (j < rope_dim)
    half = rope_dim // 2
    inv_freq = 1.0 / (ROPE_THETA ** (jnp.arange(0, rope_dim, 2, dtype=F32) / rope_dim))
    inv_lane = jnp.where(jnp.asarray(in_rope), inv_freq[np.where(in_rope, j % half, 0)], 0.0)
    out = jnp.zeros((SUBLANES, LANES), F32)
    out = out.at[0].set(inv_lane)
    out = out.at[1].set(jnp.asarray(np.where(in_rope & (j < half), -1.0, 0.0), F32))
    out = out.at[2].set(jnp.asarray(np.where(in_rope & (j >= half), 1.0, 0.0), F32))
    return out


def _pad_heads(wmat, heads, dim):
    r = wmat.shape[0]
    w3 = wmat.reshape(r, heads, dim)
    return jnp.pad(w3, ((0, 0), (0, 0), (0, LANES - dim))).reshape(r, heads * LANES)


def _pad_lane_vec(vec, dim):
    return jnp.pad(vec, (0, LANES - dim)).reshape(1, LANES)


def _layer_weights(l, ln1_g, w_in, a_qn_g, a_kn_g, b_qa_g, b_wuq, b_kva_g, b_wukv, b_qn_g, b_kn_g,
                   c_conv_w, c_conv_b, c_wr, c_br, c_wi, c_bi, c_lam, d_conv_w, d_a_log, d_dt_bias, d_on_g,
                   w_branch, w_out, ln2_g, w_up, w_down):
    wi = w_in[l]
    row = lambda v: v.reshape(1, -1)
    w = {}
    w["ln1"] = row(ln1_g[l])
    w["wa"] = wi[:, O_AQ:O_BCQ].astype(BF16)
    zeros = lambda n: jnp.zeros((D_MODEL, n), F32)
    w["wb"] = jnp.concatenate([wi[:, O_BCQ:O_BKR], zeros(B_NOPE_DIM), wi[:, O_BKR:O_CX],
                               zeros(LANES - B_QK_DIM)], axis=1).astype(BF16)
    w["wc"] = wi[:, O_CX:O_DQ].astype(BF16)
    w["wd"] = wi[:, O_DQ:O_DA].astype(BF16)
    w["wab"] = jnp.concatenate([wi[:, O_DA:O_GATE], zeros(LANES - 4 * D_HEADS)], axis=1).astype(BF16)
    w["wg"] = wi[:, O_GATE:O_END].astype(BF16)
    w["gmat"] = _block_diag(jnp.ones((A_HEADS, A_HEAD_DIM, A_HEAD_DIM), F32)).astype(BF16)
    w["aqg"] = row(jnp.tile(a_qn_g[l], A_HEADS))
    w["akg"] = row(jnp.tile(a_kn_g[l], A_HEADS))
    w["arope"] = _rope_consts(A_HEAD_DIM, 0, A_ROPE_DIM)
    w["aoff"] = _rope_offsets(w["arope"], IN_TILE)
    w["qag"] = row(b_qa_g[l])
    w["wuq"] = _pad_heads(b_wuq[l], B_HEADS, B_QK_DIM).astype(BF16)
    w["kvag"] = row(b_kva_g[l])
    wukv = b_wukv[l].reshape(B_KV_RANK, B_HEADS, B_NOPE_DIM + B_V_DIM)
    w["wukv"] = jnp.concatenate([
        _pad_heads(wukv[:, :, :B_NOPE_DIM].reshape(B_KV_RANK, -1), B_HEADS, B_NOPE_DIM),
        _pad_heads(wukv[:, :, B_NOPE_DIM:].reshape(B_KV_RANK, -1), B_HEADS, B_V_DIM)], axis=1).astype(BF16)
    w["bqg"] = _pad_lane_vec(b_qn_g[l], B_QK_DIM)
    w["bkg"] = _pad_lane_vec(b_kn_g[l], B_QK_DIM)
    w["brope"] = _rope_consts(LANES, B_NOPE_DIM, B_ROPE_DIM)
    w["boff"] = _rope_offsets(w["brope"], IN_TILE)

    w["c_conv_w"] = jnp.pad(c_conv_w[l], ((0, SUBLANES - 4), (0, 0)))
    w["c_conv_b"] = row(c_conv_b[l])
    w["c_wf"] = jnp.concatenate([_block_diag(c_wr[l, 0]), _block_diag(c_wi[l, 0])], axis=1).astype(BF16)
    w["c_wb"] = jnp.concatenate([_block_diag(c_wr[l, 1]), _block_diag(c_wi[l, 1])], axis=1).astype(BF16)
    w["c_bf"] = row(jnp.concatenate([c_br[l, 0], c_bi[l, 0]]))
    w["c_bb"] = row(jnp.concatenate([c_br[l, 1], c_bi[l, 1]]))
    w["c_lam"] = jnp.pad(c_lam[l], ((0, SUBLANES - 2), (0, 0)))

    w["d_conv_w"] = jnp.pad(d_conv_w[l], ((0, SUBLANES - 4), (0, 0)))
    dconst = jnp.zeros((SUBLANES, LANES), F32)
    dconst = dconst.at[0, 0:2 * D_HEADS].set(d_a_log[l].reshape(-1))
    dconst = dconst.at[1, 0:2 * D_HEADS].set(d_dt_bias[l].reshape(-1))
    w["d_const"] = dconst
    w["d_on_g"] = row(d_on_g[l])

    wbr = w_branch[l]
    wbr_b = jnp.pad(wbr[A_WIDTH:A_WIDTH + B_HEADS * B_V_DIM].reshape(B_HEADS, B_V_DIM, D_MODEL),
                    ((0, 0), (0, LANES - B_V_DIM), (0, 0))).reshape(B_HEADS * LANES, D_MODEL)
    w["w_branch"] = jnp.concatenate([wbr[:A_WIDTH], wbr_b, wbr[A_WIDTH + B_HEADS * B_V_DIM:]], axis=0).astype(BF16)
    w["w_out"] = w_out[l].astype(BF16)
    w["ln2"] = row(ln2_g[l])
    w["w_up"] = w_up[l].astype(BF16)
    w["w_down"] = w_down[l].astype(BF16)
    return w


def _layer(x, w):
    aqs, aks, avs, bq, bk, bv, cx, cg, dqkv, dz, dab, gates = _in_proj(x, w)
    a_outs = [_a_attn(q, k, v, dil) for q, k, v, dil in zip(aqs, aks, avs, A_DILATIONS)]
    ob = _b_flash(bq, bk, bv)
    hf, hb = _rglru(cx, w)
    dq, dk, dv, dgb = _d_prep(dqkv, dab, w)
    df, db = _d_chunk(dq, dk, dv, dgb)
    x = _merge(x, a_outs, ob, hf, hb, cg, df, db, dz, gates, w)
    return _ffn(x, w)


def kernel(x_prompt, x_sample, ln1_g, w_in, a_qn_g, a_kn_g, b_qa_g, b_wuq, b_kva_g, b_wukv, b_qn_g, b_kn_g,
           c_conv_w, c_conv_b, c_wr, c_br, c_wi, c_bi, c_lam, d_conv_w, d_a_log, d_dt_bias, d_on_g,
           w_branch, w_out, ln2_g, w_up, w_down):
    depth = ln1_g.shape[0]
    weights = [_layer_weights(l, ln1_g, w_in, a_qn_g, a_kn_g, b_qa_g, b_wuq, b_kva_g, b_wukv, b_qn_g, b_kn_g,
                              c_conv_w, c_conv_b, c_wr, c_br, c_wi, c_bi, c_lam, d_conv_w, d_a_log, d_dt_bias,
                              d_on_g, w_branch, w_out, ln2_g, w_up, w_down) for l in range(depth)]

    def trunk(x):
        for w in weights:
            x = _layer(x, w)
        return x

    return (trunk(x_prompt), trunk(x_sample))
```

```python
import functools

import jax
import jax.numpy as jnp
import numpy as np
from jax import lax
from jax.experimental import pallas as pl
from jax.experimental.pallas import tpu as pltpu

F32 = jnp.float32
BF16 = jnp.bfloat16

D_MODEL = 1024
ROPE_THETA = 500000.0
NORM_EPS = 1e-6
NEG_INF = -1e30
LOG2_E = 1.4426950408889634

A_HEADS = 8
A_HEAD_DIM = 64
A_ROPE_DIM = 16
A_DILATIONS = (1, 4, 16)
A_HALF = 64
A_WIDTH = A_HEADS * A_HEAD_DIM
A_QBLK = 128
A_SLAB_GROUPS = 4
A_TOKENS = 2048

B_HEADS = 8
B_NOPE_DIM = 64
B_ROPE_DIM = 32
B_QK_DIM = B_NOPE_DIM + B_ROPE_DIM
B_V_DIM = 64
B_Q_RANK = 256
B_KV_RANK = 128

C_WIDTH = 512
C_BLOCKS = 8
C_BLOCK_DIM = 64
C_GATE_C = 8.0

D_HEADS = 4
D_HEAD_DIM = 128
D_WIDTH = 512
D_CHUNK = 64

FF_DIM = 2816
LANES = 128
SUBLANES = 8
VMEM_LIMIT = 56 * 1024 * 1024
IN_TILE = 256

_OFF = np.cumsum([0, 512, 512, 512, 256, 128, 32, 512, 512, 512, 512, 512, 512, 8, 8, 4096])
(O_AQ, O_AK, O_AV, O_BCQ, O_BCKV, O_BKR, O_CX, O_CG, O_DQ, O_DK, O_DV, O_DZ, O_DA, O_DB, O_GATE, O_END) = [int(v) for v in _OFF]


def _const_spec(shape):
    nd = len(shape)
    return pl.BlockSpec(shape, lambda *_: (0,) * nd, pipeline_mode=pl.Buffered(1))


def _params(sem):
    return pltpu.CompilerParams(dimension_semantics=sem, vmem_limit_bytes=VMEM_LIMIT)


def _dot(a, b):
    return jnp.dot(a, b, preferred_element_type=F32)


def _dot_nt(a, b):
    return lax.dot_general(a, b, (((1,), (1,)), ((), ())), preferred_element_type=F32)


def _dot_tn(a, b):
    return lax.dot_general(a, b, (((0,), (0,)), ((), ())), preferred_element_type=F32)


def _rms_rows(x, g):
    return x * lax.rsqrt(jnp.mean(x * x, axis=-1, keepdims=True) + NORM_EPS) * g


def _split2(x):
    hi = x.astype(BF16)
    lo = (x - hi.astype(F32)).astype(BF16)
    return hi, lo


def _split3(x):
    p1 = x.astype(BF16)
    r1 = x - p1.astype(F32)
    p2 = r1.astype(BF16)
    p3 = (r1 - p2.astype(F32)).astype(BF16)
    return p1, p2, p3


def _sigmoid(x):
    return 0.5 * jnp.tanh(0.5 * x) + 0.5


def _silu(x):
    return x * _sigmoid(x)


def _softplus(x):
    return jnp.maximum(x, 0.0) + jnp.log(1.0 + jnp.exp(-jnp.abs(x)))


def _rope_tables(pos0, consts, off_ref):
    base = pos0.astype(F32) * consts[0:1, :]
    cb, sb = jnp.cos(base), jnp.sin(base)
    co, so = off_ref[0], off_ref[1]
    c = cb * co - sb * so
    s = sb * co + cb * so
    return c, s * consts[1:2, :], s * consts[2:3, :]


def _in_proj_body(x_ref, ln_ref, wa_ref, wb_ref, wc_ref, wd_ref, wab_ref, wg_ref,
                  gmat_ref, aqg_ref, akg_ref, arope_ref, aoff_ref,
                  qag_ref, wuq_ref, kvag_ref, wukv_ref, bqg_ref, bkg_ref, brope_ref, boff_ref,
                  aq1_ref, aq4_ref, aq16_ref, ak1_ref, ak4_ref, ak16_ref, av1_ref, av4_ref, av16_ref,
                  bq_ref, bk_ref, bv_ref,
                  cx_ref, cg_ref, dqkv_ref, dz_ref, dab_ref, gate_ref, perm_q, perm_k, perm_v, *, tm):
    aq_refs = (aq1_ref, aq4_ref, aq16_ref)
    ak_refs = (ak1_ref, ak4_ref, ak16_ref)
    av_refs = (av1_ref, av4_ref, av16_ref)
    pos0 = pl.program_id(1) * tm
    xn = _rms_rows(x_ref[0], ln_ref[...]).astype(BF16)

    hb = _dot(xn, wb_ref[...])
    cq = _rms_rows(hb[:, 0:B_Q_RANK], qag_ref[...]).astype(BF16)
    ckv = _rms_rows(hb[:, B_Q_RANK:B_Q_RANK + B_KV_RANK], kvag_ref[...]).astype(BF16)
    kr = hb[:, B_Q_RANK + B_KV_RANK:]
    cb, sb1, sb2 = _rope_tables(pos0, brope_ref[...], boff_ref)
    lane = lax.broadcasted_iota(jnp.int32, (tm, LANES), 1)
    ones_col = jnp.where(lane == B_V_DIM, 1.0, 0.0)

    def b_norm_rope(t, g):
        ms = jnp.sum(t * t, axis=-1, keepdims=True) * (1.0 / B_QK_DIM)
        y = t * lax.rsqrt(ms + NORM_EPS) * g
        return y * cb + pltpu.roll(y, LANES - B_ROPE_DIM // 2, 1) * sb1 + pltpu.roll(y, B_ROPE_DIM // 2, 1) * sb2

    for h in range(B_HEADS):
        sl = slice(h * LANES, (h + 1) * LANES)
        qh = _dot(cq, wuq_ref[:, sl])
        bq_ref[0, h] = (b_norm_rope(qh, bqg_ref[...]) * (B_QK_DIM ** -0.5 * LOG2_E)).astype(BF16)
        kh = _dot(ckv, wukv_ref[:, sl]) + kr
        bk_ref[0, h] = b_norm_rope(kh, bkg_ref[...]).astype(BF16)
        vh = _dot(ckv, wukv_ref[:, B_HEADS * LANES + h * LANES:B_HEADS * LANES + (h + 1) * LANES])
        bv_ref[0, h] = (vh + ones_col).astype(BF16)

    ca, sa1, sa2 = _rope_tables(pos0, arope_ref[...], aoff_ref)
    ca = jnp.concatenate([ca] * 4, axis=1)
    sa1 = jnp.concatenate([sa1] * 4, axis=1)
    sa2 = jnp.concatenate([sa2] * 4, axis=1)
    gmat = gmat_ref[...]

    def a_norm_rope(t, g):
        hi, lo = _split2(t * t)
        ms = (_dot(hi, gmat) + _dot(lo, gmat)) * (1.0 / A_HEAD_DIM)
        y = t * lax.rsqrt(ms + NORM_EPS) * g
        return y * ca + pltpu.roll(y, A_WIDTH - A_ROPE_DIM // 2, 1) * sa1 + pltpu.roll(y, A_ROPE_DIM // 2, 1) * sa2

    def emit(y, refs, perm):
        refs[0][0] = y.astype(BF16)
        nslab = A_WIDTH // LANES
        for c in range(nslab):
            perm[c] = y[:, c * LANES:(c + 1) * LANES]
        for ref, d in zip(refs[1:], A_DILATIONS[1:]):
            for r in range(d):
                for c in range(nslab):
                    col = r * A_WIDTH + c * LANES
                    ref[0, :, col:col + LANES] = perm[c, pl.ds(r, tm // d, stride=d), :].astype(BF16)

    q = _dot(xn, wa_ref[:, 0:A_WIDTH])
    emit(a_norm_rope(q, aqg_ref[...]) * (A_HEAD_DIM ** -0.5 * LOG2_E), aq_refs, perm_q)
    k = _dot(xn, wa_ref[:, A_WIDTH:2 * A_WIDTH])
    emit(a_norm_rope(k, akg_ref[...]), ak_refs, perm_k)
    emit(_dot(xn, wa_ref[:, 2 * A_WIDTH:3 * A_WIDTH]), av_refs, perm_v)

    hc = _dot(xn, wc_ref[...])
    cx_ref[0] = hc[:, 0:C_WIDTH]
    cg_ref[0] = hc[:, C_WIDTH:].astype(BF16)
    for j in range(3):
        dqkv_ref[0, :, j * D_WIDTH:(j + 1) * D_WIDTH] = _dot(xn, wd_ref[:, j * D_WIDTH:(j + 1) * D_WIDTH])
    dz_ref[0] = _dot(xn, wd_ref[:, 3 * D_WIDTH:]).astype(BF16)
    dab_ref[0] = _dot(xn, wab_ref[...])
    for j in range(4 * D_MODEL // 512):
        gate_ref[0, :, j * 512:(j + 1) * 512] = _dot(xn, wg_ref[:, j * 512:(j + 1) * 512]).astype(BF16)


def _in_proj(x, w, *, tm=IN_TILE):
    bsz, seq, _ = x.shape
    grid = (bsz, seq // tm)
    tok = lambda width: pl.BlockSpec((1, tm, width), lambda b, i: (b, i, 0))
    head = pl.BlockSpec((1, B_HEADS, tm, LANES), lambda b, i: (b, 0, i, 0))
    consts = [w["ln1"], w["wa"], w["wb"], w["wc"], w["wd"], w["wab"], w["wg"],
              w["gmat"], w["aqg"], w["akg"], w["arope"], w["aoff"],
              w["qag"], w["wuq"], w["kvag"], w["wukv"], w["bqg"], w["bkg"], w["brope"], w["boff"]]
    assert tm == IN_TILE
    tokf = lambda width, dt: jax.ShapeDtypeStruct((bsz, seq, width), dt)
    headf = jax.ShapeDtypeStruct((bsz, B_HEADS, seq, LANES), BF16)
    a_specs = [pl.BlockSpec((1, tm // d, d * A_WIDTH), lambda b, i: (b, i, 0)) for d in A_DILATIONS] * 3
    a_shapes = [jax.ShapeDtypeStruct((bsz, seq // d, d * A_WIDTH), BF16) for d in A_DILATIONS] * 3
    outs = pl.pallas_call(
        functools.partial(_in_proj_body, tm=tm),
        grid=grid,
        in_specs=[tok(D_MODEL)] + [_const_spec(c.shape) for c in consts],
        out_specs=a_specs + [head, head, head,
                             tok(C_WIDTH), tok(C_WIDTH), tok(3 * D_WIDTH), tok(D_WIDTH), tok(LANES), tok(4 * D_MODEL)],
        out_shape=a_shapes + [headf, headf, headf,
                              tokf(C_WIDTH, F32), tokf(C_WIDTH, BF16), tokf(3 * D_WIDTH, F32), tokf(D_WIDTH, BF16),
                              tokf(LANES, F32), tokf(4 * D_MODEL, BF16)],
        scratch_shapes=[pltpu.VMEM((A_WIDTH // LANES, tm, LANES), F32)] * 3,
        compiler_params=_params(("parallel", "parallel")),
        name="in_proj",
    )(x, *consts)
    nd = len(A_DILATIONS)
    return (outs[0:nd], outs[nd:2 * nd], outs[2 * nd:3 * nd]) + tuple(outs[3 * nd:])


def _a_attn_body(q_ref, k_ref, kp_ref, kn_ref, v_ref, vp_ref, vn_ref, o_ref, lse_ref, *, dil, lq, sub_len):
    blk = A_QBLK
    half = A_HALF
    nsub = lq // blk
    npiece = lq // half
    i = pl.program_id(1)
    kw = blk + 2 * half
    rel = lax.broadcasted_iota(jnp.int32, (blk, kw), 1) - half - lax.broadcasted_iota(jnp.int32, (blk, kw), 0)
    band = jnp.abs(rel) <= A_HALF
    kcol = lax.broadcasted_iota(jnp.int32, (blk, kw), 1)
    lane = lax.broadcasted_iota(jnp.int32, (blk, LANES), 1)
    first = lane < A_HEAD_DIM

    def window(main_ref, prev_ref, next_ref, j, cols):
        parts = []
        first_piece = j * (blk // half) - 1
        for t in range(first_piece, first_piece + kw // half):
            if t < 0:
                parts.append(prev_ref[0, :, cols])
            elif t >= npiece:
                parts.append(next_ref[0, :, cols])
            else:
                parts.append(main_ref[0, t * half:(t + 1) * half, cols])
        return jnp.concatenate(parts, axis=0)

    valids = []
    for j in range(nsub):
        kpos = i * lq + j * blk - half + kcol
        valids.append(band & (kpos >= 0) & (kpos < sub_len))

    nslab = A_HEADS // 2
    per_group = nslab // A_SLAB_GROUPS

    def slabs_of(g):
        return range(g * per_group, (g + 1) * per_group)

    def cols_of(r, g):
        return [slice(r * A_WIDTH + hp * LANES, r * A_WIDTH + (hp + 1) * LANES) for hp in slabs_of(g)]

    def stage_scores(j, r, g):
        scores = []
        for cl in cols_of(r, g):
            q2 = q_ref[0, j * blk:(j + 1) * blk, cl]
            k2 = window(k_ref, kp_ref, kn_ref, j, cl)
            zero = jnp.zeros_like(q2)
            scores.append([_dot_nt(qh, k2) for qh in (jnp.where(first, q2, zero), jnp.where(first, zero, q2))])
        return scores

    def stage_softmax(j, scores):
        masked = [[jnp.where(valids[j], s, NEG_INF) for s in pair] for pair in scores]
        maxes = [[jnp.max(s, axis=-1, keepdims=True) for s in pair] for pair in masked]
        probs = [[jnp.exp2(s - m).astype(BF16) for s, m in zip(ps, ms)] for ps, ms in zip(masked, maxes)]
        return probs, maxes

    ones_cols = jnp.ones((kw, LANES), BF16)

    def stage_values(j, r, g, probs):
        outs = []
        for pp, cl in zip(probs, cols_of(r, g)):
            v_ext = jnp.concatenate([window(v_ref, vp_ref, vn_ref, j, cl), ones_cols], axis=1)
            outs.append([_dot(p, v_ext) for p in pp])
        return outs

    def stage_store(j, r, g, outs, maxes):
        rows = pl.ds(j * blk * dil + r, blk, stride=dil) if dil > 1 else pl.ds(j * blk, blk)
        for hp, (oa, ob), (ma, mb) in zip(slabs_of(g), outs, maxes):
            za, zb = oa[:, LANES:], ob[:, LANES:]
            o_ref[0, hp, rows, :] = jnp.where(first, oa[:, :LANES] / za, ob[:, :LANES] / zb)
            lse_ref[0, hp, rows, :] = jnp.where(first, ma + jnp.log(za) * LOG2_E, mb + jnp.log(zb) * LOG2_E)

    problems = [(j, r, g) for j in range(nsub) for r in range(dil) for g in range(A_SLAB_GROUPS)]
    n = len(problems)
    sc, pr, st, ou = {}, {}, {}, {}
    for t in range(n + 3):
        if t < n:
            sc[t] = stage_scores(*problems[t])
        if 0 <= t - 1 < n:
            pr[t - 1], st[t - 1] = stage_softmax(problems[t - 1][0], sc.pop(t - 1))
        if 0 <= t - 2 < n:
            ou[t - 2] = stage_values(*problems[t - 2], pr.pop(t - 2))
        if 0 <= t - 3 < n:
            stage_store(*problems[t - 3], ou.pop(t - 3), st.pop(t - 3))


def _a_attn(qv, kv, vv, dil):
    bsz, sub_len, vwidth = qv.shape
    seq = sub_len * dil
    lq = A_TOKENS // dil
    nblk = sub_len // A_HALF
    per = lq // A_HALF
    main = pl.BlockSpec((1, lq, vwidth), lambda b, i: (b, i, 0))
    prev = pl.BlockSpec((1, A_HALF, vwidth), lambda b, i: (b, jnp.maximum(i * per - 1, 0), 0))
    nxt = pl.BlockSpec((1, A_HALF, vwidth), lambda b, i: (b, jnp.minimum((i + 1) * per, nblk - 1), 0))
    nslab = A_WIDTH // LANES
    out = pl.BlockSpec((1, nslab, A_TOKENS, LANES), lambda b, i: (b, 0, i, 0))
    oshape = jax.ShapeDtypeStruct((bsz, nslab, seq, LANES), F32)
    return pl.pallas_call(
        functools.partial(_a_attn_body, dil=dil, lq=lq, sub_len=sub_len),
        grid=(bsz, seq // A_TOKENS),
        in_specs=[main, main, prev, nxt, main, prev, nxt],
        out_specs=[out, out],
        out_shape=[oshape, oshape],
        compiler_params=_params(("parallel", "parallel")),
        name=f"a_attn_d{dil}",
    )(qv, kv, kv, kv, vv, vv, vv)


def _b_flash_body(q_ref, k_ref, v_ref, o_ref, *, tq, tk, seq, unroll):
    q = q_ref[0, 0]
    nk = seq // tk

    def scores(c):
        return _dot_nt(q, k_ref[0, 0, pl.ds(pl.multiple_of(c * tk, tk), tk), :])

    def update(c, m, acc, s):
        m_new = jnp.maximum(m, jnp.max(s, axis=-1, keepdims=True))
        alpha = jnp.exp2(m - m_new)
        p = jnp.exp2((s - m_new).astype(BF16))
        acc = alpha * acc + _dot(p, v_ref[0, 0, pl.ds(pl.multiple_of(c * tk, tk), tk), :])
        return m_new, acc

    def chunk(c, carry):
        m, acc, s = carry
        s_next = scores(jnp.minimum(c + 1, nk - 1))
        m, acc = update(c, m, acc, s)
        return m, acc, s_next

    m = jnp.full((tq, 1), NEG_INF, F32)
    acc = jnp.zeros((tq, LANES), F32)
    if nk <= unroll:
        s = scores(0)
        for c in range(nk):
            s_next = scores(c + 1) if c + 1 < nk else None
            m, acc = update(c, m, acc, s)
            s = s_next
    else:
        _, acc, _ = lax.fori_loop(0, nk, chunk, (m, acc, scores(0)), unroll=unroll)
    lane = lax.broadcasted_iota(jnp.int32, (tq, LANES), 1)
    denom = jnp.sum(jnp.where(lane == B_V_DIM, acc, 0.0), axis=-1, keepdims=True)
    o_ref[0] = jnp.where(lane < B_V_DIM, acc / denom, 0.0).astype(BF16)


def _b_flash(q, k, v, *, tq=512, tk=512, unroll=32):
    bsz, heads, seq, _ = q.shape
    return pl.pallas_call(
        functools.partial(_b_flash_body, tq=tq, tk=tk, seq=seq, unroll=unroll),
        grid=(bsz, heads, seq // tq),
        in_specs=[pl.BlockSpec((1, 1, tq, LANES), lambda b, h, i: (b, h, i, 0)),
                  pl.BlockSpec((1, 1, seq, LANES), lambda b, h, i: (b, h, 0, 0)),
                  pl.BlockSpec((1, 1, seq, LANES), lambda b, h, i: (b, h, 0, 0))],
        out_specs=pl.BlockSpec((1, tq, LANES), lambda b, h, i: (b, i, h)),
        out_shape=jax.ShapeDtypeStruct((bsz, seq, heads * LANES), BF16),
        compiler_params=_params(("parallel", "parallel", "parallel")),
        name="b_flash",
    )(q, k, v)


def _fill_ext(ext, prev_ref, main_ref, next_ref, is_first, is_last, ts):
    h = SUBLANES
    ext[0:h] = jnp.where(is_first, 0.0, prev_ref[0])
    ext[h:h + ts] = main_ref[0]
    ext[h + ts:] = jnp.where(is_last, 0.0, next_ref[0])


def _conv4(ext, w_ref, ts):
    h = SUBLANES
    out = ext[h - 2:h - 2 + ts] * w_ref[0:1, :]
    for j in range(1, 4):
        out = out + ext[h - 2 + j:h - 2 + j + ts] * w_ref[j:j + 1, :]
    return out


def _halo_specs(ts, width, nt, rev):
    per = ts // SUBLANES
    nblk = nt * per
    t = (lambda i: nt - 1 - i) if rev else (lambda i: i)
    main = pl.BlockSpec((1, ts, width), lambda b, i: (b, t(i), 0))
    prev = pl.BlockSpec((1, SUBLANES, width), lambda b, i: (b, jnp.maximum(t(i) * per - 1, 0), 0))
    nxt = pl.BlockSpec((1, SUBLANES, width), lambda b, i: (b, jnp.minimum((t(i) + 1) * per, nblk - 1), 0))
    return main, prev, nxt


def _scan_tiles(au_s, h_refs, carry_refs, ts):
    n = ts // SUBLANES
    row = lax.broadcasted_iota(jnp.int32, (SUBLANES, C_WIDTH), 0)

    def one(d, g, carry):
        reverse = d == 1
        gi = (n - 1 - g) if reverse else g
        r0 = pl.multiple_of(gi * SUBLANES, SUBLANES)
        a = au_s[2 * d, pl.ds(r0, SUBLANES), :]
        b = au_s[2 * d + 1, pl.ds(r0, SUBLANES), :]
        for sh in (1, 2, 4):
            if reverse:
                keep = row < SUBLANES - sh
                shift = SUBLANES - sh
            else:
                keep = row >= sh
                shift = sh
            a_sh = pltpu.roll(a, shift, 0)
            b_sh = pltpu.roll(b, shift, 0)
            b = jnp.where(keep, a * b_sh + b, b)
            a = jnp.where(keep, a * a_sh, a)
        h = b + a * carry
        h_refs[d][0, pl.ds(r0, SUBLANES), :] = h
        last = h[0:1, :] if reverse else h[SUBLANES - 1:SUBLANES, :]
        return jnp.broadcast_to(last, (SUBLANES, C_WIDTH))

    def group(g, carry):
        return one(0, g, carry[0]), one(1, g, carry[1])

    cf, cb = lax.fori_loop(0, n, group, (carry_refs[0][...], carry_refs[1][...]))
    carry_refs[0][...] = cf
    carry_refs[1][...] = cb


def _rglru_body(xf_ref, xfp_ref, xfn_ref, xb_ref, xbp_ref, xbn_ref,
                cw_ref, cb_ref, wf_ref, wb_ref, bf_ref, bb_ref, lam_ref,
                hf_ref, hb_ref, ext_f, ext_b, au_s, carry_f, carry_b, *, ts, nt):
    i = pl.program_id(1)

    @pl.when(i == 0)
    def _():
        carry_f[...] = jnp.zeros_like(carry_f)
        carry_b[...] = jnp.zeros_like(carry_b)

    def gates(d, ext, main, prev, nxt, tile, w_ref, b_ref):
        _fill_ext(ext, prev, main, nxt, tile == 0, tile == nt - 1, ts)
        xc = _conv4(ext, cw_ref, ts) + cb_ref[...]
        g = _dot(xc.astype(BF16), w_ref[...]) + b_ref[...]
        r = _sigmoid(g[:, 0:C_WIDTH])
        gi = _sigmoid(g[:, C_WIDTH:])
        log_a = (-C_GATE_C * _softplus(-lam_ref[d:d + 1, :])) * r
        a = jnp.exp(log_a)
        au_s[2 * d] = a
        au_s[2 * d + 1] = jnp.sqrt(1.0 - a * a) * (gi * xc)

    gates(0, ext_f, xf_ref, xfp_ref, xfn_ref, i, wf_ref, bf_ref)
    gates(1, ext_b, xb_ref, xbp_ref, xbn_ref, nt - 1 - i, wb_ref, bb_ref)
    _scan_tiles(au_s, (hf_ref, hb_ref), (carry_f, carry_b), ts)


def _rglru(cx, w, *, ts=512):
    bsz, seq, _ = cx.shape
    ts = min(ts, seq)
    nt = seq // ts
    fm, fp, fn = _halo_specs(ts, C_WIDTH, nt, False)
    bm, bp, bn = _halo_specs(ts, C_WIDTH, nt, True)
    consts = [w["c_conv_w"], w["c_conv_b"], w["c_wf"], w["c_wb"], w["c_bf"], w["c_bb"], w["c_lam"]]
    out = jax.ShapeDtypeStruct((bsz, seq, C_WIDTH), F32)
    return pl.pallas_call(
        functools.partial(_rglru_body, ts=ts, nt=nt),
        grid=(bsz, nt),
        in_specs=[fm, fp, fn, bm, bp, bn] + [_const_spec(c.shape) for c in consts],
        out_specs=[fm, bm],
        out_shape=[out, out],
        scratch_shapes=[pltpu.VMEM((ts + 2 * SUBLANES, C_WIDTH), F32), pltpu.VMEM((ts + 2 * SUBLANES, C_WIDTH), F32),
                        pltpu.VMEM((4, ts, C_WIDTH), F32), pltpu.VMEM((SUBLANES, C_WIDTH), F32),
                        pltpu.VMEM((SUBLANES, C_WIDTH), F32)],
        compiler_params=_params(("parallel", "arbitrary")),
        name="rglru",
    )(cx, cx, cx, cx, cx, cx, *consts)


def _d_prep_body(x_ref, xp_ref, xn_ref, ab_ref, cw_ref, dconst_ref, q_ref, k_ref, v_ref, gb_ref, ext, *, ts, nt):
    i = pl.program_id(1)
    _fill_ext(ext, xp_ref, x_ref, xn_ref, i == 0, i == nt - 1, ts)
    y = _silu(_conv4(ext, cw_ref, ts))
    for h in range(D_HEADS):
        for j, ref in ((0, q_ref), (1, k_ref)):
            t = y[:, j * D_WIDTH + h * LANES:j * D_WIDTH + (h + 1) * LANES]
            ref[0, :, h * LANES:(h + 1) * LANES] = t * lax.rsqrt(jnp.sum(t * t, axis=-1, keepdims=True) + NORM_EPS)
    v_ref[0] = y[:, 2 * D_WIDTH:]
    ab = ab_ref[0]
    lane = lax.broadcasted_iota(jnp.int32, ab.shape, 1)
    g = -jnp.exp(dconst_ref[0:1, :]) * _softplus(ab + dconst_ref[1:2, :])
    gb_ref[0] = jnp.where(lane < 2 * D_HEADS, g, _sigmoid(ab))


def _d_prep(dqkv, dab, w, *, ts=256):
    bsz, seq, width = dqkv.shape
    ts = min(ts, seq)
    nt = seq // ts
    main, prev, nxt = _halo_specs(ts, width, nt, False)
    tok = lambda wd: pl.BlockSpec((1, ts, wd), lambda b, i: (b, i, 0))
    o512 = jax.ShapeDtypeStruct((bsz, seq, D_WIDTH), F32)
    return pl.pallas_call(
        functools.partial(_d_prep_body, ts=ts, nt=nt),
        grid=(bsz, nt),
        in_specs=[main, prev, nxt, tok(LANES), _const_spec(w["d_conv_w"].shape), _const_spec(w["d_const"].shape)],
        out_specs=[tok(D_WIDTH), tok(D_WIDTH), tok(D_WIDTH), tok(LANES)],
        out_shape=[o512, o512, o512, jax.ShapeDtypeStruct((bsz, seq, LANES), F32)],
        scratch_shapes=[pltpu.VMEM((ts + 2 * SUBLANES, width), F32)],
        compiler_params=_params(("parallel", "parallel")),
        name="d_prep",
    )(dqkv, dqkv, dqkv, dab, w["d_conv_w"], w["d_const"])


def _d_chunk_body(qf_ref, kf_ref, vf_ref, gf_ref, qb_ref, kb_ref, vb_ref, gbk_ref,
                  of_ref, ob_ref, s_ref, gc_s, u_s, wq_s, ak_s, eg_s, *, ts):
    i = pl.program_id(1)

    @pl.when(i == 0)
    def _():
        s_ref[...] = jnp.zeros_like(s_ref)

    c = D_CHUNK
    nc = ts // c
    scale = D_HEAD_DIM ** -0.5
    row_t = lax.broadcasted_iota(jnp.int32, (ts, ts), 0)
    col_t = lax.broadcasted_iota(jnp.int32, (ts, ts), 1)
    same = (row_t // c) == (col_t // c)
    ri = lax.broadcasted_iota(jnp.int32, (c, c), 0)
    ci = lax.broadcasted_iota(jnp.int32, (c, c), 1)
    lane = lax.broadcasted_iota(jnp.int32, (c, LANES), 1)
    ones3 = jnp.where(lane < 3, 1.0, 0.0).astype(BF16)

    dirs = ((0, qf_ref, kf_ref, vf_ref, gf_ref, of_ref), (1, qb_ref, kb_ref, vb_ref, gbk_ref, ob_ref))
    for d, q_ref, k_ref, v_ref, g_ref, o_ref in dirs:
        rev = d == 1
        before = (col_t >= row_t) if rev else (col_t <= row_t)
        tri = jnp.where(same & before, 1.0, 0.0).astype(BF16)
        p1, p2, p3 = _split3(g_ref[0])
        gc_s[d] = _dot(tri, p1) + _dot(tri, p2) + _dot(tri, p3)

    group = 4

    def prepare(gstep, carry):
        chains = []
        for d, q_ref, k_ref, v_ref, g_ref, o_ref in dirs:
            rev = d == 1
            for gi in range(group):
                cc = gstep * group + gi
                r0 = pl.multiple_of(cc * c, c)
                gcb = gc_s[d, pl.ds(r0, c), :]
                gbb = g_ref[0, pl.ds(r0, c), :]
                for h in range(D_HEADS):
                    sl = slice(h * LANES, (h + 1) * LANES)
                    col = d * D_HEADS + h
                    ch = dict(slot=(d * nc + cc) * D_HEADS + h,
                              incl=(ci >= ri) if rev else (ci <= ri), strict=(ci > ri) if rev else (ci < ri))
                    gcol = jnp.sum(jnp.where(lane == col, gcb, 0.0), axis=-1, keepdims=True)
                    ch["beta"] = jnp.sum(jnp.where(lane == 2 * D_HEADS + col, gbb, 0.0), axis=-1, keepdims=True)
                    ch["gcol"] = gcol
                    ch["gtot"] = gcol[0:1, :] if rev else gcol[c - 1:c, :]
                    ch["q"] = q_ref[0, pl.ds(r0, c), sl]
                    ch["k"] = k_ref[0, pl.ds(r0, c), sl]
                    ch["v"] = v_ref[0, pl.ds(r0, c), sl]
                    chains.append(ch)
        for ch in chains:
            g1, g2, g3 = (p.astype(F32) for p in _split3(ch["gcol"]))
            pieces = jnp.where(lane == 0, g1, jnp.where(lane == 1, g2, jnp.where(lane == 2, g3, 0.0)))
            ch["grow"] = _dot_nt(ones3, pieces.astype(BF16))
            ch["kbeta"] = ch["k"] * ch["beta"]
            kb16 = ch["k"].astype(BF16)
            ch["kk"] = _dot_nt(ch["kbeta"].astype(BF16), kb16)
            ch["qk"] = _dot_nt((ch["q"] * scale).astype(BF16), kb16)
        for ch in chains:
            decay = jnp.exp(jnp.where(ch["incl"], ch["gcol"] - ch["grow"], NEG_INF))
            lmat = jnp.where(ch["strict"], ch["kk"] * decay, 0.0)
            ch["attn"] = jnp.where(ch["incl"], ch["qk"] * decay, 0.0).astype(BF16)
            ch["nmat"] = -lmat
            x16 = lmat.astype(BF16)
            ch["xpow"] = _dot(x16, x16)
        for it in range(5):
            for ch in chains:
                x16 = ch["xpow"].astype(BF16)
                ch["nmat"] = ch["nmat"] + ch["xpow"] + _dot(ch["nmat"].astype(BF16), x16)
                if it < 4:
                    ch["xpow"] = _dot(x16, x16)
        for ch in chains:
            n16 = ch["nmat"].astype(BF16)
            vb = ch["v"] * ch["beta"]
            wk = ch["kbeta"] * jnp.exp(ch["gcol"])
            slot = ch["slot"]
            u_s[slot] = vb + _dot(n16, vb.astype(BF16))
            wq_s[slot, 0:c] = (wk + _dot(n16, wk.astype(BF16))).astype(BF16)
            wq_s[slot, c:] = (ch["q"] * (scale * jnp.exp(ch["gcol"]))).astype(BF16)
            ak_s[slot, 0:c] = ch["attn"]
            ak_s[slot, c:] = (ch["k"] * jnp.exp(ch["gtot"] - ch["gcol"])).T.astype(BF16)
            eg_s[slot] = jnp.broadcast_to(jnp.exp(ch["gtot"]), (1, LANES))
        return carry

    lax.fori_loop(0, nc // group, prepare, 0)

    def advance(step, carry):
        chains = []
        for d, q_ref, k_ref, v_ref, g_ref, o_ref in dirs:
            cc = (nc - 1 - step) if d == 1 else step
            for h in range(D_HEADS):
                chains.append(dict(slot=(d * nc + cc) * D_HEADS + h, col=d * D_HEADS + h, o_ref=o_ref,
                                   r0=pl.multiple_of(cc * c, c), sl=slice(h * LANES, (h + 1) * LANES)))
        for ch in chains:
            ch["state"] = s_ref[ch["col"]]
            ch["ws_qs"] = _dot(wq_s[ch["slot"]], ch["state"].astype(BF16))
        for ch in chains:
            vn16 = (u_s[ch["slot"]] - ch["ws_qs"][0:c]).astype(BF16)
            ch["av_kv"] = _dot(ak_s[ch["slot"]], vn16)
        for ch in chains:
            ch["o_ref"][0, pl.ds(ch["r0"], c), ch["sl"]] = ch["ws_qs"][c:] + ch["av_kv"][0:c]
            s_ref[ch["col"]] = ch["state"] * eg_s[ch["slot"]] + ch["av_kv"][c:]
        return carry

    lax.fori_loop(0, nc, advance, 0)


def _d_chunk(q, k, v, gb, *, ts=256):
    bsz, seq, _ = q.shape
    ts = min(ts, seq)
    nt = seq // ts
    fwd = lambda wd: pl.BlockSpec((1, ts, wd), lambda b, i: (b, i, 0))
    bwd = lambda wd: pl.BlockSpec((1, ts, wd), lambda b, i: (b, nt - 1 - i, 0))
    out = jax.ShapeDtypeStruct((bsz, seq, D_WIDTH), F32)
    nslot = 2 * (ts // D_CHUNK) * D_HEADS
    return pl.pallas_call(
        functools.partial(_d_chunk_body, ts=ts),
        grid=(bsz, nt),
        in_specs=[fwd(D_WIDTH), fwd(D_WIDTH), fwd(D_WIDTH), fwd(LANES),
                  bwd(D_WIDTH), bwd(D_WIDTH), bwd(D_WIDTH), bwd(LANES)],
        out_specs=[fwd(D_WIDTH), bwd(D_WIDTH)],
        out_shape=[out, out],
        scratch_shapes=[pltpu.VMEM((2 * D_HEADS, D_HEAD_DIM, D_HEAD_DIM), F32), pltpu.VMEM((2, ts, LANES), F32),
                        pltpu.VMEM((nslot, D_CHUNK, D_HEAD_DIM), F32),
                        pltpu.VMEM((nslot, 2 * D_CHUNK, D_HEAD_DIM), BF16),
                        pltpu.VMEM((nslot, D_CHUNK + D_HEAD_DIM, D_CHUNK), BF16),
                        pltpu.VMEM((nslot, 1, LANES), F32)],
        compiler_params=_params(("parallel", "arbitrary")),
        name="d_chunk",
    )(q, k, v, gb, q, k, v, gb)


def _gelu_tanh(x):
    return 0.5 * x * (1.0 + jnp.tanh(np.sqrt(2.0 / np.pi).astype(np.float32) * (x + 0.044715 * (x * x * x))))


def _merge_body(x_ref, o1_ref, l1_ref, o4_ref, l4_ref, o16_ref, l16_ref, ob_ref,
                hf_ref, hb_ref, cg_ref, df_ref, db_ref, dz_ref, gate_ref,
                wbr_ref, wout_ref, ong_ref, y_ref):
    slabs = []
    for c in range(A_WIDTH // LANES):
        l1, l4, l16 = l1_ref[0, c], l4_ref[0, c], l16_ref[0, c]
        m = jnp.maximum(jnp.maximum(l1, l4), l16)
        e1, e4, e16 = jnp.exp2(l1 - m), jnp.exp2(l4 - m), jnp.exp2(l16 - m)
        slabs.append((e1 * o1_ref[0, c] + e4 * o4_ref[0, c] + e16 * o16_ref[0, c]) / (e1 + e4 + e16))
    oa = jnp.concatenate(slabs, axis=1)
    oc = (hf_ref[0] + hb_ref[0]) * _gelu_tanh(cg_ref[0].astype(F32))
    od_sum = df_ref[0] + db_ref[0]
    z = dz_ref[0].astype(F32)
    heads = []
    for h in range(D_HEADS):
        sl = slice(h * LANES, (h + 1) * LANES)
        heads.append(_rms_rows(od_sum[:, sl], ong_ref[...]) * _silu(z[:, sl]))
    od = jnp.concatenate(heads, axis=1)

    def gate(j):
        return _sigmoid(gate_ref[0, :, j * D_MODEL:(j + 1) * D_MODEL]).astype(F32)

    r_a, r_b, r_c = A_WIDTH, A_WIDTH + B_HEADS * LANES, A_WIDTH + B_HEADS * LANES + C_WIDTH
    merged = gate(0) * _dot(oa.astype(BF16), wbr_ref[0:r_a, :])
    merged = merged + gate(1) * _dot(ob_ref[0], wbr_ref[r_a:r_b, :])
    merged = merged + gate(2) * _dot(oc.astype(BF16), wbr_ref[r_b:r_c, :])
    merged = merged + gate(3) * _dot(od.astype(BF16), wbr_ref[r_c:, :])
    y_ref[0] = x_ref[0] + _dot(merged.astype(BF16), wout_ref[...])


def _merge(x, a_outs, ob, hf, hb, cg, df, db, dz, gates, w, *, tm=256):
    bsz, seq, _ = x.shape
    tok = lambda wd: pl.BlockSpec((1, tm, wd), lambda b, i: (b, i, 0))
    consts = [w["w_branch"], w["w_out"], w["d_on_g"]]
    ins = [x]
    specs = [tok(D_MODEL)]
    slab = pl.BlockSpec((1, A_WIDTH // LANES, tm, LANES), lambda b, i: (b, 0, i, 0))
    for o, lse in a_outs:
        ins += [o, lse]
        specs += [slab, slab]
    ins += [ob, hf, hb, cg, df, db, dz, gates]
    specs += [tok(B_HEADS * LANES), tok(C_WIDTH), tok(C_WIDTH), tok(C_WIDTH), tok(D_WIDTH), tok(D_WIDTH),
              tok(D_WIDTH), tok(4 * D_MODEL)]
    return pl.pallas_call(
        _merge_body,
        grid=(bsz, seq // tm),
        in_specs=specs + [_const_spec(c.shape) for c in consts],
        out_specs=tok(D_MODEL),
        out_shape=jax.ShapeDtypeStruct(x.shape, F32),
        compiler_params=_params(("parallel", "parallel")),
        name="merge",
    )(*ins, *consts)


FF_TILE = 256


def _ffn_body(x_ref, ln_ref, wgu_ref, wdn_ref, y_ref):
    x = x_ref[0]
    xn = _rms_rows(x, ln_ref[...]).astype(BF16)
    acc = x
    for j in range(FF_DIM // FF_TILE):
        g = _dot(xn, wgu_ref[:, j * FF_TILE:(j + 1) * FF_TILE])
        u = _dot(xn, wgu_ref[:, FF_DIM + j * FF_TILE:FF_DIM + (j + 1) * FF_TILE])
        acc = acc + _dot((_silu(g) * u).astype(BF16), wdn_ref[j * FF_TILE:(j + 1) * FF_TILE, :])
    y_ref[0] = acc


def _ffn(x, w, *, tm=512):
    bsz, seq, _ = x.shape
    tok = pl.BlockSpec((1, tm, D_MODEL), lambda b, i: (b, i, 0))
    consts = [w["ln2"], w["w_up"], w["w_down"]]
    return pl.pallas_call(
        _ffn_body,
        grid=(bsz, seq // tm),
        in_specs=[tok] + [_const_spec(c.shape) for c in consts],
        out_specs=tok,
        out_shape=jax.ShapeDtypeStruct(x.shape, F32),
        compiler_params=_params(("parallel", "parallel")),
        name="ffn",
    )(x, *consts)


def _block_diag(wblocks):
    g, i, o = wblocks.shape
    eye = jnp.eye(g, dtype=wblocks.dtype)
    return (eye[:, None, :, None] * wblocks[:, :, None, :]).reshape(g * i, g * o)


def _rope_offsets(consts, rows):
    ang = jnp.arange(rows, dtype=F32)[:, None] * consts[0][None, :]
    return jnp.stack([jnp.cos(ang), jnp.sin(ang)], axis=0)


def _rope_consts(head_period, rope_start, rope_dim):
    lane = np.arange(LANES) % head_period
    j = lane - rope_start
    in_rope = (j >= 0) & (j < rope_dim)
    half = rope_dim // 2
    inv_freq = 1.0 / (ROPE_THETA ** (jnp.arange(0, rope_dim, 2, dtype=F32) / rope_dim))
    inv_lane = jnp.where(jnp.asarray(in_rope), inv_freq[np.where(in_rope, j % half, 0)], 0.0)
    out = jnp.zeros((SUBLANES, LANES), F32)
    out = out.at[0].set(inv_lane)
    out = out.at[1].set(jnp.asarray(np.where(in_rope & (j < half), -1.0, 0.0), F32))
    out = out.at[2].set(jnp.asarray(np.where(in_rope & (j >= half), 1.0, 0.0), F32))
    return out


def _pad_heads(wmat, heads, dim):
    r = wmat.shape[0]
    w3 = wmat.reshape(r, heads, dim)
    return jnp.pad(w3, ((0, 0), (0, 0), (0, LANES - dim))).reshape(r, heads * LANES)


def _pad_lane_vec(vec, dim):
    return jnp.pad(vec, (0, LANES - dim)).reshape(1, LANES)


def _layer_weights(l, ln1_g, w_in, a_qn_g, a_kn_g, b_qa_g, b_wuq, b_kva_g, b_wukv, b_qn_g, b_kn_g,
                   c_conv_w, c_conv_b, c_wr, c_br, c_wi, c_bi, c_lam, d_conv_w, d_a_log, d_dt_bias, d_on_g,
                   w_branch, w_out, ln2_g, w_up, w_down):
    wi = w_in[l]
    row = lambda v: v.reshape(1, -1)
    w = {}
    w["ln1"] = row(ln1_g[l])
    w["wa"] = wi[:, O_AQ:O_BCQ].astype(BF16)
    zeros = lambda n: jnp.zeros((D_MODEL, n), F32)
    w["wb"] = jnp.concatenate([wi[:, O_BCQ:O_BKR], zeros(B_NOPE_DIM), wi[:, O_BKR:O_CX],
                               zeros(LANES - B_QK_DIM)], axis=1).astype(BF16)
    w["wc"] = wi[:, O_CX:O_DQ].astype(BF16)
    w["wd"] = wi[:, O_DQ:O_DA].astype(BF16)
    w["wab"] = jnp.concatenate([wi[:, O_DA:O_GATE], zeros(LANES - 4 * D_HEADS)], axis=1).astype(BF16)
    w["wg"] = wi[:, O_GATE:O_END].astype(BF16)
    w["gmat"] = _block_diag(jnp.ones((A_HEADS, A_HEAD_DIM, A_HEAD_DIM), F32)).astype(BF16)
    w["aqg"] = row(jnp.tile(a_qn_g[l], A_HEADS))
    w["akg"] = row(jnp.tile(a_kn_g[l], A_HEADS))
    w["arope"] = _rope_consts(A_HEAD_DIM, 0, A_ROPE_DIM)
    w["aoff"] = _rope_offsets(w["arope"], IN_TILE)
    w["qag"] = row(b_qa_g[l])
    w["wuq"] = _pad_heads(b_wuq[l], B_HEADS, B_QK_DIM).astype(BF16)
    w["kvag"] = row(b_kva_g[l])
    wukv = b_wukv[l].reshape(B_KV_RANK, B_HEADS, B_NOPE_DIM + B_V_DIM)
    w["wukv"] = jnp.concatenate([
        _pad_heads(wukv[:, :, :B_NOPE_DIM].reshape(B_KV_RANK, -1), B_HEADS, B_NOPE_DIM),
        _pad_heads(wukv[:, :, B_NOPE_DIM:].reshape(B_KV_RANK, -1), B_HEADS, B_V_DIM)], axis=1).astype(BF16)
    w["bqg"] = _pad_lane_vec(b_qn_g[l], B_QK_DIM)
    w["bkg"] = _pad_lane_vec(b_kn_g[l], B_QK_DIM)
    w["brope"] = _rope_consts(LANES, B_NOPE_DIM, B_ROPE_DIM)
    w["boff"] = _rope_offsets(w["brope"], IN_TILE)

    w["c_conv_w"] = jnp.pad(c_conv_w[l], ((0, SUBLANES - 4), (0, 0)))
    w["c_conv_b"] = row(c_conv_b[l])
    w["c_wf"] = jnp.concatenate([_block_diag(c_wr[l, 0]), _block_diag(c_wi[l, 0])], axis=1).astype(BF16)
    w["c_wb"] = jnp.concatenate([_block_diag(c_wr[l, 1]), _block_diag(c_wi[l, 1])], axis=1).astype(BF16)
    w["c_bf"] = row(jnp.concatenate([c_br[l, 0], c_bi[l, 0]]))
    w["c_bb"] = row(jnp.concatenate([c_br[l, 1], c_bi[l, 1]]))
    w["c_lam"] = jnp.pad(c_lam[l], ((0, SUBLANES - 2), (0, 0)))

    w["d_conv_w"] = jnp.pad(d_conv_w[l], ((0, SUBLANES - 4), (0, 0)))
    dconst = jnp.zeros((SUBLANES, LANES), F32)
    dconst = dconst.at[0, 0:2 * D_HEADS].set(d_a_log[l].reshape(-1))
    dconst = dconst.at[1, 0:2 * D_HEADS].set(d_dt_bias[l].reshape(-1))
    w["d_const"] = dconst
    w["d_on_g"] = row(d_on_g[l])

    wbr = w_branch[l]
    wbr_b = jnp.pad(wbr[A_WIDTH:A_WIDTH + B_HEADS * B_V_DIM].reshape(B_HEADS, B_V_DIM, D_MODEL),
                    ((0, 0), (0, LANES - B_V_DIM), (0, 0))).reshape(B_HEADS * LANES, D_MODEL)
    w["w_branch"] = jnp.concatenate([wbr[:A_WIDTH], wbr_b, wbr[A_WIDTH + B_HEADS * B_V_DIM:]], axis=0).astype(BF16)
    w["w_out"] = w_out[l].astype(BF16)
    w["ln2"] = row(ln2_g[l])
    w["w_up"] = w_up[l].astype(BF16)
    w["w_down"] = w_down[l].astype(BF16)
    return w


def _layer(x, w):
    aqs, aks, avs, bq, bk, bv, cx, cg, dqkv, dz, dab, gates = _in_proj(x, w)
    a_outs = [_a_attn(q, k, v, dil) for q, k, v, dil in zip(aqs, aks, avs, A_DILATIONS)]
    ob = _b_flash(bq, bk, bv)
    hf, hb = _rglru(cx, w)
    dq, dk, dv, dgb = _d_prep(dqkv, dab, w)
    df, db = _d_chunk(dq, dk, dv, dgb)
    x = _merge(x, a_outs, ob, hf, hb, cg, df, db, dz, gates, w)
    return _ffn(x, w)


def kernel(x_prompt, x_sample, ln1_g, w_in, a_qn_g, a_kn_g, b_qa_g, b_wuq, b_kva_g, b_wukv, b_qn_g, b_kn_g,
           c_conv_w, c_conv_b, c_wr, c_br, c_wi, c_bi, c_lam, d_conv_w, d_a_log, d_dt_bias, d_on_g,
           w_branch, w_out, ln2_g, w_up, w_down):
    depth = ln1_g.shape[0]
    weights = [_layer_weights(l, ln1_g, w_in, a_qn_g, a_kn_g, b_qa_g, b_wuq, b_kva_g, b_wukv, b_qn_g, b_kn_g,
                              c_conv_w, c_conv_b, c_wr, c_br, c_wi, c_bi, c_lam, d_conv_w, d_a_log, d_dt_bias,
                              d_on_g, w_branch, w_out, ln2_g, w_up, w_down) for l in range(depth)]

    def trunk(x):
        for w in weights:
            x = _layer(x, w)
        return x

    return (trunk(x_prompt), trunk(x_sample))
```

```python
import functools

import jax
import jax.numpy as jnp
import numpy as np
from jax import lax
from jax.experimental import pallas as pl
from jax.experimental.pallas import tpu as pltpu

F32 = jnp.float32
BF16 = jnp.bfloat16

D_MODEL = 1024
ROPE_THETA = 500000.0
NORM_EPS = 1e-6
NEG_INF = -1e30
LOG2_E = 1.4426950408889634

A_HEADS = 8
A_HEAD_DIM = 64
A_ROPE_DIM = 16
A_DILATIONS = (1, 4, 16)
A_HALF = 64
A_WIDTH = A_HEADS * A_HEAD_DIM
A_QBLK = 128
A_SLAB_GROUPS = 4
A_TOKENS = 2048

B_HEADS = 8
B_NOPE_DIM = 64
B_ROPE_DIM = 32
B_QK_DIM = B_NOPE_DIM + B_ROPE_DIM
B_V_DIM = 64
B_Q_RANK = 256
B_KV_RANK = 128

C_WIDTH = 512
C_BLOCKS = 8
C_BLOCK_DIM = 64
C_GATE_C = 8.0

D_HEADS = 4
D_HEAD_DIM = 128
D_WIDTH = 512
D_CHUNK = 64

FF_DIM = 2816
LANES = 128
SUBLANES = 8
VMEM_LIMIT = 56 * 1024 * 1024
IN_TILE = 256

_OFF = np.cumsum([0, 512, 512, 512, 256, 128, 32, 512, 512, 512, 512, 512, 512, 8, 8, 4096])
(O_AQ, O_AK, O_AV, O_BCQ, O_BCKV, O_BKR, O_CX, O_CG, O_DQ, O_DK, O_DV, O_DZ, O_DA, O_DB, O_GATE, O_END) = [int(v) for v in _OFF]


def _const_spec(shape):
    nd = len(shape)
    return pl.BlockSpec(shape, lambda *_: (0,) * nd, pipeline_mode=pl.Buffered(1))


def _params(sem):
    return pltpu.CompilerParams(dimension_semantics=sem, vmem_limit_bytes=VMEM_LIMIT)


def _dot(a, b):
    return jnp.dot(a, b, preferred_element_type=F32)


def _dot_nt(a, b):
    return lax.dot_general(a, b, (((1,), (1,)), ((), ())), preferred_element_type=F32)


def _dot_tn(a, b):
    return lax.dot_general(a, b, (((0,), (0,)), ((), ())), preferred_element_type=F32)


def _rms_rows(x, g):
    return x * lax.rsqrt(jnp.mean(x * x, axis=-1, keepdims=True) + NORM_EPS) * g


def _split2(x):
    hi = x.astype(BF16)
    lo = (x - hi.astype(F32)).astype(BF16)
    return hi, lo


def _split3(x):
    p1 = x.astype(BF16)
    r1 = x - p1.astype(F32)
    p2 = r1.astype(BF16)
    p3 = (r1 - p2.astype(F32)).astype(BF16)
    return p1, p2, p3


def _sigmoid(x):
    return 0.5 * jnp.tanh(0.5 * x) + 0.5


def _silu(x):
    return x * _sigmoid(x)


def _softplus(x):
    return jnp.maximum(x, 0.0) + jnp.log(1.0 + jnp.exp(-jnp.abs(x)))


def _rope_tables(pos0, consts, off_ref):
    base = pos0.astype(F32) * consts[0:1, :]
    cb, sb = jnp.cos(base), jnp.sin(base)
    co, so = off_ref[0], off_ref[1]
    c = cb * co - sb * so
    s = sb * co + cb * so
    return c, s * consts[1:2, :], s * consts[2:3, :]


def _in_proj_body(x_ref, ln_ref, wa_ref, wb_ref, wc_ref, wd_ref, wab_ref, wg_ref,
                  gmat_ref, aqg_ref, akg_ref, arope_ref, aoff_ref,
                  qag_ref, wuq_ref, kvag_ref, wukv_ref, bqg_ref, bkg_ref, brope_ref, boff_ref,
                  aq1_ref, aq4_ref, aq16_ref, ak1_ref, ak4_ref, ak16_ref, av1_ref, av4_ref, av16_ref,
                  bq_ref, bk_ref, bv_ref,
                  cx_ref, cg_ref, dqkv_ref, dz_ref, dab_ref, gate_ref, perm_q, perm_k, perm_v, *, tm):
    aq_refs = (aq1_ref, aq4_ref, aq16_ref)
    ak_refs = (ak1_ref, ak4_ref, ak16_ref)
    av_refs = (av1_ref, av4_ref, av16_ref)
    pos0 = pl.program_id(1) * tm
    xn = _rms_rows(x_ref[0], ln_ref[...]).astype(BF16)

    def xdot(wmat):
        return _dot(xn, wmat)

    hb = xdot(wb_ref[...])
    cq = _rms_rows(hb[:, 0:B_Q_RANK], qag_ref[...]).astype(BF16)
    ckv = _rms_rows(hb[:, B_Q_RANK:B_Q_RANK + B_KV_RANK], kvag_ref[...]).astype(BF16)
    kr = hb[:, B_Q_RANK + B_KV_RANK:]
    cb, sb1, sb2 = _rope_tables(pos0, brope_ref[...], boff_ref)
    lane = lax.broadcasted_iota(jnp.int32, (tm, LANES), 1)
    ones_col = jnp.where(lane == B_V_DIM, 1.0, 0.0)

    def b_norm_rope(t, g):
        ms = jnp.sum(t * t, axis=-1, keepdims=True) * (1.0 / B_QK_DIM)
        y = t * lax.rsqrt(ms + NORM_EPS) * g
        return y * cb + pltpu.roll(y, LANES - B_ROPE_DIM // 2, 1) * sb1 + pltpu.roll(y, B_ROPE_DIM // 2, 1) * sb2

    for h in range(B_HEADS):
        sl = slice(h * LANES, (h + 1) * LANES)
        qh = _dot(cq, wuq_ref[:, sl])
        bq_ref[0, h] = (b_norm_rope(qh, bqg_ref[...]) * (B_QK_DIM ** -0.5 * LOG2_E)).astype(BF16)
        kh = _dot(ckv, wukv_ref[:, sl]) + kr
        bk_ref[0, h] = b_norm_rope(kh, bkg_ref[...]).astype(BF16)
        vh = _dot(ckv, wukv_ref[:, B_HEADS * LANES + h * LANES:B_HEADS * LANES + (h + 1) * LANES])
        bv_ref[0, h] = (vh + ones_col).astype(BF16)

    ca, sa1, sa2 = _rope_tables(pos0, arope_ref[...], aoff_ref)
    ca = jnp.concatenate([ca] * 4, axis=1)
    sa1 = jnp.concatenate([sa1] * 4, axis=1)
    sa2 = jnp.concatenate([sa2] * 4, axis=1)
    gmat = gmat_ref[...]

    def a_norm_rope(t, g):
        hi, lo = _split2(t * t)
        ms = (_dot(hi, gmat) + _dot(lo, gmat)) * (1.0 / A_HEAD_DIM)
        y = t * lax.rsqrt(ms + NORM_EPS) * g
        return y * ca + pltpu.roll(y, A_WIDTH - A_ROPE_DIM // 2, 1) * sa1 + pltpu.roll(y, A_ROPE_DIM // 2, 1) * sa2

    def emit(y, refs, perm):
        refs[0][0] = y.astype(BF16)
        nslab = A_WIDTH // LANES
        for c in range(nslab):
            perm[c] = y[:, c * LANES:(c + 1) * LANES]
        for ref, d in zip(refs[1:], A_DILATIONS[1:]):
            for r in range(d):
                for c in range(nslab):
                    col = r * A_WIDTH + c * LANES
                    ref[0, :, col:col + LANES] = perm[c, pl.ds(r, tm // d, stride=d), :].astype(BF16)

    q = xdot(wa_ref[:, 0:A_WIDTH])
    emit(a_norm_rope(q, aqg_ref[...]) * (A_HEAD_DIM ** -0.5 * LOG2_E), aq_refs, perm_q)
    k = xdot(wa_ref[:, A_WIDTH:2 * A_WIDTH])
    emit(a_norm_rope(k, akg_ref[...]), ak_refs, perm_k)
    emit(xdot(wa_ref[:, 2 * A_WIDTH:3 * A_WIDTH]), av_refs, perm_v)

    hc = xdot(wc_ref[...])
    cx_ref[0] = hc[:, 0:C_WIDTH]
    cg_ref[0] = hc[:, C_WIDTH:].astype(BF16)
    for j in range(3):
        dqkv_ref[0, :, j * D_WIDTH:(j + 1) * D_WIDTH] = xdot(wd_ref[:, j * D_WIDTH:(j + 1) * D_WIDTH])
    dz_ref[0] = xdot(wd_ref[:, 3 * D_WIDTH:]).astype(BF16)
    dab_ref[0] = xdot(wab_ref[...])
    for j in range(4 * D_MODEL // 512):
        gate_ref[0, :, j * 512:(j + 1) * 512] = xdot(wg_ref[:, j * 512:(j + 1) * 512]).astype(BF16)


def _in_proj(x, w, *, tm=IN_TILE):
    bsz, seq, _ = x.shape
    grid = (bsz, seq // tm)
    tok = lambda width: pl.BlockSpec((1, tm, width), lambda b, i: (b, i, 0))
    head = pl.BlockSpec((1, B_HEADS, tm, LANES), lambda b, i: (b, 0, i, 0))
    consts = [w["ln1"], w["wa"], w["wb"], w["wc"], w["wd"], w["wab"], w["wg"],
              w["gmat"], w["aqg"], w["akg"], w["arope"], w["aoff"],
              w["qag"], w["wuq"], w["kvag"], w["wukv"], w["bqg"], w["bkg"], w["brope"], w["boff"]]
    assert tm == IN_TILE
    tokf = lambda width, dt: jax.ShapeDtypeStruct((bsz, seq, width), dt)
    headf = jax.ShapeDtypeStruct((bsz, B_HEADS, seq, LANES), BF16)
    a_specs = [pl.BlockSpec((1, tm // d, d * A_WIDTH), lambda b, i: (b, i, 0)) for d in A_DILATIONS] * 3
    a_shapes = [jax.ShapeDtypeStruct((bsz, seq // d, d * A_WIDTH), BF16) for d in A_DILATIONS] * 3
    outs = pl.pallas_call(
        functools.partial(_in_proj_body, tm=tm),
        grid=grid,
        in_specs=[tok(D_MODEL)] + [_const_spec(c.shape) for c in consts],
        out_specs=a_specs + [head, head, head,
                             tok(C_WIDTH), tok(C_WIDTH), tok(3 * D_WIDTH), tok(D_WIDTH), tok(LANES), tok(4 * D_MODEL)],
        out_shape=a_shapes + [headf, headf, headf,
                              tokf(C_WIDTH, F32), tokf(C_WIDTH, BF16), tokf(3 * D_WIDTH, F32), tokf(D_WIDTH, BF16),
                              tokf(LANES, F32), tokf(4 * D_MODEL, BF16)],
        scratch_shapes=[pltpu.VMEM((A_WIDTH // LANES, tm, LANES), F32)] * 3,
        compiler_params=_params(("parallel", "parallel")),
        name="in_proj",
    )(x, *consts)
    nd = len(A_DILATIONS)
    return (outs[0:nd], outs[nd:2 * nd], outs[2 * nd:3 * nd]) + tuple(outs[3 * nd:])


def _a_attn_body(q_ref, k_ref, kp_ref, kn_ref, v_ref, vp_ref, vn_ref, o_ref, lse_ref, *, dil, lq, sub_len):
    blk = A_QBLK
    half = A_HALF
    nsub = lq // blk
    npiece = lq // half
    i = pl.program_id(1)
    kw = blk + 2 * half
    rel = lax.broadcasted_iota(jnp.int32, (blk, kw), 1) - half - lax.broadcasted_iota(jnp.int32, (blk, kw), 0)
    band = jnp.abs(rel) <= A_HALF
    kcol = lax.broadcasted_iota(jnp.int32, (blk, kw), 1)
    lane = lax.broadcasted_iota(jnp.int32, (blk, LANES), 1)
    first = lane < A_HEAD_DIM

    def window(main_ref, prev_ref, next_ref, j, cols):
        parts = []
        first_piece = j * (blk // half) - 1
        for t in range(first_piece, first_piece + kw // half):
            if t < 0:
                parts.append(prev_ref[0, :, cols])
            elif t >= npiece:
                parts.append(next_ref[0, :, cols])
            else:
                parts.append(main_ref[0, t * half:(t + 1) * half, cols])
        return jnp.concatenate(parts, axis=0)

    valids = []
    for j in range(nsub):
        kpos = i * lq + j * blk - half + kcol
        valids.append(band & (kpos >= 0) & (kpos < sub_len))

    nslab = A_HEADS // 2
    per_group = nslab // A_SLAB_GROUPS

    def slabs_of(g):
        return range(g * per_group, (g + 1) * per_group)

    def cols_of(r, g):
        return [slice(r * A_WIDTH + hp * LANES, r * A_WIDTH + (hp + 1) * LANES) for hp in slabs_of(g)]

    def stage_scores(j, r, g):
        scores = []
        for cl in cols_of(r, g):
            q2 = q_ref[0, j * blk:(j + 1) * blk, cl]
            k2 = window(k_ref, kp_ref, kn_ref, j, cl)
            zero = jnp.zeros_like(q2)
            scores.append([_dot_nt(qh, k2) for qh in (jnp.where(first, q2, zero), jnp.where(first, zero, q2))])
        return scores

    def stage_softmax(j, scores):
        masked = [[jnp.where(valids[j], s, NEG_INF) for s in pair] for pair in scores]
        maxes = [[jnp.max(s, axis=-1, keepdims=True) for s in pair] for pair in masked]
        probs = [[jnp.exp2(s - m).astype(BF16) for s, m in zip(ps, ms)] for ps, ms in zip(masked, maxes)]
        return probs, maxes

    ones_cols = jnp.ones((kw, LANES), BF16)

    def stage_values(j, r, g, probs):
        outs = []
        for pp, cl in zip(probs, cols_of(r, g)):
            v_ext = jnp.concatenate([window(v_ref, vp_ref, vn_ref, j, cl), ones_cols], axis=1)
            outs.append([_dot(p, v_ext) for p in pp])
        return outs

    def stage_store(j, r, g, outs, maxes):
        rows = pl.ds(j * blk * dil + r, blk, stride=dil) if dil > 1 else pl.ds(j * blk, blk)
        for hp, (oa, ob), (ma, mb) in zip(slabs_of(g), outs, maxes):
            za, zb = oa[:, LANES:], ob[:, LANES:]
            o_ref[0, hp, rows, :] = jnp.where(first, oa[:, :LANES] / za, ob[:, :LANES] / zb)
            lse_ref[0, hp, rows, :] = jnp.where(first, ma + jnp.log(za) * LOG2_E, mb + jnp.log(zb) * LOG2_E)

    problems = [(j, r, g) for j in range(nsub) for r in range(dil) for g in range(A_SLAB_GROUPS)]
    n = len(problems)
    sc, pr, st, ou = {}, {}, {}, {}
    for t in range(n + 3):
        if t < n:
            sc[t] = stage_scores(*problems[t])
        if 0 <= t - 1 < n:
            pr[t - 1], st[t - 1] = stage_softmax(problems[t - 1][0], sc.pop(t - 1))
        if 0 <= t - 2 < n:
            ou[t - 2] = stage_values(*problems[t - 2], pr.pop(t - 2))
        if 0 <= t - 3 < n:
            stage_store(*problems[t - 3], ou.pop(t - 3), st.pop(t - 3))


def _a_attn(qv, kv, vv, dil):
    bsz, sub_len, vwidth = qv.shape
    seq = sub_len * dil
    lq = A_TOKENS // dil
    nblk = sub_len // A_HALF
    per = lq // A_HALF
    main = pl.BlockSpec((1, lq, vwidth), lambda b, i: (b, i, 0))
    prev = pl.BlockSpec((1, A_HALF, vwidth), lambda b, i: (b, jnp.maximum(i * per - 1, 0), 0))
    nxt = pl.BlockSpec((1, A_HALF, vwidth), lambda b, i: (b, jnp.minimum((i + 1) * per, nblk - 1), 0))
    nslab = A_WIDTH // LANES
    out = pl.BlockSpec((1, nslab, A_TOKENS, LANES), lambda b, i: (b, 0, i, 0))
    oshape = jax.ShapeDtypeStruct((bsz, nslab, seq, LANES), F32)
    return pl.pallas_call(
        functools.partial(_a_attn_body, dil=dil, lq=lq, sub_len=sub_len),
        grid=(bsz, seq // A_TOKENS),
        in_specs=[main, main, prev, nxt, main, prev, nxt],
        out_specs=[out, out],
        out_shape=[oshape, oshape],
        compiler_params=_params(("parallel", "parallel")),
        name=f"a_attn_d{dil}",
    )(qv, kv, kv, kv, vv, vv, vv)


def _b_flash_body(q_ref, k_ref, v_ref, o_ref, *, tq, tk, seq, unroll, nq):
    nk = seq // tk
    lane = lax.broadcasted_iota(jnp.int32, (tq, LANES), 1)

    for t in range(nq):
        q = q_ref[0, 0, t * tq:(t + 1) * tq, :]

        def scores(c, q=q):
            return _dot_nt(q, k_ref[0, 0, pl.ds(pl.multiple_of(c * tk, tk), tk), :])

        def update(c, m, acc, s):
            m_new = jnp.maximum(m, jnp.max(s, axis=-1, keepdims=True))
            alpha = jnp.exp2(m - m_new)
            p = jnp.exp2((s - m_new).astype(BF16))
            acc = alpha * acc + _dot(p, v_ref[0, 0, pl.ds(pl.multiple_of(c * tk, tk), tk), :])
            return m_new, acc

        def chunk(c, carry, scores=scores):
            m, acc, s = carry
            s_next = scores(jnp.minimum(c + 1, nk - 1))
            m, acc = update(c, m, acc, s)
            return m, acc, s_next

        m = jnp.full((tq, 1), NEG_INF, F32)
        acc = jnp.zeros((tq, LANES), F32)
        if nk <= unroll:
            s = scores(0)
            for c in range(nk):
                s_next = scores(c + 1) if c + 1 < nk else None
                m, acc = update(c, m, acc, s)
                s = s_next
        else:
            _, acc, _ = lax.fori_loop(0, nk, chunk, (m, acc, scores(0)), unroll=unroll)
        denom = jnp.sum(jnp.where(lane == B_V_DIM, acc, 0.0), axis=-1, keepdims=True)
        o_ref[0, t * tq:(t + 1) * tq, :] = jnp.where(lane < B_V_DIM, acc / denom, 0.0).astype(BF16)


def _b_flash(q, k, v, *, tq=512, tk=512, unroll=32, nq=2):
    bsz, heads, seq, _ = q.shape
    tstep = tq * nq
    return pl.pallas_call(
        functools.partial(_b_flash_body, tq=tq, tk=tk, seq=seq, unroll=unroll, nq=nq),
        grid=(bsz, heads, seq // tstep),
        in_specs=[pl.BlockSpec((1, 1, tstep, LANES), lambda b, h, i: (b, h, i, 0)),
                  pl.BlockSpec((1, 1, seq, LANES), lambda b, h, i: (b, h, 0, 0)),
                  pl.BlockSpec((1, 1, seq, LANES), lambda b, h, i: (b, h, 0, 0))],
        out_specs=pl.BlockSpec((1, tstep, LANES), lambda b, h, i: (b, i, h)),
        out_shape=jax.ShapeDtypeStruct((bsz, seq, heads * LANES), BF16),
        compiler_params=_params(("parallel", "parallel", "parallel")),
        name="b_flash",
    )(q, k, v)


def _fill_ext(ext, prev_ref, main_ref, next_ref, is_first, is_last, ts):
    h = SUBLANES
    ext[0:h] = jnp.where(is_first, 0.0, prev_ref[0])
    ext[h:h + ts] = main_ref[0]
    ext[h + ts:] = jnp.where(is_last, 0.0, next_ref[0])


def _conv4(ext, w_ref, ts):
    h = SUBLANES
    out = ext[h - 2:h - 2 + ts] * w_ref[0:1, :]
    for j in range(1, 4):
        out = out + ext[h - 2 + j:h - 2 + j + ts] * w_ref[j:j + 1, :]
    return out


def _halo_specs(ts, width, nt, rev):
    per = ts // SUBLANES
    nblk = nt * per
    t = (lambda i: nt - 1 - i) if rev else (lambda i: i)
    main = pl.BlockSpec((1, ts, width), lambda b, i: (b, t(i), 0))
    prev = pl.BlockSpec((1, SUBLANES, width), lambda b, i: (b, jnp.maximum(t(i) * per - 1, 0), 0))
    nxt = pl.BlockSpec((1, SUBLANES, width), lambda b, i: (b, jnp.minimum((t(i) + 1) * per, nblk - 1), 0))
    return main, prev, nxt


def _scan_tiles(au_s, h_refs, carry_refs, ts):
    n = ts // SUBLANES
    row = lax.broadcasted_iota(jnp.int32, (SUBLANES, C_WIDTH), 0)

    def one(d, g, carry):
        reverse = d == 1
        gi = (n - 1 - g) if reverse else g
        r0 = pl.multiple_of(gi * SUBLANES, SUBLANES)
        a = au_s[2 * d, pl.ds(r0, SUBLANES), :]
        b = au_s[2 * d + 1, pl.ds(r0, SUBLANES), :]
        for sh in (1, 2, 4):
            if reverse:
                keep = row < SUBLANES - sh
                shift = SUBLANES - sh
            else:
                keep = row >= sh
                shift = sh
            a_sh = pltpu.roll(a, shift, 0)
            b_sh = pltpu.roll(b, shift, 0)
            b = jnp.where(keep, a * b_sh + b, b)
            a = jnp.where(keep, a * a_sh, a)
        h = b + a * carry
        h_refs[d][0, pl.ds(r0, SUBLANES), :] = h
        last = h[0:1, :] if reverse else h[SUBLANES - 1:SUBLANES, :]
        return jnp.broadcast_to(last, (SUBLANES, C_WIDTH))

    def group(g, carry):
        return one(0, g, carry[0]), one(1, g, carry[1])

    cf, cb = lax.fori_loop(0, n, group, (carry_refs[0][...], carry_refs[1][...]))
    carry_refs[0][...] = cf
    carry_refs[1][...] = cb


def _rglru_body(xf_ref, xfp_ref, xfn_ref, xb_ref, xbp_ref, xbn_ref,
                cw_ref, cb_ref, wf_ref, wb_ref, bf_ref, bb_ref, lam_ref,
                hf_ref, hb_ref, ext_f, ext_b, au_s, carry_f, carry_b, *, ts, nt):
    i = pl.program_id(1)

    @pl.when(i == 0)
    def _():
        carry_f[...] = jnp.zeros_like(carry_f)
        carry_b[...] = jnp.zeros_like(carry_b)

    def gates(d, ext, main, prev, nxt, tile, w_ref, b_ref):
        _fill_ext(ext, prev, main, nxt, tile == 0, tile == nt - 1, ts)
        xc = _conv4(ext, cw_ref, ts) + cb_ref[...]
        g = _dot(xc.astype(BF16), w_ref[...]) + b_ref[...]
        r = _sigmoid(g[:, 0:C_WIDTH])
        gi = _sigmoid(g[:, C_WIDTH:])
        log_a = (-C_GATE_C * _softplus(-lam_ref[d:d + 1, :])) * r
        a = jnp.exp(log_a)
        au_s[2 * d] = a
        au_s[2 * d + 1] = jnp.sqrt(1.0 - a * a) * (gi * xc)

    gates(0, ext_f, xf_ref, xfp_ref, xfn_ref, i, wf_ref, bf_ref)
    gates(1, ext_b, xb_ref, xbp_ref, xbn_ref, nt - 1 - i, wb_ref, bb_ref)
    _scan_tiles(au_s, (hf_ref, hb_ref), (carry_f, carry_b), ts)


def _rglru(cx, w, *, ts=512):
    bsz, seq, _ = cx.shape
    ts = min(ts, seq)
    nt = seq // ts
    fm, fp, fn = _halo_specs(ts, C_WIDTH, nt, False)
    bm, bp, bn = _halo_specs(ts, C_WIDTH, nt, True)
    consts = [w["c_conv_w"], w["c_conv_b"], w["c_wf"], w["c_wb"], w["c_bf"], w["c_bb"], w["c_lam"]]
    out = jax.ShapeDtypeStruct((bsz, seq, C_WIDTH), F32)
    return pl.pallas_call(
        functools.partial(_rglru_body, ts=ts, nt=nt),
        grid=(bsz, nt),
        in_specs=[fm, fp, fn, bm, bp, bn] + [_const_spec(c.shape) for c in consts],
        out_specs=[fm, bm],
        out_shape=[out, out],
        scratch_shapes=[pltpu.VMEM((ts + 2 * SUBLANES, C_WIDTH), F32), pltpu.VMEM((ts + 2 * SUBLANES, C_WIDTH), F32),
                        pltpu.VMEM((4, ts, C_WIDTH), F32), pltpu.VMEM((SUBLANES, C_WIDTH), F32),
                        pltpu.VMEM((SUBLANES, C_WIDTH), F32)],
        compiler_params=_params(("parallel", "arbitrary")),
        name="rglru",
    )(cx, cx, cx, cx, cx, cx, *consts)


def _d_prep_body(x_ref, xp_ref, xn_ref, ab_ref, cw_ref, dconst_ref, q_ref, k_ref, v_ref, gb_ref, ext, *, ts, nt):
    i = pl.program_id(1)
    _fill_ext(ext, xp_ref, x_ref, xn_ref, i == 0, i == nt - 1, ts)
    y = _silu(_conv4(ext, cw_ref, ts))
    for h in range(D_HEADS):
        for j, ref in ((0, q_ref), (1, k_ref)):
            t = y[:, j * D_WIDTH + h * LANES:j * D_WIDTH + (h + 1) * LANES]
            ref[0, :, h * LANES:(h + 1) * LANES] = t * lax.rsqrt(jnp.sum(t * t, axis=-1, keepdims=True) + NORM_EPS)
    v_ref[0] = y[:, 2 * D_WIDTH:]
    ab = ab_ref[0]
    lane = lax.broadcasted_iota(jnp.int32, ab.shape, 1)
    g = -jnp.exp(dconst_ref[0:1, :]) * _softplus(ab + dconst_ref[1:2, :])
    gb_ref[0] = jnp.where(lane < 2 * D_HEADS, g, _sigmoid(ab))


def _d_prep(dqkv, dab, w, *, ts=256):
    bsz, seq, width = dqkv.shape
    ts = min(ts, seq)
    nt = seq // ts
    main, prev, nxt = _halo_specs(ts, width, nt, False)
    tok = lambda wd: pl.BlockSpec((1, ts, wd), lambda b, i: (b, i, 0))
    o512 = jax.ShapeDtypeStruct((bsz, seq, D_WIDTH), F32)
    return pl.pallas_call(
        functools.partial(_d_prep_body, ts=ts, nt=nt),
        grid=(bsz, nt),
        in_specs=[main, prev, nxt, tok(LANES), _const_spec(w["d_conv_w"].shape), _const_spec(w["d_const"].shape)],
        out_specs=[tok(D_WIDTH), tok(D_WIDTH), tok(D_WIDTH), tok(LANES)],
        out_shape=[o512, o512, o512, jax.ShapeDtypeStruct((bsz, seq, LANES), F32)],
        scratch_shapes=[pltpu.VMEM((ts + 2 * SUBLANES, width), F32)],
        compiler_params=_params(("parallel", "parallel")),
        name="d_prep",
    )(dqkv, dqkv, dqkv, dab, w["d_conv_w"], w["d_const"])


def _d_chunk_body(qf_ref, kf_ref, vf_ref, gf_ref, qb_ref, kb_ref, vb_ref, gbk_ref,
                  of_ref, ob_ref, s_ref, gc_s, u_s, wq_s, ak_s, eg_s, *, ts):
    i = pl.program_id(1)

    @pl.when(i == 0)
    def _():
        s_ref[...] = jnp.zeros_like(s_ref)

    c = D_CHUNK
    nc = ts // c
    scale = D_HEAD_DIM ** -0.5
    row_t = lax.broadcasted_iota(jnp.int32, (ts, ts), 0)
    col_t = lax.broadcasted_iota(jnp.int32, (ts, ts), 1)
    same = (row_t // c) == (col_t // c)
    ri = lax.broadcasted_iota(jnp.int32, (c, c), 0)
    ci = lax.broadcasted_iota(jnp.int32, (c, c), 1)
    lane = lax.broadcasted_iota(jnp.int32, (c, LANES), 1)
    ones3 = jnp.where(lane < 3, 1.0, 0.0).astype(BF16)

    dirs = ((0, qf_ref, kf_ref, vf_ref, gf_ref, of_ref), (1, qb_ref, kb_ref, vb_ref, gbk_ref, ob_ref))
    for d, q_ref, k_ref, v_ref, g_ref, o_ref in dirs:
        rev = d == 1
        before = (col_t >= row_t) if rev else (col_t <= row_t)
        tri = jnp.where(same & before, 1.0, 0.0).astype(BF16)
        p1, p2, p3 = _split3(g_ref[0])
        gc_s[d] = _dot(tri, p1) + _dot(tri, p2) + _dot(tri, p3)

    group = 4

    def prepare(gstep, carry):
        chains = []
        for d, q_ref, k_ref, v_ref, g_ref, o_ref in dirs:
            rev = d == 1
            for gi in range(group):
                cc = gstep * group + gi
                r0 = pl.multiple_of(cc * c, c)
                gcb = gc_s[d, pl.ds(r0, c), :]
                gbb = g_ref[0, pl.ds(r0, c), :]
                for h in range(D_HEADS):
                    sl = slice(h * LANES, (h + 1) * LANES)
                    col = d * D_HEADS + h
                    ch = dict(slot=(d * nc + cc) * D_HEADS + h,
                              incl=(ci >= ri) if rev else (ci <= ri), strict=(ci > ri) if rev else (ci < ri))
                    gcol = jnp.sum(jnp.where(lane == col, gcb, 0.0), axis=-1, keepdims=True)
                    ch["beta"] = jnp.sum(jnp.where(lane == 2 * D_HEADS + col, gbb, 0.0), axis=-1, keepdims=True)
                    ch["gcol"] = gcol
                    ch["gtot"] = gcol[0:1, :] if rev else gcol[c - 1:c, :]
                    ch["q"] = q_ref[0, pl.ds(r0, c), sl]
                    ch["k"] = k_ref[0, pl.ds(r0, c), sl]
                    ch["v"] = v_ref[0, pl.ds(r0, c), sl]
                    chains.append(ch)
        for ch in chains:
            g1, g2, g3 = (p.astype(F32) for p in _split3(ch["gcol"]))
            pieces = jnp.where(lane == 0, g1, jnp.where(lane == 1, g2, jnp.where(lane == 2, g3, 0.0)))
            ch["grow"] = _dot_nt(ones3, pieces.astype(BF16))
            ch["kbeta"] = ch["k"] * ch["beta"]
            kb16 = ch["k"].astype(BF16)
            ch["kk"] = _dot_nt(ch["kbeta"].astype(BF16), kb16)
            ch["qk"] = _dot_nt((ch["q"] * scale).astype(BF16), kb16)
        for ch in chains:
            decay = jnp.exp(jnp.where(ch["incl"], ch["gcol"] - ch["grow"], NEG_INF))
            lmat = jnp.where(ch["strict"], ch["kk"] * decay, 0.0)
            ch["attn"] = jnp.where(ch["incl"], ch["qk"] * decay, 0.0).astype(BF16)
            ch["nmat"] = -lmat
            x16 = lmat.astype(BF16)
            ch["xpow"] = _dot(x16, x16)
        for it in range(5):
            for ch in chains:
                x16 = ch["xpow"].astype(BF16)
                if it < 4:
                    both = _dot(jnp.concatenate([ch["nmat"].astype(BF16), x16], axis=0), x16)
                    ch["nmat"] = ch["nmat"] + ch["xpow"] + both[0:c]
                    ch["xpow"] = both[c:]
                else:
                    ch["nmat"] = ch["nmat"] + ch["xpow"] + _dot(ch["nmat"].astype(BF16), x16)
        for ch in chains:
            n16 = ch["nmat"].astype(BF16)
            vb = ch["v"] * ch["beta"]
            wk = ch["kbeta"] * jnp.exp(ch["gcol"])
            slot = ch["slot"]
            solved = _dot(n16, jnp.concatenate([vb.astype(BF16), wk.astype(BF16)], axis=1))
            u_s[slot] = vb + solved[:, 0:D_HEAD_DIM]
            wq_s[slot, 0:c] = (wk + solved[:, D_HEAD_DIM:]).astype(BF16)
            wq_s[slot, c:] = (ch["q"] * (scale * jnp.exp(ch["gcol"]))).astype(BF16)
            ak_s[slot, 0:c] = ch["attn"]
            ak_s[slot, c:] = (ch["k"] * jnp.exp(ch["gtot"] - ch["gcol"])).T.astype(BF16)
            eg_s[slot] = jnp.broadcast_to(jnp.exp(ch["gtot"]), (1, LANES))
        return carry

    lax.fori_loop(0, nc // group, prepare, 0)

    def advance(step, carry):
        chains = []
        for d, q_ref, k_ref, v_ref, g_ref, o_ref in dirs:
            cc = (nc - 1 - step) if d == 1 else step
            for h in range(D_HEADS):
                chains.append(dict(slot=(d * nc + cc) * D_HEADS + h, col=d * D_HEADS + h, o_ref=o_ref,
                                   r0=pl.multiple_of(cc * c, c), sl=slice(h * LANES, (h + 1) * LANES)))
        for ch in chains:
            ch["state"] = s_ref[ch["col"]]
            ch["ws_qs"] = _dot(wq_s[ch["slot"]], ch["state"].astype(BF16))
        for ch in chains:
            vn16 = (u_s[ch["slot"]] - ch["ws_qs"][0:c]).astype(BF16)
            ch["av_kv"] = _dot(ak_s[ch["slot"]], vn16)
        for ch in chains:
            ch["o_ref"][0, pl.ds(ch["r0"], c), ch["sl"]] = ch["ws_qs"][c:] + ch["av_kv"][0:c]
            s_ref[ch["col"]] = ch["state"] * eg_s[ch["slot"]] + ch["av_kv"][c:]
        return carry

    lax.fori_loop(0, nc, advance, 0)


def _d_chunk(q, k, v, gb, *, ts=256):
    bsz, seq, _ = q.shape
    ts = min(ts, seq)
    nt = seq // ts
    fwd = lambda wd: pl.BlockSpec((1, ts, wd), lambda b, i: (b, i, 0))
    bwd = lambda wd: pl.BlockSpec((1, ts, wd), lambda b, i: (b, nt - 1 - i, 0))
    out = jax.ShapeDtypeStruct((bsz, seq, D_WIDTH), F32)
    nslot = 2 * (ts // D_CHUNK) * D_HEADS
    return pl.pallas_call(
        functools.partial(_d_chunk_body, ts=ts),
        grid=(bsz, nt),
        in_specs=[fwd(D_WIDTH), fwd(D_WIDTH), fwd(D_WIDTH), fwd(LANES),
                  bwd(D_WIDTH), bwd(D_WIDTH), bwd(D_WIDTH), bwd(LANES)],
        out_specs=[fwd(D_WIDTH), bwd(D_WIDTH)],
        out_shape=[out, out],
        scratch_shapes=[pltpu.VMEM((2 * D_HEADS, D_HEAD_DIM, D_HEAD_DIM), F32), pltpu.VMEM((2, ts, LANES), F32),
                        pltpu.VMEM((nslot, D_CHUNK, D_HEAD_DIM), F32),
                        pltpu.VMEM((nslot, 2 * D_CHUNK, D_HEAD_DIM), BF16),
                        pltpu.VMEM((nslot, D_CHUNK + D_HEAD_DIM, D_CHUNK), BF16),
                        pltpu.VMEM((nslot, 1, LANES), F32)],
        compiler_params=_params(("parallel", "arbitrary")),
        name="d_chunk",
    )(q, k, v, gb, q, k, v, gb)


def _gelu_tanh(x):
    return 0.5 * x * (1.0 + jnp.tanh(np.sqrt(2.0 / np.pi).astype(np.float32) * (x + 0.044715 * (x * x * x))))


def _merge_body(x_ref, o1_ref, l1_ref, o4_ref, l4_ref, o16_ref, l16_ref, ob_ref,
                hf_ref, hb_ref, cg_ref, df_ref, db_ref, dz_ref, gate_ref,
                wbr_ref, wout_ref, ong_ref, y_ref):
    slabs = []
    for c in range(A_WIDTH // LANES):
        l1, l4, l16 = l1_ref[0, c], l4_ref[0, c], l16_ref[0, c]
        m = jnp.maximum(jnp.maximum(l1, l4), l16)
        e1, e4, e16 = jnp.exp2(l1 - m), jnp.exp2(l4 - m), jnp.exp2(l16 - m)
        slabs.append((e1 * o1_ref[0, c] + e4 * o4_ref[0, c] + e16 * o16_ref[0, c]) / (e1 + e4 + e16))
    oa = jnp.concatenate(slabs, axis=1)
    oc = (hf_ref[0] + hb_ref[0]) * _gelu_tanh(cg_ref[0].astype(F32))
    od_sum = df_ref[0] + db_ref[0]
    z = dz_ref[0].astype(F32)
    heads = []
    for h in range(D_HEADS):
        sl = slice(h * LANES, (h + 1) * LANES)
        heads.append(_rms_rows(od_sum[:, sl], ong_ref[...]) * _silu(z[:, sl]))
    od = jnp.concatenate(heads, axis=1)

    def gate(j):
        return _sigmoid(gate_ref[0, :, j * D_MODEL:(j + 1) * D_MODEL]).astype(F32)

    r_a, r_b, r_c = A_WIDTH, A_WIDTH + B_HEADS * LANES, A_WIDTH + B_HEADS * LANES + C_WIDTH
    merged = gate(0) * _dot(oa.astype(BF16), wbr_ref[0:r_a, :])
    merged = merged + gate(1) * _dot(ob_ref[0], wbr_ref[r_a:r_b, :])
    merged = merged + gate(2) * _dot(oc.astype(BF16), wbr_ref[r_b:r_c, :])
    merged = merged + gate(3) * _dot(od.astype(BF16), wbr_ref[r_c:, :])
    y_ref[0] = x_ref[0] + _dot(merged.astype(BF16), wout_ref[...])


def _merge(x, a_outs, ob, hf, hb, cg, df, db, dz, gates, w, *, tm=256):
    bsz, seq, _ = x.shape
    tok = lambda wd: pl.BlockSpec((1, tm, wd), lambda b, i: (b, i, 0))
    consts = [w["w_branch"], w["w_out"], w["d_on_g"]]
    ins = [x]
    specs = [tok(D_MODEL)]
    slab = pl.BlockSpec((1, A_WIDTH // LANES, tm, LANES), lambda b, i: (b, 0, i, 0))
    for o, lse in a_outs:
        ins += [o, lse]
        specs += [slab, slab]
    ins += [ob, hf, hb, cg, df, db, dz, gates]
    specs += [tok(B_HEADS * LANES), tok(C_WIDTH), tok(C_WIDTH), tok(C_WIDTH), tok(D_WIDTH), tok(D_WIDTH),
              tok(D_WIDTH), tok(4 * D_MODEL)]
    return pl.pallas_call(
        _merge_body,
        grid=(bsz, seq // tm),
        in_specs=specs + [_const_spec(c.shape) for c in consts],
        out_specs=tok(D_MODEL),
        out_shape=jax.ShapeDtypeStruct(x.shape, F32),
        compiler_params=_params(("parallel", "parallel")),
        name="merge",
    )(*ins, *consts)


FF_TILE = 256


def _ffn_body(x_ref, ln_ref, wgu_ref, wdn_ref, y_ref):
    x = x_ref[0]
    xn = _rms_rows(x, ln_ref[...]).astype(BF16)
    acc = x
    for j in range(FF_DIM // FF_TILE):
        g = _dot(xn, wgu_ref[:, j * FF_TILE:(j + 1) * FF_TILE])
        u = _dot(xn, wgu_ref[:, FF_DIM + j * FF_TILE:FF_DIM + (j + 1) * FF_TILE])
        acc = acc + _dot((_silu(g) * u).astype(BF16), wdn_ref[j * FF_TILE:(j + 1) * FF_TILE, :])
    y_ref[0] = acc


def _ffn(x, w, *, tm=512):
    bsz, seq, _ = x.shape
    tok = pl.BlockSpec((1, tm, D_MODEL), lambda b, i: (b, i, 0))
    consts = [w["ln2"], w["w_up"], w["w_down"]]
    return pl.pallas_call(
        _ffn_body,
        grid=(bsz, seq // tm),
        in_specs=[tok] + [_const_spec(c.shape) for c in consts],
        out_specs=tok,
        out_shape=jax.ShapeDtypeStruct(x.shape, F32),
        compiler_params=_params(("parallel", "parallel")),
        name="ffn",
    )(x, *consts)


def _block_diag(wblocks):
    g, i, o = wblocks.shape
    eye = jnp.eye(g, dtype=wblocks.dtype)
    return (eye[:, None, :, None] * wblocks[:, :, None, :]).reshape(g * i, g * o)


def _rope_offsets(consts, rows):
    ang = jnp.arange(rows, dtype=F32)[:, None] * consts[0][None, :]
    return jnp.stack([jnp.cos(ang), jnp.sin(ang)], axis=0)


def _rope_consts(head_period, rope_start, rope_dim):
    lane = np.arange(LANES) % head_period
    j = lane - rope_start
    in_rope = (j >= 0) & (j < rope_dim)
    half = rope_dim // 2
    inv_freq = 1.0 / (ROPE_THETA ** (jnp.arange(0, rope_dim, 2, dtype=F32) / rope_dim))
    inv_lane = jnp.where(jnp.asarray(in_rope), inv_freq[np.where(in_rope, j % half, 0)], 0.0)
    out = jnp.zeros((SUBLANES, LANES), F32)
    out = out.at[0].set(inv_lane)
    out = out.at[1].set(jnp.asarray(np.where(in_rope & (j < half), -1.0, 0.0), F32))
    out = out.at[2].set(jnp.asarray(np.where(in_rope & (j >= half), 1.0, 0.0), F32))
    return out


def _pad_heads(wmat, heads, dim):
    r = wmat.shape[0]
    w3 = wmat.reshape(r, heads, dim)
    return jnp.pad(w3, ((0, 0), (0, 0), (0, LANES - dim))).reshape(r, heads * LANES)


def _pad_lane_vec(vec, dim):
    return jnp.pad(vec, (0, LANES - dim)).reshape(1, LANES)


def _layer_weights(l, ln1_g, w_in, a_qn_g, a_kn_g, b_qa_g, b_wuq, b_kva_g, b_wukv, b_qn_g, b_kn_g,
                   c_conv_w, c_conv_b, c_wr, c_br, c_wi, c_bi, c_lam, d_conv_w, d_a_log, d_dt_bias, d_on_g,
                   w_branch, w_out, ln2_g, w_up, w_down):
    wi = w_in[l]
    row = lambda v: v.reshape(1, -1)
    w = {}
    w["ln1"] = row(ln1_g[l])
    w["wa"] = wi[:, O_AQ:O_BCQ].astype(BF16)
    zeros = lambda n: jnp.zeros((D_MODEL, n), F32)
    w["wb"] = jnp.concatenate([wi[:, O_BCQ:O_BKR], zeros(B_NOPE_DIM), wi[:, O_BKR:O_CX],
                               zeros(LANES - B_QK_DIM)], axis=1).astype(BF16)
    w["wc"] = wi[:, O_CX:O_DQ].astype(BF16)
    w["wd"] = wi[:, O_DQ:O_DA].astype(BF16)
    w["wab"] = jnp.concatenate([wi[:, O_DA:O_GATE], zeros(LANES - 4 * D_HEADS)], axis=1).astype(BF16)
    w["wg"] = wi[:, O_GATE:O_END].astype(BF16)
    w["gmat"] = _block_diag(jnp.ones((A_HEADS, A_HEAD_DIM, A_HEAD_DIM), F32)).astype(BF16)
    w["aqg"] = row(jnp.tile(a_qn_g[l], A_HEADS))
    w["akg"] = row(jnp.tile(a_kn_g[l], A_HEADS))
    w["arope"] = _rope_consts(A_HEAD_DIM, 0, A_ROPE_DIM)
    w["aoff"] = _rope_offsets(w["arope"], IN_TILE)
    w["qag"] = row(b_qa_g[l])
    w["wuq"] = _pad_heads(b_wuq[l], B_HEADS, B_QK_DIM).astype(BF16)
    w["kvag"] = row(b_kva_g[l])
    wukv = b_wukv[l].reshape(B_KV_RANK, B_HEADS, B_NOPE_DIM + B_V_DIM)
    w["wukv"] = jnp.concatenate([
        _pad_heads(wukv[:, :, :B_NOPE_DIM].reshape(B_KV_RANK, -1), B_HEADS, B_NOPE_DIM),
        _pad_heads(wukv[:, :, B_NOPE_DIM:].reshape(B_KV_RANK, -1), B_HEADS, B_V_DIM)], axis=1).astype(BF16)
    w["bqg"] = _pad_lane_vec(b_qn_g[l], B_QK_DIM)
    w["bkg"] = _pad_lane_vec(b_kn_g[l], B_QK_DIM)
    w["brope"] = _rope_consts(LANES, B_NOPE_DIM, B_ROPE_DIM)
    w["boff"] = _rope_offsets(w["brope"], IN_TILE)

    w["c_conv_w"] = jnp.pad(c_conv_w[l], ((0, SUBLANES - 4), (0, 0)))
    w["c_conv_b"] = row(c_conv_b[l])
    w["c_wf"] = jnp.concatenate([_block_diag(c_wr[l, 0]), _block_diag(c_wi[l, 0])], axis=1).astype(BF16)
    w["c_wb"] = jnp.concatenate([_block_diag(c_wr[l, 1]), _block_diag(c_wi[l, 1])], axis=1).astype(BF16)
    w["c_bf"] = row(jnp.concatenate([c_br[l, 0], c_bi[l, 0]]))
    w["c_bb"] = row(jnp.concatenate([c_br[l, 1], c_bi[l, 1]]))
    w["c_lam"] = jnp.pad(c_lam[l], ((0, SUBLANES - 2), (0, 0)))

    w["d_conv_w"] = jnp.pad(d_conv_w[l], ((0, SUBLANES - 4), (0, 0)))
    dconst = jnp.zeros((SUBLANES, LANES), F32)
    dconst = dconst.at[0, 0:2 * D_HEADS].set(d_a_log[l].reshape(-1))
    dconst = dconst.at[1, 0:2 * D_HEADS].set(d_dt_bias[l].reshape(-1))
    w["d_const"] = dconst
    w["d_on_g"] = row(d_on_g[l])

    wbr = w_branch[l]
    wbr_b = jnp.pad(wbr[A_WIDTH:A_WIDTH + B_HEADS * B_V_DIM].reshape(B_HEADS, B_V_DIM, D_MODEL),
                    ((0, 0), (0, LANES - B_V_DIM), (0, 0))).reshape(B_HEADS * LANES, D_MODEL)
    w["w_branch"] = jnp.concatenate([wbr[:A_WIDTH], wbr_b, wbr[A_WIDTH + B_HEADS * B_V_DIM:]], axis=0).astype(BF16)
    w["w_out"] = w_out[l].astype(BF16)
    w["ln2"] = row(ln2_g[l])
    w["w_up"] = w_up[l].astype(BF16)
    w["w_down"] = w_down[l].astype(BF16)
    return w


def _layer(x, w):
    aqs, aks, avs, bq, bk, bv, cx, cg, dqkv, dz, dab, gates = _in_proj(x, w)
    a_outs = [_a_attn(q, k, v, dil) for q, k, v, dil in zip(aqs, aks, avs, A_DILATIONS)]
    ob = _b_flash(bq, bk, bv)
    hf, hb = _rglru(cx, w)
    dq, dk, dv, dgb = _d_prep(dqkv, dab, w)
    df, db = _d_chunk(dq, dk, dv, dgb)
    x = _merge(x, a_outs, ob, hf, hb, cg, df, db, dz, gates, w)
    return _ffn(x, w)


def kernel(x_prompt, x_sample, ln1_g, w_in, a_qn_g, a_kn_g, b_qa_g, b_wuq, b_kva_g, b_wukv, b_qn_g, b_kn_g,
           c_conv_w, c_conv_b, c_wr, c_br, c_wi, c_bi, c_lam, d_conv_w, d_a_log, d_dt_bias, d_on_g,
           w_branch, w_out, ln2_g, w_up, w_down):
    depth = ln1_g.shape[0]
    weights = [_layer_weights(l, ln1_g, w_in, a_qn_g, a_kn_g, b_qa_g, b_wuq, b_kva_g, b_wukv, b_qn_g, b_kn_g,
                              c_conv_w, c_conv_b, c_wr, c_br, c_wi, c_bi, c_lam, d_conv_w, d_a_log, d_dt_bias,
                              d_on_g, w_branch, w_out, ln2_g, w_up, w_down) for l in range(depth)]

    def trunk(x):
        for w in weights:
            x = _layer(x, w)
        return x

    return (trunk(x_prompt), trunk(x_sample))
```

```python
import functools

import jax
import jax.numpy as jnp
import numpy as np
from jax import lax
from jax.experimental import pallas as pl
from jax.experimental.pallas import tpu as pltpu

F32 = jnp.float32
BF16 = jnp.bfloat16

D_MODEL = 1024
ROPE_THETA = 500000.0
NORM_EPS = 1e-6
NEG_INF = -1e30
LOG2_E = 1.4426950408889634

A_HEADS = 8
A_HEAD_DIM = 64
A_ROPE_DIM = 16
A_DILATIONS = (1, 4, 16)
A_HALF = 64
A_WIDTH = A_HEADS * A_HEAD_DIM
A_QBLK = 128
A_SLAB_GROUPS = 4
A_TOKENS = 2048

B_HEADS = 8
B_NOPE_DIM = 64
B_ROPE_DIM = 32
B_QK_DIM = B_NOPE_DIM + B_ROPE_DIM
B_V_DIM = 64
B_Q_RANK = 256
B_KV_RANK = 128

C_WIDTH = 512
C_BLOCKS = 8
C_BLOCK_DIM = 64
C_GATE_C = 8.0

D_HEADS = 4
D_HEAD_DIM = 128
D_WIDTH = 512
D_CHUNK = 64

FF_DIM = 2816
LANES = 128
SUBLANES = 8
VMEM_LIMIT = 56 * 1024 * 1024
IN_TILE = 256

_OFF = np.cumsum([0, 512, 512, 512, 256, 128, 32, 512, 512, 512, 512, 512, 512, 8, 8, 4096])
(O_AQ, O_AK, O_AV, O_BCQ, O_BCKV, O_BKR, O_CX, O_CG, O_DQ, O_DK, O_DV, O_DZ, O_DA, O_DB, O_GATE, O_END) = [int(v) for v in _OFF]


def _const_spec(shape):
    nd = len(shape)
    return pl.BlockSpec(shape, lambda *_: (0,) * nd, pipeline_mode=pl.Buffered(1))


def _params(sem):
    return pltpu.CompilerParams(dimension_semantics=sem, vmem_limit_bytes=VMEM_LIMIT)


def _dot(a, b):
    return jnp.dot(a, b, preferred_element_type=F32)


def _dot_nt(a, b):
    return lax.dot_general(a, b, (((1,), (1,)), ((), ())), preferred_element_type=F32)


def _dot_tn(a, b):
    return lax.dot_general(a, b, (((0,), (0,)), ((), ())), preferred_element_type=F32)


def _rms_rows(x, g):
    return x * lax.rsqrt(jnp.mean(x * x, axis=-1, keepdims=True) + NORM_EPS) * g


def _split3(x):
    p1 = x.astype(BF16)
    r1 = x - p1.astype(F32)
    p2 = r1.astype(BF16)
    p3 = (r1 - p2.astype(F32)).astype(BF16)
    return p1, p2, p3


def _sigmoid(x):
    return 0.5 * jnp.tanh(0.5 * x) + 0.5


def _silu(x):
    return x * _sigmoid(x)


def _softplus(x):
    return jnp.maximum(x, 0.0) + jnp.log(1.0 + jnp.exp(-jnp.abs(x)))


def _rope_tables(pos0, consts, off_ref):
    base = pos0.astype(F32) * consts[0:1, :]
    cb, sb = jnp.cos(base), jnp.sin(base)
    co, so = off_ref[0], off_ref[1]
    c = cb * co - sb * so
    s = sb * co + cb * so
    return c, s * consts[1:2, :], s * consts[2:3, :]


def _in_proj_body(x_ref, ln_ref, wa_ref, wb_ref, wc_ref, wd_ref, wab_ref, wg_ref,
                  aqg_ref, akg_ref, arope_ref, aoff_ref,
                  qag_ref, wuq_ref, kvag_ref, wukv_ref, bqg_ref, bkg_ref, brope_ref, boff_ref,
                  aq1_ref, aq4_ref, aq16_ref, ak1_ref, ak4_ref, ak16_ref, av1_ref, av4_ref, av16_ref,
                  bq_ref, bk_ref, bv_ref,
                  cx_ref, cg_ref, dqkv_ref, dz_ref, dab_ref, gate_ref, perm_q, perm_k, perm_v, *, tm):
    aq_refs = (aq1_ref, aq4_ref, aq16_ref)
    ak_refs = (ak1_ref, ak4_ref, ak16_ref)
    av_refs = (av1_ref, av4_ref, av16_ref)
    pos0 = pl.program_id(1) * tm
    xn = _rms_rows(x_ref[0], ln_ref[...]).astype(BF16)

    def xdot(wmat):
        return _dot(xn, wmat)

    hb = xdot(wb_ref[...])
    cq = _rms_rows(hb[:, 0:B_Q_RANK], qag_ref[...]).astype(BF16)
    ckv = _rms_rows(hb[:, B_Q_RANK:B_Q_RANK + B_KV_RANK], kvag_ref[...]).astype(BF16)
    kr = hb[:, B_Q_RANK + B_KV_RANK:]
    cb, sb1, sb2 = _rope_tables(pos0, brope_ref[...], boff_ref)
    lane = lax.broadcasted_iota(jnp.int32, (tm, LANES), 1)
    ones_col = jnp.where(lane == B_V_DIM, 1.0, 0.0)

    def b_norm_rope(t, g):
        ms = jnp.sum(t * t, axis=-1, keepdims=True) * (1.0 / B_QK_DIM)
        y = t * lax.rsqrt(ms + NORM_EPS) * g
        return y * cb + pltpu.roll(y, LANES - B_ROPE_DIM // 2, 1) * sb1 + pltpu.roll(y, B_ROPE_DIM // 2, 1) * sb2

    for hp in range(B_HEADS // 2):
        pair = slice(2 * hp * LANES, (2 * hp + 2) * LANES)
        q2 = _dot(cq, wuq_ref[:, pair])
        k2 = _dot(ckv, wukv_ref[:, pair])
        v2 = _dot(ckv, wukv_ref[:, B_HEADS * LANES + 2 * hp * LANES:B_HEADS * LANES + (2 * hp + 2) * LANES])
        for j in range(2):
            h = 2 * hp + j
            sl = slice(j * LANES, (j + 1) * LANES)
            bq_ref[0, h] = (b_norm_rope(q2[:, sl], bqg_ref[...]) * (B_QK_DIM ** -0.5 * LOG2_E)).astype(BF16)
            bk_ref[0, h] = b_norm_rope(k2[:, sl] + kr, bkg_ref[...]).astype(BF16)
            bv_ref[0, h] = (v2[:, sl] + ones_col).astype(BF16)

    ca, sa1, sa2 = _rope_tables(pos0, arope_ref[...], aoff_ref)
    ca = jnp.concatenate([ca] * 4, axis=1)
    sa1 = jnp.concatenate([sa1] * 4, axis=1)
    sa2 = jnp.concatenate([sa2] * 4, axis=1)

    lane_a = lax.broadcasted_iota(jnp.int32, (tm, LANES), 1)
    first_head = lane_a < A_HEAD_DIM

    def a_norm_rope(t, g):
        sq = t * t
        slabs = []
        for c in range(A_WIDTH // LANES):
            s2 = sq[:, c * LANES:(c + 1) * LANES]
            sum_a = jnp.sum(jnp.where(first_head, s2, 0.0), axis=-1, keepdims=True)
            sum_b = jnp.sum(jnp.where(first_head, 0.0, s2), axis=-1, keepdims=True)
            slabs.append(jnp.where(first_head, sum_a, sum_b))
        ms = jnp.concatenate(slabs, axis=1) * (1.0 / A_HEAD_DIM)
        y = t * lax.rsqrt(ms + NORM_EPS) * g
        return y * ca + pltpu.roll(y, A_WIDTH - A_ROPE_DIM // 2, 1) * sa1 + pltpu.roll(y, A_ROPE_DIM // 2, 1) * sa2

    def emit(y, refs, perm):
        refs[0][0] = y.astype(BF16)
        nslab = A_WIDTH // LANES
        for c in range(nslab):
            perm[c] = y[:, c * LANES:(c + 1) * LANES]
        for ref, d in zip(refs[1:], A_DILATIONS[1:]):
            for r in range(d):
                for c in range(nslab):
                    col = r * A_WIDTH + c * LANES
                    ref[0, :, col:col + LANES] = perm[c, pl.ds(r, tm // d, stride=d), :].astype(BF16)

    q = xdot(wa_ref[:, 0:A_WIDTH])
    emit(a_norm_rope(q, aqg_ref[...]) * (A_HEAD_DIM ** -0.5 * LOG2_E), aq_refs, perm_q)
    k = xdot(wa_ref[:, A_WIDTH:2 * A_WIDTH])
    emit(a_norm_rope(k, akg_ref[...]), ak_refs, perm_k)
    emit(xdot(wa_ref[:, 2 * A_WIDTH:3 * A_WIDTH]), av_refs, perm_v)

    hc = xdot(wc_ref[...])
    cx_ref[0] = hc[:, 0:C_WIDTH]
    cg_ref[0] = hc[:, C_WIDTH:].astype(BF16)
    for j in range(3):
        dqkv_ref[0, :, j * D_WIDTH:(j + 1) * D_WIDTH] = xdot(wd_ref[:, j * D_WIDTH:(j + 1) * D_WIDTH])
    dz_ref[0] = xdot(wd_ref[:, 3 * D_WIDTH:]).astype(BF16)
    dab_ref[0] = xdot(wab_ref[...])
    for j in range(4 * D_MODEL // 512):
        gate_ref[0, :, j * 512:(j + 1) * 512] = xdot(wg_ref[:, j * 512:(j + 1) * 512]).astype(BF16)


def _in_proj(x, w, *, tm=IN_TILE):
    bsz, seq, _ = x.shape
    grid = (bsz, seq // tm)
    tok = lambda width: pl.BlockSpec((1, tm, width), lambda b, i: (b, i, 0))
    head = pl.BlockSpec((1, B_HEADS, tm, LANES), lambda b, i: (b, 0, i, 0))
    consts = [w["ln1"], w["wa"], w["wb"], w["wc"], w["wd"], w["wab"], w["wg"],
              w["aqg"], w["akg"], w["arope"], w["aoff"],
              w["qag"], w["wuq"], w["kvag"], w["wukv"], w["bqg"], w["bkg"], w["brope"], w["boff"]]
    assert tm == IN_TILE
    tokf = lambda width, dt: jax.ShapeDtypeStruct((bsz, seq, width), dt)
    headf = jax.ShapeDtypeStruct((bsz, B_HEADS, seq, LANES), BF16)
    a_specs = [pl.BlockSpec((1, tm // d, d * A_WIDTH), lambda b, i: (b, i, 0)) for d in A_DILATIONS] * 3
    a_shapes = [jax.ShapeDtypeStruct((bsz, seq // d, d * A_WIDTH), BF16) for d in A_DILATIONS] * 3
    outs = pl.pallas_call(
        functools.partial(_in_proj_body, tm=tm),
        grid=grid,
        in_specs=[tok(D_MODEL)] + [_const_spec(c.shape) for c in consts],
        out_specs=a_specs + [head, head, head,
                             tok(C_WIDTH), tok(C_WIDTH), tok(3 * D_WIDTH), tok(D_WIDTH), tok(LANES), tok(4 * D_MODEL)],
        out_shape=a_shapes + [headf, headf, headf,
                              tokf(C_WIDTH, F32), tokf(C_WIDTH, BF16), tokf(3 * D_WIDTH, F32), tokf(D_WIDTH, BF16),
                              tokf(LANES, F32), tokf(4 * D_MODEL, BF16)],
        scratch_shapes=[pltpu.VMEM((A_WIDTH // LANES, tm, LANES), F32)] * 3,
        compiler_params=_params(("parallel", "parallel")),
        name="in_proj",
    )(x, *consts)
    nd = len(A_DILATIONS)
    return (outs[0:nd], outs[nd:2 * nd], outs[2 * nd:3 * nd]) + tuple(outs[3 * nd:])


def _a_attn_body(q_ref, k_ref, kp_ref, kn_ref, v_ref, vp_ref, vn_ref, o_ref, lse_ref, *, dil, lq, sub_len):
    blk = A_QBLK
    half = A_HALF
    nsub = lq // blk
    npiece = lq // half
    i = pl.program_id(1)
    kw = blk + 2 * half
    rel = lax.broadcasted_iota(jnp.int32, (blk, kw), 1) - half - lax.broadcasted_iota(jnp.int32, (blk, kw), 0)
    band = jnp.abs(rel) <= A_HALF
    kcol = lax.broadcasted_iota(jnp.int32, (blk, kw), 1)
    lane = lax.broadcasted_iota(jnp.int32, (blk, LANES), 1)
    first = lane < A_HEAD_DIM

    def window(main_ref, prev_ref, next_ref, j, cols):
        parts = []
        first_piece = j * (blk // half) - 1
        for t in range(first_piece, first_piece + kw // half):
            if t < 0:
                parts.append(prev_ref[0, :, cols])
            elif t >= npiece:
                parts.append(next_ref[0, :, cols])
            else:
                parts.append(main_ref[0, t * half:(t + 1) * half, cols])
        return jnp.concatenate(parts, axis=0)

    valids = []
    for j in range(nsub):
        kpos = i * lq + j * blk - half + kcol
        valids.append(band & (kpos >= 0) & (kpos < sub_len))

    nslab = A_HEADS // 2
    per_group = nslab // A_SLAB_GROUPS

    def slabs_of(g):
        return range(g * per_group, (g + 1) * per_group)

    def cols_of(r, g):
        return [slice(r * A_WIDTH + hp * LANES, r * A_WIDTH + (hp + 1) * LANES) for hp in slabs_of(g)]

    def stage_scores(j, r, g):
        scores = []
        for cl in cols_of(r, g):
            q2 = q_ref[0, j * blk:(j + 1) * blk, cl]
            k2 = window(k_ref, kp_ref, kn_ref, j, cl)
            zero = jnp.zeros_like(q2)
            scores.append([_dot_nt(qh, k2) for qh in (jnp.where(first, q2, zero), jnp.where(first, zero, q2))])
        return scores

    def stage_softmax(j, scores):
        masked = [[jnp.where(valids[j], s, NEG_INF) for s in pair] for pair in scores]
        maxes = [[jnp.max(s, axis=-1, keepdims=True) for s in pair] for pair in masked]
        probs = [[jnp.exp2(s - m).astype(BF16) for s, m in zip(ps, ms)] for ps, ms in zip(masked, maxes)]
        return probs, maxes

    ones_cols = jnp.ones((kw, LANES), BF16)

    def stage_values(j, r, g, probs):
        outs = []
        for pp, cl in zip(probs, cols_of(r, g)):
            v_ext = jnp.concatenate([window(v_ref, vp_ref, vn_ref, j, cl), ones_cols], axis=1)
            outs.append([_dot(p, v_ext) for p in pp])
        return outs

    def stage_store(j, r, g, outs, maxes):
        rows = pl.ds(j * blk * dil + r, blk, stride=dil) if dil > 1 else pl.ds(j * blk, blk)
        for hp, (oa, ob), (ma, mb) in zip(slabs_of(g), outs, maxes):
            za, zb = oa[:, LANES:], ob[:, LANES:]
            o_ref[0, hp, rows, :] = jnp.where(first, oa[:, :LANES] / za, ob[:, :LANES] / zb)
            lse_ref[0, hp, rows, :] = jnp.where(first, ma + jnp.log(za) * LOG2_E, mb + jnp.log(zb) * LOG2_E)

    problems = [(j, r, g) for j in range(nsub) for r in range(dil) for g in range(A_SLAB_GROUPS)]
    n = len(problems)
    sc, pr, st, ou = {}, {}, {}, {}
    for t in range(n + 3):
        if t < n:
            sc[t] = stage_scores(*problems[t])
        if 0 <= t - 1 < n:
            pr[t - 1], st[t - 1] = stage_softmax(problems[t - 1][0], sc.pop(t - 1))
        if 0 <= t - 2 < n:
            ou[t - 2] = stage_values(*problems[t - 2], pr.pop(t - 2))
        if 0 <= t - 3 < n:
            stage_store(*problems[t - 3], ou.pop(t - 3), st.pop(t - 3))


def _a_attn(qv, kv, vv, dil):
    bsz, sub_len, vwidth = qv.shape
    seq = sub_len * dil
    lq = A_TOKENS // dil
    nblk = sub_len // A_HALF
    per = lq // A_HALF
    main = pl.BlockSpec((1, lq, vwidth), lambda b, i: (b, i, 0))
    prev = pl.BlockSpec((1, A_HALF, vwidth), lambda b, i: (b, jnp.maximum(i * per - 1, 0), 0))
    nxt = pl.BlockSpec((1, A_HALF, vwidth), lambda b, i: (b, jnp.minimum((i + 1) * per, nblk - 1), 0))
    nslab = A_WIDTH // LANES
    out = pl.BlockSpec((1, nslab, A_TOKENS, LANES), lambda b, i: (b, 0, i, 0))
    oshape = jax.ShapeDtypeStruct((bsz, nslab, seq, LANES), F32)
    return pl.pallas_call(
        functools.partial(_a_attn_body, dil=dil, lq=lq, sub_len=sub_len),
        grid=(bsz, seq // A_TOKENS),
        in_specs=[main, main, prev, nxt, main, prev, nxt],
        out_specs=[out, out],
        out_shape=[oshape, oshape],
        compiler_params=_params(("parallel", "parallel")),
        name=f"a_attn_d{dil}",
    )(qv, kv, kv, kv, vv, vv, vv)


def _b_flash_body(q_ref, k_ref, v_ref, o_ref, *, tq, tk, seq, unroll, nq):
    nk = seq // tk
    lane = lax.broadcasted_iota(jnp.int32, (tq, LANES), 1)

    for t in range(nq):
        q = q_ref[0, 0, t * tq:(t + 1) * tq, :]

        def scores(c, q=q):
            return _dot_nt(q, k_ref[0, 0, pl.ds(pl.multiple_of(c * tk, tk), tk), :])

        def update(c, m, acc, s):
            m_new = jnp.maximum(m, jnp.max(s, axis=-1, keepdims=True))
            alpha = jnp.exp2(m - m_new)
            p = jnp.exp2((s - m_new).astype(BF16))
            acc = alpha * acc + _dot(p, v_ref[0, 0, pl.ds(pl.multiple_of(c * tk, tk), tk), :])
            return m_new, acc

        def chunk(c, carry, scores=scores):
            m, acc, s = carry
            s_next = scores(jnp.minimum(c + 1, nk - 1))
            m, acc = update(c, m, acc, s)
            return m, acc, s_next

        m = jnp.full((tq, 1), NEG_INF, F32)
        acc = jnp.zeros((tq, LANES), F32)
        if nk <= unroll:
            s = scores(0)
            for c in range(nk):
                s_next = scores(c + 1) if c + 1 < nk else None
                m, acc = update(c, m, acc, s)
                s = s_next
        else:
            _, acc, _ = lax.fori_loop(0, nk, chunk, (m, acc, scores(0)), unroll=unroll)
        denom = jnp.sum(jnp.where(lane == B_V_DIM, acc, 0.0), axis=-1, keepdims=True)
        o_ref[0, t * tq:(t + 1) * tq, :] = jnp.where(lane < B_V_DIM, acc / denom, 0.0).astype(BF16)


def _b_flash(q, k, v, *, tq=512, tk=512, unroll=32, nq=2):
    bsz, heads, seq, _ = q.shape
    tstep = tq * nq
    return pl.pallas_call(
        functools.partial(_b_flash_body, tq=tq, tk=tk, seq=seq, unroll=unroll, nq=nq),
        grid=(bsz, heads, seq // tstep),
        in_specs=[pl.BlockSpec((1, 1, tstep, LANES), lambda b, h, i: (b, h, i, 0)),
                  pl.BlockSpec((1, 1, seq, LANES), lambda b, h, i: (b, h, 0, 0)),
                  pl.BlockSpec((1, 1, seq, LANES), lambda b, h, i: (b, h, 0, 0))],
        out_specs=pl.BlockSpec((1, tstep, LANES), lambda b, h, i: (b, i, h)),
        out_shape=jax.ShapeDtypeStruct((bsz, seq, heads * LANES), BF16),
        compiler_params=_params(("parallel", "parallel", "parallel")),
        name="b_flash",
    )(q, k, v)


def _fill_ext(ext, prev_ref, main_ref, next_ref, is_first, is_last, ts):
    h = SUBLANES
    ext[0:h] = jnp.where(is_first, 0.0, prev_ref[0])
    ext[h:h + ts] = main_ref[0]
    ext[h + ts:] = jnp.where(is_last, 0.0, next_ref[0])


def _conv4(ext, w_ref, ts):
    h = SUBLANES
    out = ext[h - 2:h - 2 + ts] * w_ref[0:1, :]
    for j in range(1, 4):
        out = out + ext[h - 2 + j:h - 2 + j + ts] * w_ref[j:j + 1, :]
    return out


def _halo_specs(ts, width, nt, rev):
    per = ts // SUBLANES
    nblk = nt * per
    t = (lambda i: nt - 1 - i) if rev else (lambda i: i)
    main = pl.BlockSpec((1, ts, width), lambda b, i: (b, t(i), 0))
    prev = pl.BlockSpec((1, SUBLANES, width), lambda b, i: (b, jnp.maximum(t(i) * per - 1, 0), 0))
    nxt = pl.BlockSpec((1, SUBLANES, width), lambda b, i: (b, jnp.minimum((t(i) + 1) * per, nblk - 1), 0))
    return main, prev, nxt


def _scan_tiles(au_s, h_refs, carry_refs, ts):
    n = ts // SUBLANES
    row = lax.broadcasted_iota(jnp.int32, (SUBLANES, C_WIDTH), 0)

    def one(d, g, carry):
        reverse = d == 1
        gi = (n - 1 - g) if reverse else g
        r0 = pl.multiple_of(gi * SUBLANES, SUBLANES)
        a = au_s[2 * d, pl.ds(r0, SUBLANES), :]
        b = au_s[2 * d + 1, pl.ds(r0, SUBLANES), :]
        for sh in (1, 2, 4):
            if reverse:
                keep = row < SUBLANES - sh
                shift = SUBLANES - sh
            else:
                keep = row >= sh
                shift = sh
            a_sh = pltpu.roll(a, shift, 0)
            b_sh = pltpu.roll(b, shift, 0)
            b = jnp.where(keep, a * b_sh + b, b)
            a = jnp.where(keep, a * a_sh, a)
        h = b + a * carry
        h_refs[d][0, pl.ds(r0, SUBLANES), :] = h
        last = h[0:1, :] if reverse else h[SUBLANES - 1:SUBLANES, :]
        return jnp.broadcast_to(last, (SUBLANES, C_WIDTH))

    def group(g, carry):
        return one(0, g, carry[0]), one(1, g, carry[1])

    cf, cb = lax.fori_loop(0, n, group, (carry_refs[0][...], carry_refs[1][...]))
    carry_refs[0][...] = cf
    carry_refs[1][...] = cb


def _rglru_body(xf_ref, xfp_ref, xfn_ref, xb_ref, xbp_ref, xbn_ref,
                cw_ref, cb_ref, wf_ref, wb_ref, bf_ref, bb_ref, lam_ref,
                hf_ref, hb_ref, ext_f, ext_b, au_s, carry_f, carry_b, *, ts, nt):
    i = pl.program_id(1)

    @pl.when(i == 0)
    def _():
        carry_f[...] = jnp.zeros_like(carry_f)
        carry_b[...] = jnp.zeros_like(carry_b)

    def gates(d, ext, main, prev, nxt, tile, w_ref, b_ref):
        _fill_ext(ext, prev, main, nxt, tile == 0, tile == nt - 1, ts)
        xc = _conv4(ext, cw_ref, ts) + cb_ref[...]
        g = _dot(xc.astype(BF16), w_ref[...]) + b_ref[...]
        r = _sigmoid(g[:, 0:C_WIDTH])
        gi = _sigmoid(g[:, C_WIDTH:])
        log_a = (-C_GATE_C * _softplus(-lam_ref[d:d + 1, :])) * r
        a = jnp.exp(log_a)
        au_s[2 * d] = a
        au_s[2 * d + 1] = jnp.sqrt(1.0 - a * a) * (gi * xc)

    gates(0, ext_f, xf_ref, xfp_ref, xfn_ref, i, wf_ref, bf_ref)
    gates(1, ext_b, xb_ref, xbp_ref, xbn_ref, nt - 1 - i, wb_ref, bb_ref)
    _scan_tiles(au_s, (hf_ref, hb_ref), (carry_f, carry_b), ts)


def _rglru(cx, w, *, ts=512):
    bsz, seq, _ = cx.shape
    ts = min(ts, seq)
    nt = seq // ts
    fm, fp, fn = _halo_specs(ts, C_WIDTH, nt, False)
    bm, bp, bn = _halo_specs(ts, C_WIDTH, nt, True)
    consts = [w["c_conv_w"], w["c_conv_b"], w["c_wf"], w["c_wb"], w["c_bf"], w["c_bb"], w["c_lam"]]
    out = jax.ShapeDtypeStruct((bsz, seq, C_WIDTH), F32)
    return pl.pallas_call(
        functools.partial(_rglru_body, ts=ts, nt=nt),
        grid=(bsz, nt),
        in_specs=[fm, fp, fn, bm, bp, bn] + [_const_spec(c.shape) for c in consts],
        out_specs=[fm, bm],
        out_shape=[out, out],
        scratch_shapes=[pltpu.VMEM((ts + 2 * SUBLANES, C_WIDTH), F32), pltpu.VMEM((ts + 2 * SUBLANES, C_WIDTH), F32),
                        pltpu.VMEM((4, ts, C_WIDTH), F32), pltpu.VMEM((SUBLANES, C_WIDTH), F32),
                        pltpu.VMEM((SUBLANES, C_WIDTH), F32)],
        compiler_params=_params(("parallel", "arbitrary")),
        name="rglru",
    )(cx, cx, cx, cx, cx, cx, *consts)


def _d_prep_body(x_ref, xp_ref, xn_ref, ab_ref, cw_ref, dconst_ref, q_ref, k_ref, v_ref, gb_ref, ext, *, ts, nt):
    i = pl.program_id(1)
    _fill_ext(ext, xp_ref, x_ref, xn_ref, i == 0, i == nt - 1, ts)
    y = _silu(_conv4(ext, cw_ref, ts))
    for h in range(D_HEADS):
        for j, ref in ((0, q_ref), (1, k_ref)):
            t = y[:, j * D_WIDTH + h * LANES:j * D_WIDTH + (h + 1) * LANES]
            ref[0, :, h * LANES:(h + 1) * LANES] = t * lax.rsqrt(jnp.sum(t * t, axis=-1, keepdims=True) + NORM_EPS)
    v_ref[0] = y[:, 2 * D_WIDTH:]
    ab = ab_ref[0]
    lane = lax.broadcasted_iota(jnp.int32, ab.shape, 1)
    g = -jnp.exp(dconst_ref[0:1, :]) * _softplus(ab + dconst_ref[1:2, :])
    gb_ref[0] = jnp.where(lane < 2 * D_HEADS, g, _sigmoid(ab))


def _d_prep(dqkv, dab, w, *, ts=256):
    bsz, seq, width = dqkv.shape
    ts = min(ts, seq)
    nt = seq // ts
    main, prev, nxt = _halo_specs(ts, width, nt, False)
    tok = lambda wd: pl.BlockSpec((1, ts, wd), lambda b, i: (b, i, 0))
    o512 = jax.ShapeDtypeStruct((bsz, seq, D_WIDTH), F32)
    return pl.pallas_call(
        functools.partial(_d_prep_body, ts=ts, nt=nt),
        grid=(bsz, nt),
        in_specs=[main, prev, nxt, tok(LANES), _const_spec(w["d_conv_w"].shape), _const_spec(w["d_const"].shape)],
        out_specs=[tok(D_WIDTH), tok(D_WIDTH), tok(D_WIDTH), tok(LANES)],
        out_shape=[o512, o512, o512, jax.ShapeDtypeStruct((bsz, seq, LANES), F32)],
        scratch_shapes=[pltpu.VMEM((ts + 2 * SUBLANES, width), F32)],
        compiler_params=_params(("parallel", "parallel")),
        name="d_prep",
    )(dqkv, dqkv, dqkv, dab, w["d_conv_w"], w["d_const"])


def _d_chunk_body(qf_ref, kf_ref, vf_ref, gf_ref, qb_ref, kb_ref, vb_ref, gbk_ref,
                  of_ref, ob_ref, s_ref, gc_s, u_s, wq_s, ak_s, eg_s, *, ts):
    i = pl.program_id(1)

    @pl.when(i == 0)
    def _():
        s_ref[...] = jnp.zeros_like(s_ref)

    c = D_CHUNK
    nc = ts // c
    scale = D_HEAD_DIM ** -0.5
    row_t = lax.broadcasted_iota(jnp.int32, (ts, ts), 0)
    col_t = lax.broadcasted_iota(jnp.int32, (ts, ts), 1)
    same = (row_t // c) == (col_t // c)
    ri = lax.broadcasted_iota(jnp.int32, (c, c), 0)
    ci = lax.broadcasted_iota(jnp.int32, (c, c), 1)
    lane = lax.broadcasted_iota(jnp.int32, (c, LANES), 1)
    ones3 = jnp.where(lane < 3, 1.0, 0.0).astype(BF16)

    dirs = ((0, qf_ref, kf_ref, vf_ref, gf_ref, of_ref), (1, qb_ref, kb_ref, vb_ref, gbk_ref, ob_ref))
    for d, q_ref, k_ref, v_ref, g_ref, o_ref in dirs:
        rev = d == 1
        before = (col_t >= row_t) if rev else (col_t <= row_t)
        tri = jnp.where(same & before, 1.0, 0.0).astype(BF16)
        p1, p2, p3 = _split3(g_ref[0])
        gc_s[d] = _dot(tri, p1) + _dot(tri, p2) + _dot(tri, p3)

    group = 4

    def prepare(gstep, carry):
        chains = []
        for d, q_ref, k_ref, v_ref, g_ref, o_ref in dirs:
            rev = d == 1
            for gi in range(group):
                cc = gstep * group + gi
                r0 = pl.multiple_of(cc * c, c)
                gcb = gc_s[d, pl.ds(r0, c), :]
                gbb = g_ref[0, pl.ds(r0, c), :]
                for h in range(D_HEADS):
                    sl = slice(h * LANES, (h + 1) * LANES)
                    col = d * D_HEADS + h
                    ch = dict(slot=(d * nc + cc) * D_HEADS + h,
                              incl=(ci >= ri) if rev else (ci <= ri), strict=(ci > ri) if rev else (ci < ri))
                    gcol = jnp.sum(jnp.where(lane == col, gcb, 0.0), axis=-1, keepdims=True)
                    ch["beta"] = jnp.sum(jnp.where(lane == 2 * D_HEADS + col, gbb, 0.0), axis=-1, keepdims=True)
                    ch["gcol"] = gcol
                    ch["gtot"] = gcol[0:1, :] if rev else gcol[c - 1:c, :]
                    ch["q"] = q_ref[0, pl.ds(r0, c), sl]
                    ch["k"] = k_ref[0, pl.ds(r0, c), sl]
                    ch["v"] = v_ref[0, pl.ds(r0, c), sl]
                    chains.append(ch)
        for ch in chains:
            g1, g2, g3 = (p.astype(F32) for p in _split3(ch["gcol"]))
            pieces = jnp.where(lane == 0, g1, jnp.where(lane == 1, g2, jnp.where(lane == 2, g3, 0.0)))
            ch["grow"] = _dot_nt(ones3, pieces.astype(BF16))
            ch["kbeta"] = ch["k"] * ch["beta"]
            kb16 = ch["k"].astype(BF16)
            ch["kk"] = _dot_nt(ch["kbeta"].astype(BF16), kb16)
            ch["qk"] = _dot_nt((ch["q"] * scale).astype(BF16), kb16)
        for ch in chains:
            decay = jnp.exp(jnp.where(ch["incl"], ch["gcol"] - ch["grow"], NEG_INF))
            lmat = jnp.where(ch["strict"], ch["kk"] * decay, 0.0)
            ch["attn"] = jnp.where(ch["incl"], ch["qk"] * decay, 0.0).astype(BF16)
            ch["nmat"] = -lmat
            x16 = lmat.astype(BF16)
            ch["xpow"] = _dot(x16, x16)
        for it in range(5):
            for ch in chains:
                x16 = ch["xpow"].astype(BF16)
                if it < 4:
                    both = _dot(jnp.concatenate([ch["nmat"].astype(BF16), x16], axis=0), x16)
                    ch["nmat"] = ch["nmat"] + ch["xpow"] + both[0:c]
                    ch["xpow"] = both[c:]
                else:
                    ch["nmat"] = ch["nmat"] + ch["xpow"] + _dot(ch["nmat"].astype(BF16), x16)
        for ch in chains:
            n16 = ch["nmat"].astype(BF16)
            vb = ch["v"] * ch["beta"]
            wk = ch["kbeta"] * jnp.exp(ch["gcol"])
            slot = ch["slot"]
            solved = _dot(n16, jnp.concatenate([vb.astype(BF16), wk.astype(BF16)], axis=1))
            u_s[slot] = vb + solved[:, 0:D_HEAD_DIM]
            wq_s[slot, 0:c] = (wk + solved[:, D_HEAD_DIM:]).astype(BF16)
            wq_s[slot, c:] = (ch["q"] * (scale * jnp.exp(ch["gcol"]))).astype(BF16)
            ak_s[slot, 0:c] = ch["attn"]
            ak_s[slot, c:] = (ch["k"] * jnp.exp(ch["gtot"] - ch["gcol"])).T.astype(BF16)
            eg_s[slot] = jnp.broadcast_to(jnp.exp(ch["gtot"]), (1, LANES))
        return carry

    lax.fori_loop(0, nc // group, prepare, 0)

    def advance(step, carry):
        chains = []
        for d, q_ref, k_ref, v_ref, g_ref, o_ref in dirs:
            cc = (nc - 1 - step) if d == 1 else step
            for h in range(D_HEADS):
                chains.append(dict(slot=(d * nc + cc) * D_HEADS + h, col=d * D_HEADS + h, o_ref=o_ref,
                                   r0=pl.multiple_of(cc * c, c), sl=slice(h * LANES, (h + 1) * LANES)))
        for ch in chains:
            ch["state"] = s_ref[ch["col"]]
            ch["ws_qs"] = _dot(wq_s[ch["slot"]], ch["state"].astype(BF16))
        for ch in chains:
            vn16 = (u_s[ch["slot"]] - ch["ws_qs"][0:c]).astype(BF16)
            ch["av_kv"] = _dot(ak_s[ch["slot"]], vn16)
        for ch in chains:
            ch["o_ref"][0, pl.ds(ch["r0"], c), ch["sl"]] = ch["ws_qs"][c:] + ch["av_kv"][0:c]
            s_ref[ch["col"]] = ch["state"] * eg_s[ch["slot"]] + ch["av_kv"][c:]
        return carry

    lax.fori_loop(0, nc, advance, 0)


def _d_chunk(q, k, v, gb, *, ts=256):
    bsz, seq, _ = q.shape
    ts = min(ts, seq)
    nt = seq // ts
    fwd = lambda wd: pl.BlockSpec((1, ts, wd), lambda b, i: (b, i, 0))
    bwd = lambda wd: pl.BlockSpec((1, ts, wd), lambda b, i: (b, nt - 1 - i, 0))
    out = jax.ShapeDtypeStruct((bsz, seq, D_WIDTH), F32)
    nslot = 2 * (ts // D_CHUNK) * D_HEADS
    return pl.pallas_call(
        functools.partial(_d_chunk_body, ts=ts),
        grid=(bsz, nt),
        in_specs=[fwd(D_WIDTH), fwd(D_WIDTH), fwd(D_WIDTH), fwd(LANES),
                  bwd(D_WIDTH), bwd(D_WIDTH), bwd(D_WIDTH), bwd(LANES)],
        out_specs=[fwd(D_WIDTH), bwd(D_WIDTH)],
        out_shape=[out, out],
        scratch_shapes=[pltpu.VMEM((2 * D_HEADS, D_HEAD_DIM, D_HEAD_DIM), F32), pltpu.VMEM((2, ts, LANES), F32),
                        pltpu.VMEM((nslot, D_CHUNK, D_HEAD_DIM), F32),
                        pltpu.VMEM((nslot, 2 * D_CHUNK, D_HEAD_DIM), BF16),
                        pltpu.VMEM((nslot, D_CHUNK + D_HEAD_DIM, D_CHUNK), BF16),
                        pltpu.VMEM((nslot, 1, LANES), F32)],
        compiler_params=_params(("parallel", "arbitrary")),
        name="d_chunk",
    )(q, k, v, gb, q, k, v, gb)


def _gelu_tanh(x):
    return 0.5 * x * (1.0 + jnp.tanh(np.sqrt(2.0 / np.pi).astype(np.float32) * (x + 0.044715 * (x * x * x))))


def _merge_body(x_ref, o1_ref, l1_ref, o4_ref, l4_ref, o16_ref, l16_ref, ob_ref,
                hf_ref, hb_ref, cg_ref, df_ref, db_ref, dz_ref, gate_ref,
                wbr_ref, wout_ref, ong_ref, y_ref):
    slabs = []
    for c in range(A_WIDTH // LANES):
        l1, l4, l16 = l1_ref[0, c], l4_ref[0, c], l16_ref[0, c]
        m = jnp.maximum(jnp.maximum(l1, l4), l16)
        e1, e4, e16 = jnp.exp2(l1 - m), jnp.exp2(l4 - m), jnp.exp2(l16 - m)
        slabs.append((e1 * o1_ref[0, c] + e4 * o4_ref[0, c] + e16 * o16_ref[0, c]) / (e1 + e4 + e16))
    oa = jnp.concatenate(slabs, axis=1)
    oc = (hf_ref[0] + hb_ref[0]) * _gelu_tanh(cg_ref[0].astype(F32))
    od_sum = df_ref[0] + db_ref[0]
    z = dz_ref[0].astype(F32)
    heads = []
    for h in range(D_HEADS):
        sl = slice(h * LANES, (h + 1) * LANES)
        heads.append(_rms_rows(od_sum[:, sl], ong_ref[...]) * _silu(z[:, sl]))
    od = jnp.concatenate(heads, axis=1)

    def gate(j):
        return _sigmoid(gate_ref[0, :, j * D_MODEL:(j + 1) * D_MODEL]).astype(F32)

    r_a, r_b, r_c = A_WIDTH, A_WIDTH + B_HEADS * LANES, A_WIDTH + B_HEADS * LANES + C_WIDTH
    merged = gate(0) * _dot(oa.astype(BF16), wbr_ref[0:r_a, :])
    merged = merged + gate(1) * _dot(ob_ref[0], wbr_ref[r_a:r_b, :])
    merged = merged + gate(2) * _dot(oc.astype(BF16), wbr_ref[r_b:r_c, :])
    merged = merged + gate(3) * _dot(od.astype(BF16), wbr_ref[r_c:, :])
    y_ref[0] = x_ref[0] + _dot(merged.astype(BF16), wout_ref[...])


def _merge(x, a_outs, ob, hf, hb, cg, df, db, dz, gates, w, *, tm=256):
    bsz, seq, _ = x.shape
    tok = lambda wd: pl.BlockSpec((1, tm, wd), lambda b, i: (b, i, 0))
    consts = [w["w_branch"], w["w_out"], w["d_on_g"]]
    ins = [x]
    specs = [tok(D_MODEL)]
    slab = pl.BlockSpec((1, A_WIDTH // LANES, tm, LANES), lambda b, i: (b, 0, i, 0))
    for o, lse in a_outs:
        ins += [o, lse]
        specs += [slab, slab]
    ins += [ob, hf, hb, cg, df, db, dz, gates]
    specs += [tok(B_HEADS * LANES), tok(C_WIDTH), tok(C_WIDTH), tok(C_WIDTH), tok(D_WIDTH), tok(D_WIDTH),
              tok(D_WIDTH), tok(4 * D_MODEL)]
    return pl.pallas_call(
        _merge_body,
        grid=(bsz, seq // tm),
        in_specs=specs + [_const_spec(c.shape) for c in consts],
        out_specs=tok(D_MODEL),
        out_shape=jax.ShapeDtypeStruct(x.shape, F32),
        compiler_params=_params(("parallel", "parallel")),
        name="merge",
    )(*ins, *consts)


FF_TILE = 256


def _ffn_body(x_ref, ln_ref, wgu_ref, wdn_ref, y_ref):
    x = x_ref[0]
    xn = _rms_rows(x, ln_ref[...]).astype(BF16)
    acc = x
    for j in range(FF_DIM // FF_TILE):
        g = _dot(xn, wgu_ref[:, j * FF_TILE:(j + 1) * FF_TILE])
        u = _dot(xn, wgu_ref[:, FF_DIM + j * FF_TILE:FF_DIM + (j + 1) * FF_TILE])
        acc = acc + _dot((_silu(g) * u).astype(BF16), wdn_ref[j * FF_TILE:(j + 1) * FF_TILE, :])
    y_ref[0] = acc


def _ffn(x, w, *, tm=512):
    bsz, seq, _ = x.shape
    tok = pl.BlockSpec((1, tm, D_MODEL), lambda b, i: (b, i, 0))
    consts = [w["ln2"], w["w_up"], w["w_down"]]
    return pl.pallas_call(
        _ffn_body,
        grid=(bsz, seq // tm),
        in_specs=[tok] + [_const_spec(c.shape) for c in consts],
        out_specs=tok,
        out_shape=jax.ShapeDtypeStruct(x.shape, F32),
        compiler_params=_params(("parallel", "parallel")),
        name="ffn",
    )(x, *consts)


def _block_diag(wblocks):
    g, i, o = wblocks.shape
    eye = jnp.eye(g, dtype=wblocks.dtype)
    return (eye[:, None, :, None] * wblocks[:, :, None, :]).reshape(g * i, g * o)


def _rope_offsets(consts, rows):
    ang = jnp.arange(rows, dtype=F32)[:, None] * consts[0][None, :]
    return jnp.stack([jnp.cos(ang), jnp.sin(ang)], axis=0)


def _rope_consts(head_period, rope_start, rope_dim):
    lane = np.arange(LANES) % head_period
    j = lane - rope_start
    in_rope = (j >= 0) & (j < rope_dim)
    half = rope_dim // 2
    inv_freq = 1.0 / (ROPE_THETA ** (jnp.arange(0, rope_dim, 2, dtype=F32) / rope_dim))
    inv_lane = jnp.where(jnp.asarray(in_rope), inv_freq[np.where(in_rope, j % half, 0)], 0.0)
    out = jnp.zeros((SUBLANES, LANES), F32)
    out = out.at[0].set(inv_lane)
    out = out.at[1].set(jnp.asarray(np.where(in_rope & (j < half), -1.0, 0.0), F32))
    out = out.at[2].set(jnp.asarray(np.where(in_rope & (j >= half), 1.0, 0.0), F32))
    return out


def _pad_heads(wmat, heads, dim):
    r = wmat.shape[0]
    w3 = wmat.reshape(r, heads, dim)
    return jnp.pad(w3, ((0, 0), (0, 0), (0, LANES - dim))).reshape(r, heads * LANES)


def _pad_lane_vec(vec, dim):
    return jnp.pad(vec, (0, LANES - dim)).reshape(1, LANES)


def _layer_weights(l, ln1_g, w_in, a_qn_g, a_kn_g, b_qa_g, b_wuq, b_kva_g, b_wukv, b_qn_g, b_kn_g,
                   c_conv_w, c_conv_b, c_wr, c_br, c_wi, c_bi, c_lam, d_conv_w, d_a_log, d_dt_bias, d_on_g,
                   w_branch, w_out, ln2_g, w_up, w_down):
    wi = w_in[l]
    row = lambda v: v.reshape(1, -1)
    w = {}
    w["ln1"] = row(ln1_g[l])
    w["wa"] = wi[:, O_AQ:O_BCQ].astype(BF16)
    zeros = lambda n: jnp.zeros((D_MODEL, n), F32)
    w["wb"] = jnp.concatenate([wi[:, O_BCQ:O_BKR], zeros(B_NOPE_DIM), wi[:, O_BKR:O_CX],
                               zeros(LANES - B_QK_DIM)], axis=1).astype(BF16)
    w["wc"] = wi[:, O_CX:O_DQ].astype(BF16)
    w["wd"] = wi[:, O_DQ:O_DA].astype(BF16)
    w["wab"] = jnp.concatenate([wi[:, O_DA:O_GATE], zeros(LANES - 4 * D_HEADS)], axis=1).astype(BF16)
    w["wg"] = wi[:, O_GATE:O_END].astype(BF16)
    w["aqg"] = row(jnp.tile(a_qn_g[l], A_HEADS))
    w["akg"] = row(jnp.tile(a_kn_g[l], A_HEADS))
    w["arope"] = _rope_consts(A_HEAD_DIM, 0, A_ROPE_DIM)
    w["aoff"] = _rope_offsets(w["arope"], IN_TILE)
    w["qag"] = row(b_qa_g[l])
    w["wuq"] = _pad_heads(b_wuq[l], B_HEADS, B_QK_DIM).astype(BF16)
    w["kvag"] = row(b_kva_g[l])
    wukv = b_wukv[l].reshape(B_KV_RANK, B_HEADS, B_NOPE_DIM + B_V_DIM)
    w["wukv"] = jnp.concatenate([
        _pad_heads(wukv[:, :, :B_NOPE_DIM].reshape(B_KV_RANK, -1), B_HEADS, B_NOPE_DIM),
        _pad_heads(wukv[:, :, B_NOPE_DIM:].reshape(B_KV_RANK, -1), B_HEADS, B_V_DIM)], axis=1).astype(BF16)
    w["bqg"] = _pad_lane_vec(b_qn_g[l], B_QK_DIM)
    w["bkg"] = _pad_lane_vec(b_kn_g[l], B_QK_DIM)
    w["brope"] = _rope_consts(LANES, B_NOPE_DIM, B_ROPE_DIM)
    w["boff"] = _rope_offsets(w["brope"], IN_TILE)

    w["c_conv_w"] = jnp.pad(c_conv_w[l], ((0, SUBLANES - 4), (0, 0)))
    w["c_conv_b"] = row(c_conv_b[l])
    w["c_wf"] = jnp.concatenate([_block_diag(c_wr[l, 0]), _block_diag(c_wi[l, 0])], axis=1).astype(BF16)
    w["c_wb"] = jnp.concatenate([_block_diag(c_wr[l, 1]), _block_diag(c_wi[l, 1])], axis=1).astype(BF16)
    w["c_bf"] = row(jnp.concatenate([c_br[l, 0], c_bi[l, 0]]))
    w["c_bb"] = row(jnp.concatenate([c_br[l, 1], c_bi[l, 1]]))
    w["c_lam"] = jnp.pad(c_lam[l], ((0, SUBLANES - 2), (0, 0)))

    w["d_conv_w"] = jnp.pad(d_conv_w[l], ((0, SUBLANES - 4), (0, 0)))
    dconst = jnp.zeros((SUBLANES, LANES), F32)
    dconst = dconst.at[0, 0:2 * D_HEADS].set(d_a_log[l].reshape(-1))
    dconst = dconst.at[1, 0:2 * D_HEADS].set(d_dt_bias[l].reshape(-1))
    w["d_const"] = dconst
    w["d_on_g"] = row(d_on_g[l])

    wbr = w_branch[l]
    wbr_b = jnp.pad(wbr[A_WIDTH:A_WIDTH + B_HEADS * B_V_DIM].reshape(B_HEADS, B_V_DIM, D_MODEL),
                    ((0, 0), (0, LANES - B_V_DIM), (0, 0))).reshape(B_HEADS * LANES, D_MODEL)
    w["w_branch"] = jnp.concatenate([wbr[:A_WIDTH], wbr_b, wbr[A_WIDTH + B_HEADS * B_V_DIM:]], axis=0).astype(BF16)
    w["w_out"] = w_out[l].astype(BF16)
    w["ln2"] = row(ln2_g[l])
    w["w_up"] = w_up[l].astype(BF16)
    w["w_down"] = w_down[l].astype(BF16)
    return w


def _layer(x, w):
    aqs, aks, avs, bq, bk, bv, cx, cg, dqkv, dz, dab, gates = _in_proj(x, w)
    a_outs = [_a_attn(q, k, v, dil) for q, k, v, dil in zip(aqs, aks, avs, A_DILATIONS)]
    ob = _b_flash(bq, bk, bv)
    hf, hb = _rglru(cx, w)
    dq, dk, dv, dgb = _d_prep(dqkv, dab, w)
    df, db = _d_chunk(dq, dk, dv, dgb)
    x = _merge(x, a_outs, ob, hf, hb, cg, df, db, dz, gates, w)
    return _ffn(x, w)


def kernel(x_prompt, x_sample, ln1_g, w_in, a_qn_g, a_kn_g, b_qa_g, b_wuq, b_kva_g, b_wukv, b_qn_g, b_kn_g,
           c_conv_w, c_conv_b, c_wr, c_br, c_wi, c_bi, c_lam, d_conv_w, d_a_log, d_dt_bias, d_on_g,
           w_branch, w_out, ln2_g, w_up, w_down):
    depth = ln1_g.shape[0]
    weights = [_layer_weights(l, ln1_g, w_in, a_qn_g, a_kn_g, b_qa_g, b_wuq, b_kva_g, b_wukv, b_qn_g, b_kn_g,
                              c_conv_w, c_conv_b, c_wr, c_br, c_wi, c_bi, c_lam, d_conv_w, d_a_log, d_dt_bias,
                              d_on_g, w_branch, w_out, ln2_g, w_up, w_down) for l in range(depth)]

    def trunk(x):
        for w in weights:
            x = _layer(x, w)
        return x

    return (trunk(x_prompt), trunk(x_sample))
```

```python
import functools

import jax
import jax.numpy as jnp
import numpy as np
from jax import lax
from jax.experimental import pallas as pl
from jax.experimental.pallas import tpu as pltpu

F32 = jnp.float32
BF16 = jnp.bfloat16

D_MODEL = 1024
ROPE_THETA = 500000.0
NORM_EPS = 1e-6
NEG_INF = -1e30
LOG2_E = 1.4426950408889634

A_HEADS = 8
A_HEAD_DIM = 64
A_ROPE_DIM = 16
A_DILATIONS = (1, 4, 16)
A_HALF = 64
A_WIDTH = A_HEADS * A_HEAD_DIM
A_QBLK = 128
A_SLAB_GROUPS = 4
A_TOKENS = 2048

B_HEADS = 8
B_NOPE_DIM = 64
B_ROPE_DIM = 32
B_QK_DIM = B_NOPE_DIM + B_ROPE_DIM
B_V_DIM = 64
B_Q_RANK = 256
B_KV_RANK = 128

C_WIDTH = 512
C_BLOCKS = 8
C_BLOCK_DIM = 64
C_GATE_C = 8.0

D_HEADS = 4
D_HEAD_DIM = 128
D_WIDTH = 512
D_CHUNK = 64

FF_DIM = 2816
LANES = 128
SUBLANES = 8
VMEM_LIMIT = 56 * 1024 * 1024
IN_TILE = 256

_OFF = np.cumsum([0, 512, 512, 512, 256, 128, 32, 512, 512, 512, 512, 512, 512, 8, 8, 4096])
(O_AQ, O_AK, O_AV, O_BCQ, O_BCKV, O_BKR, O_CX, O_CG, O_DQ, O_DK, O_DV, O_DZ, O_DA, O_DB, O_GATE, O_END) = [int(v) for v in _OFF]


def _const_spec(shape):
    nd = len(shape)
    return pl.BlockSpec(shape, lambda *_: (0,) * nd, pipeline_mode=pl.Buffered(1))


def _params(sem):
    return pltpu.CompilerParams(dimension_semantics=sem, vmem_limit_bytes=VMEM_LIMIT)


def _dot(a, b):
    return jnp.dot(a, b, preferred_element_type=F32)


def _dot_nt(a, b):
    return lax.dot_general(a, b, (((1,), (1,)), ((), ())), preferred_element_type=F32)


def _dot_tn(a, b):
    return lax.dot_general(a, b, (((0,), (0,)), ((), ())), preferred_element_type=F32)


def _rms_rows(x, g):
    return x * lax.rsqrt(jnp.mean(x * x, axis=-1, keepdims=True) + NORM_EPS) * g


def _split3(x):
    p1 = x.astype(BF16)
    r1 = x - p1.astype(F32)
    p2 = r1.astype(BF16)
    p3 = (r1 - p2.astype(F32)).astype(BF16)
    return p1, p2, p3


def _sigmoid(x):
    return 0.5 * jnp.tanh(0.5 * x) + 0.5


def _silu(x):
    return x * _sigmoid(x)


def _softplus(x):
    return jnp.maximum(x, 0.0) + jnp.log(1.0 + jnp.exp(-jnp.abs(x)))


def _rope_tables(pos0, consts, off_ref):
    base = pos0.astype(F32) * consts[0:1, :]
    cb, sb = jnp.cos(base), jnp.sin(base)
    co, so = off_ref[0], off_ref[1]
    c = cb * co - sb * so
    s = sb * co + cb * so
    return c, s * consts[1:2, :], s * consts[2:3, :]


def _in_proj_body(x_ref, ln_ref, wa_ref, wb_ref, wc_ref, wd_ref, wab_ref, wg_ref,
                  aqg_ref, akg_ref, arope_ref, aoff_ref,
                  qag_ref, wuq_ref, kvag_ref, wukv_ref, bqg_ref, bkg_ref, brope_ref, boff_ref,
                  aq1_ref, aq4_ref, aq16_ref, ak1_ref, ak4_ref, ak16_ref, av1_ref, av4_ref, av16_ref,
                  bq_ref, bk_ref, bv_ref,
                  cx_ref, cg_ref, dqkv_ref, dz_ref, dab_ref, gate_ref, perm_q, perm_k, perm_v, *, tm):
    aq_refs = (aq1_ref, aq4_ref, aq16_ref)
    ak_refs = (ak1_ref, ak4_ref, ak16_ref)
    av_refs = (av1_ref, av4_ref, av16_ref)
    pos0 = pl.program_id(1) * tm
    xn = _rms_rows(x_ref[0], ln_ref[...]).astype(BF16)

    def xdot(wmat):
        return _dot(xn, wmat)

    hb = xdot(wb_ref[...])
    cq = _rms_rows(hb[:, 0:B_Q_RANK], qag_ref[...]).astype(BF16)
    ckv = _rms_rows(hb[:, B_Q_RANK:B_Q_RANK + B_KV_RANK], kvag_ref[...]).astype(BF16)
    kr = hb[:, B_Q_RANK + B_KV_RANK:]
    cb, sb1, sb2 = _rope_tables(pos0, brope_ref[...], boff_ref)
    lane = lax.broadcasted_iota(jnp.int32, (tm, LANES), 1)
    ones_col = jnp.where(lane == B_V_DIM, 1.0, 0.0)

    def b_norm_rope(t, g):
        ms = jnp.sum(t * t, axis=-1, keepdims=True) * (1.0 / B_QK_DIM)
        y = t * lax.rsqrt(ms + NORM_EPS) * g
        return y * cb + pltpu.roll(y, LANES - B_ROPE_DIM // 2, 1) * sb1 + pltpu.roll(y, B_ROPE_DIM // 2, 1) * sb2

    for hp in range(B_HEADS // 2):
        pair = slice(2 * hp * LANES, (2 * hp + 2) * LANES)
        q2 = _dot(cq, wuq_ref[:, pair])
        k2 = _dot(ckv, wukv_ref[:, pair])
        v2 = _dot(ckv, wukv_ref[:, B_HEADS * LANES + 2 * hp * LANES:B_HEADS * LANES + (2 * hp + 2) * LANES])
        for j in range(2):
            h = 2 * hp + j
            sl = slice(j * LANES, (j + 1) * LANES)
            bq_ref[0, h] = (b_norm_rope(q2[:, sl], bqg_ref[...]) * (B_QK_DIM ** -0.5 * LOG2_E)).astype(BF16)
            bk_ref[0, h] = b_norm_rope(k2[:, sl] + kr, bkg_ref[...]).astype(BF16)
            bv_ref[0, h] = (v2[:, sl] + ones_col).astype(BF16)

    ca, sa1, sa2 = _rope_tables(pos0, arope_ref[...], aoff_ref)
    ca = jnp.concatenate([ca] * 4, axis=1)
    sa1 = jnp.concatenate([sa1] * 4, axis=1)
    sa2 = jnp.concatenate([sa2] * 4, axis=1)

    lane_a = lax.broadcasted_iota(jnp.int32, (tm, LANES), 1)
    first_head = lane_a < A_HEAD_DIM

    def a_norm_rope(t, g):
        sq = t * t
        slabs = []
        for c in range(A_WIDTH // LANES):
            s2 = sq[:, c * LANES:(c + 1) * LANES]
            sum_a = jnp.sum(jnp.where(first_head, s2, 0.0), axis=-1, keepdims=True)
            sum_b = jnp.sum(jnp.where(first_head, 0.0, s2), axis=-1, keepdims=True)
            slabs.append(jnp.where(first_head, sum_a, sum_b))
        ms = jnp.concatenate(slabs, axis=1) * (1.0 / A_HEAD_DIM)
        y = t * lax.rsqrt(ms + NORM_EPS) * g
        return y * ca + pltpu.roll(y, A_WIDTH - A_ROPE_DIM // 2, 1) * sa1 + pltpu.roll(y, A_ROPE_DIM // 2, 1) * sa2

    def emit(y, refs, perm):
        refs[0][0] = y.astype(BF16)
        nslab = A_WIDTH // LANES
        for c in range(nslab):
            perm[c] = y[:, c * LANES:(c + 1) * LANES]
        for ref, d in zip(refs[1:], A_DILATIONS[1:]):
            for r in range(d):
                for c in range(nslab):
                    col = r * A_WIDTH + c * LANES
                    ref[0, :, col:col + LANES] = perm[c, pl.ds(r, tm // d, stride=d), :].astype(BF16)

    q = xdot(wa_ref[:, 0:A_WIDTH])
    emit(a_norm_rope(q, aqg_ref[...]) * (A_HEAD_DIM ** -0.5 * LOG2_E), aq_refs, perm_q)
    k = xdot(wa_ref[:, A_WIDTH:2 * A_WIDTH])
    emit(a_norm_rope(k, akg_ref[...]), ak_refs, perm_k)
    emit(xdot(wa_ref[:, 2 * A_WIDTH:3 * A_WIDTH]), av_refs, perm_v)

    hc = xdot(wc_ref[...])
    cx_ref[0] = hc[:, 0:C_WIDTH]
    cg_ref[0] = hc[:, C_WIDTH:].astype(BF16)
    for j in range(3):
        dqkv_ref[0, :, j * D_WIDTH:(j + 1) * D_WIDTH] = xdot(wd_ref[:, j * D_WIDTH:(j + 1) * D_WIDTH])
    dz_ref[0] = xdot(wd_ref[:, 3 * D_WIDTH:]).astype(BF16)
    dab_ref[0] = xdot(wab_ref[...])
    for j in range(4 * D_MODEL // 512):
        gate_ref[0, :, j * 512:(j + 1) * 512] = xdot(wg_ref[:, j * 512:(j + 1) * 512]).astype(BF16)


def _in_proj(x, w, *, tm=IN_TILE):
    bsz, seq, _ = x.shape
    grid = (bsz, seq // tm)
    tok = lambda width: pl.BlockSpec((1, tm, width), lambda b, i: (b, i, 0))
    head = pl.BlockSpec((1, B_HEADS, tm, LANES), lambda b, i: (b, 0, i, 0))
    consts = [w["ln1"], w["wa"], w["wb"], w["wc"], w["wd"], w["wab"], w["wg"],
              w["aqg"], w["akg"], w["arope"], w["aoff"],
              w["qag"], w["wuq"], w["kvag"], w["wukv"], w["bqg"], w["bkg"], w["brope"], w["boff"]]
    assert tm == IN_TILE
    tokf = lambda width, dt: jax.ShapeDtypeStruct((bsz, seq, width), dt)
    headf = jax.ShapeDtypeStruct((bsz, B_HEADS, seq, LANES), BF16)
    a_specs = [pl.BlockSpec((1, tm // d, d * A_WIDTH), lambda b, i: (b, i, 0)) for d in A_DILATIONS] * 3
    a_shapes = [jax.ShapeDtypeStruct((bsz, seq // d, d * A_WIDTH), BF16) for d in A_DILATIONS] * 3
    outs = pl.pallas_call(
        functools.partial(_in_proj_body, tm=tm),
        grid=grid,
        in_specs=[tok(D_MODEL)] + [_const_spec(c.shape) for c in consts],
        out_specs=a_specs + [head, head, head,
                             tok(C_WIDTH), tok(C_WIDTH), tok(3 * D_WIDTH), tok(D_WIDTH), tok(LANES), tok(4 * D_MODEL)],
        out_shape=a_shapes + [headf, headf, headf,
                              tokf(C_WIDTH, F32), tokf(C_WIDTH, BF16), tokf(3 * D_WIDTH, F32), tokf(D_WIDTH, BF16),
                              tokf(LANES, F32), tokf(4 * D_MODEL, BF16)],
        scratch_shapes=[pltpu.VMEM((A_WIDTH // LANES, tm, LANES), F32)] * 3,
        compiler_params=_params(("parallel", "parallel")),
        name="in_proj",
    )(x, *consts)
    nd = len(A_DILATIONS)
    return (outs[0:nd], outs[nd:2 * nd], outs[2 * nd:3 * nd]) + tuple(outs[3 * nd:])


def _a_attn_body(q_ref, k_ref, kp_ref, kn_ref, v_ref, vp_ref, vn_ref, o_ref, lse_ref, *, dil, lq, sub_len):
    blk = A_QBLK
    half = A_HALF
    nsub = lq // blk
    npiece = lq // half
    i = pl.program_id(1)
    kw = blk + 2 * half
    rel = lax.broadcasted_iota(jnp.int32, (blk, kw), 1) - half - lax.broadcasted_iota(jnp.int32, (blk, kw), 0)
    band = jnp.abs(rel) <= A_HALF
    kcol = lax.broadcasted_iota(jnp.int32, (blk, kw), 1)
    lane = lax.broadcasted_iota(jnp.int32, (blk, LANES), 1)
    first = lane < A_HEAD_DIM

    def window(main_ref, prev_ref, next_ref, j, cols):
        parts = []
        first_piece = j * (blk // half) - 1
        for t in range(first_piece, first_piece + kw // half):
            if t < 0:
                parts.append(prev_ref[0, :, cols])
            elif t >= npiece:
                parts.append(next_ref[0, :, cols])
            else:
                parts.append(main_ref[0, t * half:(t + 1) * half, cols])
        return jnp.concatenate(parts, axis=0)

    valids = []
    for j in range(nsub):
        kpos = i * lq + j * blk - half + kcol
        valids.append(band & (kpos >= 0) & (kpos < sub_len))

    nslab = A_HEADS // 2
    per_group = nslab // A_SLAB_GROUPS

    def slabs_of(g):
        return range(g * per_group, (g + 1) * per_group)

    def cols_of(r, g):
        return [slice(r * A_WIDTH + hp * LANES, r * A_WIDTH + (hp + 1) * LANES) for hp in slabs_of(g)]

    def stage_scores(j, r, g):
        scores = []
        for cl in cols_of(r, g):
            q2 = q_ref[0, j * blk:(j + 1) * blk, cl]
            k2 = window(k_ref, kp_ref, kn_ref, j, cl)
            zero = jnp.zeros_like(q2)
            scores.append([_dot_nt(qh, k2) for qh in (jnp.where(first, q2, zero), jnp.where(first, zero, q2))])
        return scores

    def stage_softmax(j, scores):
        masked = [[jnp.where(valids[j], s, NEG_INF) for s in pair] for pair in scores]
        maxes = [[jnp.max(s, axis=-1, keepdims=True) for s in pair] for pair in masked]
        probs = [[jnp.exp2(s - m).astype(BF16) for s, m in zip(ps, ms)] for ps, ms in zip(masked, maxes)]
        return probs, maxes

    ones_cols = jnp.ones((kw, LANES), BF16)

    def stage_values(j, r, g, probs):
        outs = []
        for pp, cl in zip(probs, cols_of(r, g)):
            v_ext = jnp.concatenate([window(v_ref, vp_ref, vn_ref, j, cl), ones_cols], axis=1)
            outs.append([_dot(p, v_ext) for p in pp])
        return outs

    def stage_store(j, r, g, outs, maxes):
        rows = pl.ds(j * blk * dil + r, blk, stride=dil) if dil > 1 else pl.ds(j * blk, blk)
        for hp, (oa, ob), (ma, mb) in zip(slabs_of(g), outs, maxes):
            za, zb = oa[:, LANES:], ob[:, LANES:]
            o_ref[0, hp, rows, :] = jnp.where(first, oa[:, :LANES] / za, ob[:, :LANES] / zb)
            lse_ref[0, hp, rows, :] = jnp.where(first, ma + jnp.log(za) * LOG2_E, mb + jnp.log(zb) * LOG2_E)

    problems = [(j, r, g) for j in range(nsub) for r in range(dil) for g in range(A_SLAB_GROUPS)]
    n = len(problems)
    sc, pr, st, ou = {}, {}, {}, {}
    for t in range(n + 3):
        if t < n:
            sc[t] = stage_scores(*problems[t])
        if 0 <= t - 1 < n:
            pr[t - 1], st[t - 1] = stage_softmax(problems[t - 1][0], sc.pop(t - 1))
        if 0 <= t - 2 < n:
            ou[t - 2] = stage_values(*problems[t - 2], pr.pop(t - 2))
        if 0 <= t - 3 < n:
            stage_store(*problems[t - 3], ou.pop(t - 3), st.pop(t - 3))


def _a_attn(qv, kv, vv, dil):
    bsz, sub_len, vwidth = qv.shape
    seq = sub_len * dil
    lq = A_TOKENS // dil
    nblk = sub_len // A_HALF
    per = lq // A_HALF
    main = pl.BlockSpec((1, lq, vwidth), lambda b, i: (b, i, 0))
    prev = pl.BlockSpec((1, A_HALF, vwidth), lambda b, i: (b, jnp.maximum(i * per - 1, 0), 0))
    nxt = pl.BlockSpec((1, A_HALF, vwidth), lambda b, i: (b, jnp.minimum((i + 1) * per, nblk - 1), 0))
    nslab = A_WIDTH // LANES
    out = pl.BlockSpec((1, nslab, A_TOKENS, LANES), lambda b, i: (b, 0, i, 0))
    oshape = jax.ShapeDtypeStruct((bsz, nslab, seq, LANES), F32)
    return pl.pallas_call(
        functools.partial(_a_attn_body, dil=dil, lq=lq, sub_len=sub_len),
        grid=(bsz, seq // A_TOKENS),
        in_specs=[main, main, prev, nxt, main, prev, nxt],
        out_specs=[out, out],
        out_shape=[oshape, oshape],
        compiler_params=_params(("parallel", "parallel")),
        name=f"a_attn_d{dil}",
    )(qv, kv, kv, kv, vv, vv, vv)


def _b_flash_body(q_ref, k_ref, v_ref, o_ref, *, tq, tk, seq, unroll, nq):
    nk = seq // tk
    lane = lax.broadcasted_iota(jnp.int32, (tq, LANES), 1)

    for t in range(nq):
        q = q_ref[0, 0, t * tq:(t + 1) * tq, :]

        def scores(c, q=q):
            return _dot_nt(q, k_ref[0, 0, pl.ds(pl.multiple_of(c * tk, tk), tk), :])

        def update(c, m, acc, s):
            m_new = jnp.maximum(m, jnp.max(s, axis=-1, keepdims=True))
            alpha = jnp.exp2(m - m_new)
            p = jnp.exp2((s - m_new).astype(BF16))
            acc = alpha * acc + _dot(p, v_ref[0, 0, pl.ds(pl.multiple_of(c * tk, tk), tk), :])
            return m_new, acc

        def chunk(c, carry, scores=scores):
            m, acc, s = carry
            s_next = scores(jnp.minimum(c + 1, nk - 1))
            m, acc = update(c, m, acc, s)
            return m, acc, s_next

        m = jnp.full((tq, 1), NEG_INF, F32)
        acc = jnp.zeros((tq, LANES), F32)
        if nk <= unroll:
            s = scores(0)
            for c in range(nk):
                s_next = scores(c + 1) if c + 1 < nk else None
                m, acc = update(c, m, acc, s)
                s = s_next
        else:
            _, acc, _ = lax.fori_loop(0, nk, chunk, (m, acc, scores(0)), unroll=unroll)
        denom = jnp.sum(jnp.where(lane == B_V_DIM, acc, 0.0), axis=-1, keepdims=True)
        o_ref[0, t * tq:(t + 1) * tq, :] = jnp.where(lane < B_V_DIM, acc / denom, 0.0).astype(BF16)


def _b_flash(q, k, v, *, tq=512, tk=512, unroll=32, nq=2):
    bsz, heads, seq, _ = q.shape
    tstep = tq * nq
    return pl.pallas_call(
        functools.partial(_b_flash_body, tq=tq, tk=tk, seq=seq, unroll=unroll, nq=nq),
        grid=(bsz, heads, seq // tstep),
        in_specs=[pl.BlockSpec((1, 1, tstep, LANES), lambda b, h, i: (b, h, i, 0)),
                  pl.BlockSpec((1, 1, seq, LANES), lambda b, h, i: (b, h, 0, 0)),
                  pl.BlockSpec((1, 1, seq, LANES), lambda b, h, i: (b, h, 0, 0))],
        out_specs=pl.BlockSpec((1, tstep, LANES), lambda b, h, i: (b, i, h)),
        out_shape=jax.ShapeDtypeStruct((bsz, seq, heads * LANES), BF16),
        compiler_params=_params(("parallel", "parallel", "parallel")),
        name="b_flash",
    )(q, k, v)


def _fill_ext(ext, prev_ref, main_ref, next_ref, is_first, is_last, ts):
    h = SUBLANES
    ext[0:h] = jnp.where(is_first, 0.0, prev_ref[0])
    ext[h:h + ts] = main_ref[0]
    ext[h + ts:] = jnp.where(is_last, 0.0, next_ref[0])


def _conv4(ext, w_ref, ts):
    h = SUBLANES
    out = ext[h - 2:h - 2 + ts] * w_ref[0:1, :]
    for j in range(1, 4):
        out = out + ext[h - 2 + j:h - 2 + j + ts] * w_ref[j:j + 1, :]
    return out


def _halo_specs(ts, width, nt, rev):
    per = ts // SUBLANES
    nblk = nt * per
    t = (lambda i: nt - 1 - i) if rev else (lambda i: i)
    main = pl.BlockSpec((1, ts, width), lambda b, i: (b, t(i), 0))
    prev = pl.BlockSpec((1, SUBLANES, width), lambda b, i: (b, jnp.maximum(t(i) * per - 1, 0), 0))
    nxt = pl.BlockSpec((1, SUBLANES, width), lambda b, i: (b, jnp.minimum((t(i) + 1) * per, nblk - 1), 0))
    return main, prev, nxt


def _scan_tiles(au_s, h_refs, carry_refs, ts):
    n = ts // SUBLANES
    row = lax.broadcasted_iota(jnp.int32, (SUBLANES, C_WIDTH), 0)

    def one(d, g, carry):
        reverse = d == 1
        gi = (n - 1 - g) if reverse else g
        r0 = pl.multiple_of(gi * SUBLANES, SUBLANES)
        a = au_s[2 * d, pl.ds(r0, SUBLANES), :]
        b = au_s[2 * d + 1, pl.ds(r0, SUBLANES), :]
        for sh in (1, 2, 4):
            if reverse:
                keep = row < SUBLANES - sh
                shift = SUBLANES - sh
            else:
                keep = row >= sh
                shift = sh
            a_sh = pltpu.roll(a, shift, 0)
            b_sh = pltpu.roll(b, shift, 0)
            b = jnp.where(keep, a * b_sh + b, b)
            a = jnp.where(keep, a * a_sh, a)
        h = b + a * carry
        h_refs[d][0, pl.ds(r0, SUBLANES), :] = h
        last = h[0:1, :] if reverse else h[SUBLANES - 1:SUBLANES, :]
        return jnp.broadcast_to(last, (SUBLANES, C_WIDTH))

    def group(g, carry):
        return one(0, g, carry[0]), one(1, g, carry[1])

    cf, cb = lax.fori_loop(0, n, group, (carry_refs[0][...], carry_refs[1][...]))
    carry_refs[0][...] = cf
    carry_refs[1][...] = cb


def _rglru_body(xf_ref, xfp_ref, xfn_ref, xb_ref, xbp_ref, xbn_ref,
                cw_ref, cb_ref, wf_ref, wb_ref, bf_ref, bb_ref, lam_ref,
                hf_ref, hb_ref, ext_f, ext_b, au_s, carry_f, carry_b, *, ts, nt):
    i = pl.program_id(1)

    @pl.when(i == 0)
    def _():
        carry_f[...] = jnp.zeros_like(carry_f)
        carry_b[...] = jnp.zeros_like(carry_b)

    def gates(d, ext, main, prev, nxt, tile, w_ref, b_ref):
        _fill_ext(ext, prev, main, nxt, tile == 0, tile == nt - 1, ts)
        xc = _conv4(ext, cw_ref, ts) + cb_ref[...]
        g = _dot(xc.astype(BF16), w_ref[...]) + b_ref[...]
        r = _sigmoid(g[:, 0:C_WIDTH])
        gi = _sigmoid(g[:, C_WIDTH:])
        log_a = (-C_GATE_C * _softplus(-lam_ref[d:d + 1, :])) * r
        a = jnp.exp(log_a)
        au_s[2 * d] = a
        au_s[2 * d + 1] = jnp.sqrt(1.0 - a * a) * (gi * xc)

    gates(0, ext_f, xf_ref, xfp_ref, xfn_ref, i, wf_ref, bf_ref)
    gates(1, ext_b, xb_ref, xbp_ref, xbn_ref, nt - 1 - i, wb_ref, bb_ref)
    _scan_tiles(au_s, (hf_ref, hb_ref), (carry_f, carry_b), ts)


def _rglru(cx, w, *, ts=512):
    bsz, seq, _ = cx.shape
    ts = min(ts, seq)
    nt = seq // ts
    fm, fp, fn = _halo_specs(ts, C_WIDTH, nt, False)
    bm, bp, bn = _halo_specs(ts, C_WIDTH, nt, True)
    consts = [w["c_conv_w"], w["c_conv_b"], w["c_wf"], w["c_wb"], w["c_bf"], w["c_bb"], w["c_lam"]]
    out = jax.ShapeDtypeStruct((bsz, seq, C_WIDTH), F32)
    return pl.pallas_call(
        functools.partial(_rglru_body, ts=ts, nt=nt),
        grid=(bsz, nt),
        in_specs=[fm, fp, fn, bm, bp, bn] + [_const_spec(c.shape) for c in consts],
        out_specs=[fm, bm],
        out_shape=[out, out],
        scratch_shapes=[pltpu.VMEM((ts + 2 * SUBLANES, C_WIDTH), F32), pltpu.VMEM((ts + 2 * SUBLANES, C_WIDTH), F32),
                        pltpu.VMEM((4, ts, C_WIDTH), F32), pltpu.VMEM((SUBLANES, C_WIDTH), F32),
                        pltpu.VMEM((SUBLANES, C_WIDTH), F32)],
        compiler_params=_params(("parallel", "arbitrary")),
        name="rglru",
    )(cx, cx, cx, cx, cx, cx, *consts)


def _d_prep_body(x_ref, xp_ref, xn_ref, ab_ref, cw_ref, dconst_ref, q_ref, k_ref, v_ref, gb_ref, ext, *, ts, nt):
    i = pl.program_id(1)
    _fill_ext(ext, xp_ref, x_ref, xn_ref, i == 0, i == nt - 1, ts)
    y = _silu(_conv4(ext, cw_ref, ts))
    for h in range(D_HEADS):
        for j, ref in ((0, q_ref), (1, k_ref)):
            t = y[:, j * D_WIDTH + h * LANES:j * D_WIDTH + (h + 1) * LANES]
            ref[0, :, h * LANES:(h + 1) * LANES] = t * lax.rsqrt(jnp.sum(t * t, axis=-1, keepdims=True) + NORM_EPS)
    v_ref[0] = y[:, 2 * D_WIDTH:]
    ab = ab_ref[0]
    lane = lax.broadcasted_iota(jnp.int32, ab.shape, 1)
    g = -jnp.exp(dconst_ref[0:1, :]) * _softplus(ab + dconst_ref[1:2, :])
    gb_ref[0] = jnp.where(lane < 2 * D_HEADS, g, _sigmoid(ab))


def _d_prep(dqkv, dab, w, *, ts=256):
    bsz, seq, width = dqkv.shape
    ts = min(ts, seq)
    nt = seq // ts
    main, prev, nxt = _halo_specs(ts, width, nt, False)
    tok = lambda wd: pl.BlockSpec((1, ts, wd), lambda b, i: (b, i, 0))
    o512 = jax.ShapeDtypeStruct((bsz, seq, D_WIDTH), F32)
    return pl.pallas_call(
        functools.partial(_d_prep_body, ts=ts, nt=nt),
        grid=(bsz, nt),
        in_specs=[main, prev, nxt, tok(LANES), _const_spec(w["d_conv_w"].shape), _const_spec(w["d_const"].shape)],
        out_specs=[tok(D_WIDTH), tok(D_WIDTH), tok(D_WIDTH), tok(LANES)],
        out_shape=[o512, o512, o512, jax.ShapeDtypeStruct((bsz, seq, LANES), F32)],
        scratch_shapes=[pltpu.VMEM((ts + 2 * SUBLANES, width), F32)],
        compiler_params=_params(("parallel", "parallel")),
        name="d_prep",
    )(dqkv, dqkv, dqkv, dab, w["d_conv_w"], w["d_const"])


def _d_chunk_body(qf_ref, kf_ref, vf_ref, gf_ref, qb_ref, kb_ref, vb_ref, gbk_ref,
                  of_ref, ob_ref, s_ref, gc_s, u_s, wq_s, ak_s, eg_s, *, ts):
    i = pl.program_id(1)

    @pl.when(i == 0)
    def _():
        s_ref[...] = jnp.zeros_like(s_ref)

    c = D_CHUNK
    nc = ts // c
    scale = D_HEAD_DIM ** -0.5
    row_t = lax.broadcasted_iota(jnp.int32, (ts, ts), 0)
    col_t = lax.broadcasted_iota(jnp.int32, (ts, ts), 1)
    same = (row_t // c) == (col_t // c)
    ri = lax.broadcasted_iota(jnp.int32, (c, c), 0)
    ci = lax.broadcasted_iota(jnp.int32, (c, c), 1)
    lane = lax.broadcasted_iota(jnp.int32, (c, LANES), 1)
    ones3 = jnp.where(lane < 3, 1.0, 0.0).astype(BF16)

    dirs = ((0, qf_ref, kf_ref, vf_ref, gf_ref, of_ref), (1, qb_ref, kb_ref, vb_ref, gbk_ref, ob_ref))
    for d, q_ref, k_ref, v_ref, g_ref, o_ref in dirs:
        rev = d == 1
        before = (col_t >= row_t) if rev else (col_t <= row_t)
        tri = jnp.where(same & before, 1.0, 0.0).astype(BF16)
        p1, p2, p3 = _split3(g_ref[0])
        gc_s[d] = _dot(tri, p1) + _dot(tri, p2) + _dot(tri, p3)

    group = 4

    def prepare(gstep, carry):
        chains = []
        for d, q_ref, k_ref, v_ref, g_ref, o_ref in dirs:
            rev = d == 1
            for gi in range(group):
                cc = gstep * group + gi
                r0 = pl.multiple_of(cc * c, c)
                gcb = gc_s[d, pl.ds(r0, c), :]
                gbb = g_ref[0, pl.ds(r0, c), :]
                for h in range(D_HEADS):
                    sl = slice(h * LANES, (h + 1) * LANES)
                    col = d * D_HEADS + h
                    ch = dict(slot=(d * nc + cc) * D_HEADS + h,
                              incl=(ci >= ri) if rev else (ci <= ri), strict=(ci > ri) if rev else (ci < ri))
                    gcol = jnp.sum(jnp.where(lane == col, gcb, 0.0), axis=-1, keepdims=True)
                    ch["beta"] = jnp.sum(jnp.where(lane == 2 * D_HEADS + col, gbb, 0.0), axis=-1, keepdims=True)
                    ch["gcol"] = gcol
                    ch["gtot"] = gcol[0:1, :] if rev else gcol[c - 1:c, :]
                    ch["q"] = q_ref[0, pl.ds(r0, c), sl]
                    ch["k"] = k_ref[0, pl.ds(r0, c), sl]
                    ch["v"] = v_ref[0, pl.ds(r0, c), sl]
                    chains.append(ch)
        for ch in chains:
            g1, g2, g3 = (p.astype(F32) for p in _split3(ch["gcol"]))
            pieces = jnp.where(lane == 0, g1, jnp.where(lane == 1, g2, jnp.where(lane == 2, g3, 0.0)))
            ch["grow"] = _dot_nt(ones3, pieces.astype(BF16))
            ch["kbeta"] = ch["k"] * ch["beta"]
            kb16 = ch["k"].astype(BF16)
            ch["kk"] = _dot_nt(ch["kbeta"].astype(BF16), kb16)
            ch["qk"] = _dot_nt((ch["q"] * scale).astype(BF16), kb16)
        for ch in chains:
            decay = jnp.exp(jnp.where(ch["incl"], ch["gcol"] - ch["grow"], NEG_INF))
            lmat = jnp.where(ch["strict"], ch["kk"] * decay, 0.0)
            ch["attn"] = jnp.where(ch["incl"], ch["qk"] * decay, 0.0).astype(BF16)
            ch["nmat"] = -lmat
            x16 = lmat.astype(BF16)
            ch["xpow"] = _dot(x16, x16)
        for it in range(5):
            for ch in chains:
                x16 = ch["xpow"].astype(BF16)
                if it < 4:
                    both = _dot(jnp.concatenate([ch["nmat"].astype(BF16), x16], axis=0), x16)
                    ch["nmat"] = ch["nmat"] + ch["xpow"] + both[0:c]
                    ch["xpow"] = both[c:]
                else:
                    ch["nmat"] = ch["nmat"] + ch["xpow"] + _dot(ch["nmat"].astype(BF16), x16)
        for ch in chains:
            n16 = ch["nmat"].astype(BF16)
            vb = ch["v"] * ch["beta"]
            wk = ch["kbeta"] * jnp.exp(ch["gcol"])
            slot = ch["slot"]
            solved = _dot(n16, jnp.concatenate([vb.astype(BF16), wk.astype(BF16)], axis=1))
            u_s[slot] = vb + solved[:, 0:D_HEAD_DIM]
            wq_s[slot, 0:c] = (wk + solved[:, D_HEAD_DIM:]).astype(BF16)
            wq_s[slot, c:] = (ch["q"] * (scale * jnp.exp(ch["gcol"]))).astype(BF16)
            ak_s[slot, 0:c] = ch["attn"]
            ak_s[slot, c:] = (ch["k"] * jnp.exp(ch["gtot"] - ch["gcol"])).T.astype(BF16)
            eg_s[slot] = jnp.broadcast_to(jnp.exp(ch["gtot"]), (1, LANES))
        return carry

    lax.fori_loop(0, nc // group, prepare, 0)

    def advance(step, carry):
        chains = []
        for d, q_ref, k_ref, v_ref, g_ref, o_ref in dirs:
            cc = (nc - 1 - step) if d == 1 else step
            for h in range(D_HEADS):
                chains.append(dict(slot=(d * nc + cc) * D_HEADS + h, col=d * D_HEADS + h, o_ref=o_ref,
                                   r0=pl.multiple_of(cc * c, c), sl=slice(h * LANES, (h + 1) * LANES)))
        for ch in chains:
            ch["state"] = s_ref[ch["col"]]
            ch["ws_qs"] = _dot(wq_s[ch["slot"]], ch["state"].astype(BF16))
        for ch in chains:
            vn16 = (u_s[ch["slot"]] - ch["ws_qs"][0:c]).astype(BF16)
            ch["av_kv"] = _dot(ak_s[ch["slot"]], vn16)
        for ch in chains:
            ch["o_ref"][0, pl.ds(ch["r0"], c), ch["sl"]] = ch["ws_qs"][c:] + ch["av_kv"][0:c]
            s_ref[ch["col"]] = ch["state"] * eg_s[ch["slot"]] + ch["av_kv"][c:]
        return carry

    lax.fori_loop(0, nc, advance, 0)


def _d_chunk(q, k, v, gb, *, ts=256):
    bsz, seq, _ = q.shape
    ts = min(ts, seq)
    nt = seq // ts
    fwd = lambda wd: pl.BlockSpec((1, ts, wd), lambda b, i: (b, i, 0))
    bwd = lambda wd: pl.BlockSpec((1, ts, wd), lambda b, i: (b, nt - 1 - i, 0))
    out = jax.ShapeDtypeStruct((bsz, seq, D_WIDTH), F32)
    nslot = 2 * (ts // D_CHUNK) * D_HEADS
    return pl.pallas_call(
        functools.partial(_d_chunk_body, ts=ts),
        grid=(bsz, nt),
        in_specs=[fwd(D_WIDTH), fwd(D_WIDTH), fwd(D_WIDTH), fwd(LANES),
                  bwd(D_WIDTH), bwd(D_WIDTH), bwd(D_WIDTH), bwd(LANES)],
        out_specs=[fwd(D_WIDTH), bwd(D_WIDTH)],
        out_shape=[out, out],
        scratch_shapes=[pltpu.VMEM((2 * D_HEADS, D_HEAD_DIM, D_HEAD_DIM), F32), pltpu.VMEM((2, ts, LANES), F32),
                        pltpu.VMEM((nslot, D_CHUNK, D_HEAD_DIM), F32),
                        pltpu.VMEM((nslot, 2 * D_CHUNK, D_HEAD_DIM), BF16),
                        pltpu.VMEM((nslot, D_CHUNK + D_HEAD_DIM, D_CHUNK), BF16),
                        pltpu.VMEM((nslot, 1, LANES), F32)],
        compiler_params=_params(("parallel", "arbitrary")),
        name="d_chunk",
    )(q, k, v, gb, q, k, v, gb)


def _gelu_tanh(x):
    return 0.5 * x * (1.0 + jnp.tanh(np.sqrt(2.0 / np.pi).astype(np.float32) * (x + 0.044715 * (x * x * x))))


def _merge_body(x_ref, o1_ref, l1_ref, o4_ref, l4_ref, o16_ref, l16_ref, ob_ref,
                hf_ref, hb_ref, cg_ref, df_ref, db_ref, dz_ref, gate_ref,
                wbr_ref, wout_ref, ong_ref, ln2_ref, wgu_ref, wdn_ref, y_ref):
    slabs = []
    for c in range(A_WIDTH // LANES):
        l1, l4, l16 = l1_ref[0, c], l4_ref[0, c], l16_ref[0, c]
        m = jnp.maximum(jnp.maximum(l1, l4), l16)
        e1, e4, e16 = jnp.exp2(l1 - m), jnp.exp2(l4 - m), jnp.exp2(l16 - m)
        slabs.append((e1 * o1_ref[0, c] + e4 * o4_ref[0, c] + e16 * o16_ref[0, c]) / (e1 + e4 + e16))
    oa = jnp.concatenate(slabs, axis=1)
    oc = (hf_ref[0] + hb_ref[0]) * _gelu_tanh(cg_ref[0].astype(F32))
    od_sum = df_ref[0] + db_ref[0]
    z = dz_ref[0].astype(F32)
    heads = []
    for h in range(D_HEADS):
        sl = slice(h * LANES, (h + 1) * LANES)
        heads.append(_rms_rows(od_sum[:, sl], ong_ref[...]) * _silu(z[:, sl]))
    od = jnp.concatenate(heads, axis=1)

    def gate(j):
        return _sigmoid(gate_ref[0, :, j * D_MODEL:(j + 1) * D_MODEL]).astype(F32)

    r_a, r_b, r_c = A_WIDTH, A_WIDTH + B_HEADS * LANES, A_WIDTH + B_HEADS * LANES + C_WIDTH
    merged = gate(0) * _dot(oa.astype(BF16), wbr_ref[0:r_a, :])
    merged = merged + gate(1) * _dot(ob_ref[0], wbr_ref[r_a:r_b, :])
    merged = merged + gate(2) * _dot(oc.astype(BF16), wbr_ref[r_b:r_c, :])
    merged = merged + gate(3) * _dot(od.astype(BF16), wbr_ref[r_c:, :])
    x1 = x_ref[0] + _dot(merged.astype(BF16), wout_ref[...])
    xn = _rms_rows(x1, ln2_ref[...]).astype(BF16)
    acc = x1
    for j in range(FF_DIM // FF_TILE):
        g = _dot(xn, wgu_ref[:, j * FF_TILE:(j + 1) * FF_TILE])
        u = _dot(xn, wgu_ref[:, FF_DIM + j * FF_TILE:FF_DIM + (j + 1) * FF_TILE])
        acc = acc + _dot((_silu(g) * u).astype(BF16), wdn_ref[j * FF_TILE:(j + 1) * FF_TILE, :])
    y_ref[0] = acc


def _merge(x, a_outs, ob, hf, hb, cg, df, db, dz, gates, w, *, tm=256):
    bsz, seq, _ = x.shape
    tok = lambda wd: pl.BlockSpec((1, tm, wd), lambda b, i: (b, i, 0))
    consts = [w["w_branch"], w["w_out"], w["d_on_g"], w["ln2"], w["w_up"], w["w_down"]]
    ins = [x]
    specs = [tok(D_MODEL)]
    slab = pl.BlockSpec((1, A_WIDTH // LANES, tm, LANES), lambda b, i: (b, 0, i, 0))
    for o, lse in a_outs:
        ins += [o, lse]
        specs += [slab, slab]
    ins += [ob, hf, hb, cg, df, db, dz, gates]
    specs += [tok(B_HEADS * LANES), tok(C_WIDTH), tok(C_WIDTH), tok(C_WIDTH), tok(D_WIDTH), tok(D_WIDTH),
              tok(D_WIDTH), tok(4 * D_MODEL)]
    return pl.pallas_call(
        _merge_body,
        grid=(bsz, seq // tm),
        in_specs=specs + [_const_spec(c.shape) for c in consts],
        out_specs=tok(D_MODEL),
        out_shape=jax.ShapeDtypeStruct(x.shape, F32),
        compiler_params=_params(("parallel", "parallel")),
        name="merge",
    )(*ins, *consts)


FF_TILE = 256


def _block_diag(wblocks):
    g, i, o = wblocks.shape
    eye = jnp.eye(g, dtype=wblocks.dtype)
    return (eye[:, None, :, None] * wblocks[:, :, None, :]).reshape(g * i, g * o)


def _rope_offsets(consts, rows):
    ang = jnp.arange(rows, dtype=F32)[:, None] * consts[0][None, :]
    return jnp.stack([jnp.cos(ang), jnp.sin(ang)], axis=0)


def _rope_consts(head_period, rope_start, rope_dim):
    lane = np.arange(LANES) % head_period
    j = lane - rope_start
    in_rope = (j >= 0) & (j < rope_dim)
    half = rope_dim // 2
    inv_freq = 1.0 / (ROPE_THETA ** (jnp.arange(0, rope_dim, 2, dtype=F32) / rope_dim))
    inv_lane = jnp.where(jnp.asarray(in_rope), inv_freq[np.where(in_rope, j % half, 0)], 0.0)
    out = jnp.zeros((SUBLANES, LANES), F32)
    out = out.at[0].set(inv_lane)
    out = out.at[1].set(jnp.asarray(np.where(in_rope & (j < half), -1.0, 0.0), F32))
    out = out.at[2].set(jnp.asarray(np.where(in_rope & (j >= half), 1.0, 0.0), F32))
    return out


def _pad_heads(wmat, heads, dim):
    r = wmat.shape[0]
    w3 = wmat.reshape(r, heads, dim)
    return jnp.pad(w3, ((0, 0), (0, 0), (0, LANES - dim))).reshape(r, heads * LANES)


def _pad_lane_vec(vec, dim):
    return jnp.pad(vec, (0, LANES - dim)).reshape(1, LANES)


def _layer_weights(l, ln1_g, w_in, a_qn_g, a_kn_g, b_qa_g, b_wuq, b_kva_g, b_wukv, b_qn_g, b_kn_g,
                   c_conv_w, c_conv_b, c_wr, c_br, c_wi, c_bi, c_lam, d_conv_w, d_a_log, d_dt_bias, d_on_g,
                   w_branch, w_out, ln2_g, w_up, w_down):
    wi = w_in[l]
    row = lambda v: v.reshape(1, -1)
    w = {}
    w["ln1"] = row(ln1_g[l])
    w["wa"] = wi[:, O_AQ:O_BCQ].astype(BF16)
    zeros = lambda n: jnp.zeros((D_MODEL, n), F32)
    w["wb"] = jnp.concatenate([wi[:, O_BCQ:O_BKR], zeros(B_NOPE_DIM), wi[:, O_BKR:O_CX],
                               zeros(LANES - B_QK_DIM)], axis=1).astype(BF16)
    w["wc"] = wi[:, O_CX:O_DQ].astype(BF16)
    w["wd"] = wi[:, O_DQ:O_DA].astype(BF16)
    w["wab"] = jnp.concatenate([wi[:, O_DA:O_GATE], zeros(LANES - 4 * D_HEADS)], axis=1).astype(BF16)
    w["wg"] = wi[:, O_GATE:O_END].astype(BF16)
    w["aqg"] = row(jnp.tile(a_qn_g[l], A_HEADS))
    w["akg"] = row(jnp.tile(a_kn_g[l], A_HEADS))
    w["arope"] = _rope_consts(A_HEAD_DIM, 0, A_ROPE_DIM)
    w["aoff"] = _rope_offsets(w["arope"], IN_TILE)
    w["qag"] = row(b_qa_g[l])
    w["wuq"] = _pad_heads(b_wuq[l], B_HEADS, B_QK_DIM).astype(BF16)
    w["kvag"] = row(b_kva_g[l])
    wukv = b_wukv[l].reshape(B_KV_RANK, B_HEADS, B_NOPE_DIM + B_V_DIM)
    w["wukv"] = jnp.concatenate([
        _pad_heads(wukv[:, :, :B_NOPE_DIM].reshape(B_KV_RANK, -1), B_HEADS, B_NOPE_DIM),
        _pad_heads(wukv[:, :, B_NOPE_DIM:].reshape(B_KV_RANK, -1), B_HEADS, B_V_DIM)], axis=1).astype(BF16)
    w["bqg"] = _pad_lane_vec(b_qn_g[l], B_QK_DIM)
    w["bkg"] = _pad_lane_vec(b_kn_g[l], B_QK_DIM)
    w["brope"] = _rope_consts(LANES, B_NOPE_DIM, B_ROPE_DIM)
    w["boff"] = _rope_offsets(w["brope"], IN_TILE)

    w["c_conv_w"] = jnp.pad(c_conv_w[l], ((0, SUBLANES - 4), (0, 0)))
    w["c_conv_b"] = row(c_conv_b[l])
    w["c_wf"] = jnp.concatenate([_block_diag(c_wr[l, 0]), _block_diag(c_wi[l, 0])], axis=1).astype(BF16)
    w["c_wb"] = jnp.concatenate([_block_diag(c_wr[l, 1]), _block_diag(c_wi[l, 1])], axis=1).astype(BF16)
    w["c_bf"] = row(jnp.concatenate([c_br[l, 0], c_bi[l, 0]]))
    w["c_bb"] = row(jnp.concatenate([c_br[l, 1], c_bi[l, 1]]))
    w["c_lam"] = jnp.pad(c_lam[l], ((0, SUBLANES - 2), (0, 0)))

    w["d_conv_w"] = jnp.pad(d_conv_w[l], ((0, SUBLANES - 4), (0, 0)))
    dconst = jnp.zeros((SUBLANES, LANES), F32)
    dconst = dconst.at[0, 0:2 * D_HEADS].set(d_a_log[l].reshape(-1))
    dconst = dconst.at[1, 0:2 * D_HEADS].set(d_dt_bias[l].reshape(-1))
    w["d_const"] = dconst
    w["d_on_g"] = row(d_on_g[l])

    wbr = w_branch[l]
    wbr_b = jnp.pad(wbr[A_WIDTH:A_WIDTH + B_HEADS * B_V_DIM].reshape(B_HEADS, B_V_DIM, D_MODEL),
                    ((0, 0), (0, LANES - B_V_DIM), (0, 0))).reshape(B_HEADS * LANES, D_MODEL)
    w["w_branch"] = jnp.concatenate([wbr[:A_WIDTH], wbr_b, wbr[A_WIDTH + B_HEADS * B_V_DIM:]], axis=0).astype(BF16)
    w["w_out"] = w_out[l].astype(BF16)
    w["ln2"] = row(ln2_g[l])
    w["w_up"] = w_up[l].astype(BF16)
    w["w_down"] = w_down[l].astype(BF16)
    return w


def _layer(x, w):
    aqs, aks, avs, bq, bk, bv, cx, cg, dqkv, dz, dab, gates = _in_proj(x, w)
    a_outs = [_a_attn(q, k, v, dil) for q, k, v, dil in zip(aqs, aks, avs, A_DILATIONS)]
    ob = _b_flash(bq, bk, bv)
    hf, hb = _rglru(cx, w)
    dq, dk, dv, dgb = _d_prep(dqkv, dab, w)
    df, db = _d_chunk(dq, dk, dv, dgb)
    return _merge(x, a_outs, ob, hf, hb, cg, df, db, dz, gates, w)


def kernel(x_prompt, x_sample, ln1_g, w_in, a_qn_g, a_kn_g, b_qa_g, b_wuq, b_kva_g, b_wukv, b_qn_g, b_kn_g,
           c_conv_w, c_conv_b, c_wr, c_br, c_wi, c_bi, c_lam, d_conv_w, d_a_log, d_dt_bias, d_on_g,
           w_branch, w_out, ln2_g, w_up, w_down):
    depth = ln1_g.shape[0]
    weights = [_layer_weights(l, ln1_g, w_in, a_qn_g, a_kn_g, b_qa_g, b_wuq, b_kva_g, b_wukv, b_qn_g, b_kn_g,
                              c_conv_w, c_conv_b, c_wr, c_br, c_wi, c_bi, c_lam, d_conv_w, d_a_log, d_dt_bias,
                              d_on_g, w_branch, w_out, ln2_g, w_up, w_down) for l in range(depth)]

    def trunk(x):
        for w in weights:
            x = _layer(x, w)
        return x

    return (trunk(x_prompt), trunk(x_sample))
```
